```python
import math
import jax, jax.numpy as jnp
from jax import lax
import numpy as np

D_MODEL = 1024
BATCH = 8
SEQ = 4096
DEPTH = 1

HEAD_DIM = 64
N_Q_HEADS = 8
N_KV_HEADS = 2
GQ = N_Q_HEADS // N_KV_HEADS
WINDOW = 128
BLOCK = 128
ROPE_THETA = 10000.0
D_A = N_Q_HEADS * HEAD_DIM
D_KV = N_KV_HEADS * HEAD_DIM

CHUNK = 128
N_GROUPS = 4
D_B = D_MODEL // 2
GROUP_W = D_B // N_GROUPS

EPS = 1e-6
NEG = -1e30

SPLIT_SIZES = [D_A, D_KV, D_KV, D_A,
               D_B, D_B, D_B,
               2 * D_MODEL]
SPLIT_IDX = [int(s) for s in np.cumsum(SPLIT_SIZES)[:-1]]
D_IN = int(sum(SPLIT_SIZES))

kernel_name = "hybrid_swa_sink_gmlp_gated_block"


def _rms(x, g):
    xf = x.astype(jnp.float32)
    y = xf * lax.rsqrt(jnp.mean(xf * xf, axis=-1, keepdims=True) + EPS)
    return (y * g.astype(jnp.float32)).astype(x.dtype)


def _layernorm(x, g, b):
    xf = x.astype(jnp.float32)
    mu = jnp.mean(xf, axis=-1, keepdims=True)
    var = jnp.mean(jnp.square(xf - mu), axis=-1, keepdims=True)
    y = (xf - mu) * lax.rsqrt(var + EPS)
    return (y * g.astype(jnp.float32) + b.astype(jnp.float32)).astype(x.dtype)


def _rope(t, positions):
    half = HEAD_DIM // 2
    inv_freq = ROPE_THETA ** (-jnp.arange(half, dtype=jnp.float32) / half)
    ang = positions.astype(jnp.float32)[..., None] * inv_freq
    cos = jnp.cos(ang)[:, :, None, :]
    sin = jnp.sin(ang)[:, :, None, :]
    tf = t.astype(jnp.float32)
    t1, t2 = tf[..., :half], tf[..., half:]
    out = jnp.concatenate([t1 * cos - t2 * sin, t2 * cos + t1 * sin], axis=-1)
    return out.astype(t.dtype)


def _band(t):
    b, s, h, d = t.shape
    tb = t.reshape(b, s // BLOCK, BLOCK, h, d)
    prev = jnp.pad(tb[:, :-1], ((0, 0), (1, 0), (0, 0), (0, 0), (0, 0)))
    return jnp.concatenate([prev, tb], axis=2)


def _swa_with_sinks(q, k, v, sinks):
    b, s = q.shape[:2]
    nb = s // BLOCK
    qb = q.reshape(b, nb, BLOCK, N_KV_HEADS, GQ, HEAD_DIM)
    kb, vb = _band(k), _band(v)
    scores = jnp.einsum('bnqkgd,bnjkd->bnkgqj', qb, kb).astype(jnp.float32)
    scores = scores * (1.0 / math.sqrt(HEAD_DIM))
    qi = jnp.arange(BLOCK)[:, None]
    kj = jnp.arange(2 * BLOCK)[None, :]
    rel = qi + BLOCK - kj
    in_win = (rel >= 0) & (rel < WINDOW)
    blk = jnp.arange(nb)[:, None, None]
    valid = in_win[None] & ((blk > 0) | (kj[None] >= BLOCK))
    scores = jnp.where(valid[None, :, None, None], scores, NEG)
    sink = sinks.astype(jnp.float32).reshape(N_KV_HEADS, GQ)[None, None, :, :, None, None]
    sink = jnp.broadcast_to(sink, scores.shape[:-1] + (1,))
    probs = jax.nn.softmax(jnp.concatenate([scores, sink], axis=-1), axis=-1)[..., :-1]
    out = jnp.einsum('bnkgqj,bnjkd->bnqkgd', probs.astype(v.dtype), vb)
    return out.reshape(b, s, N_Q_HEADS * HEAD_DIM)


def _chunked_spatial_gating(u, v, ln_g, ln_b, w_s, b_s):
    b, s, _ = v.shape
    nc = s // CHUNK
    vn = _layernorm(v, ln_g, ln_b).reshape(b, nc, CHUNK, N_GROUPS, GROUP_W)
    causal = jnp.tril(jnp.ones((CHUNK, CHUNK), dtype=bool))
    w = jnp.where(causal[None], w_s, jnp.zeros((), w_s.dtype))
    sv = jnp.einsum('gts,bnsgc->bntgc', w, vn) + b_s.T[None, None, :, :, None]
    return u * sv.reshape(b, s, D_B)


def setup_inputs(seed: int = 0) -> dict:
    key = jax.random.key(seed)
    ks = jax.random.split(key, 18)
    f32 = jnp.float32
    nrm = lambda k, shape, s: jax.random.normal(k, shape, f32) * s
    x = jax.random.normal(ks[0], (BATCH, SEQ, D_MODEL), f32)
    c = jax.random.normal(ks[1], (BATCH, D_MODEL), f32)
    positions = jnp.broadcast_to(jnp.arange(SEQ, dtype=jnp.int32), (BATCH, SEQ))
    return {
        "x": x,
        "c": c,
        "positions": positions,
        "w_ada": nrm(ks[2], (DEPTH, D_MODEL, 3 * D_MODEL), 0.5 * D_MODEL ** -0.5),
        "b_ada": nrm(ks[3], (DEPTH, 3 * D_MODEL), 0.02),
        "g_pre": 1.0 + nrm(ks[4], (DEPTH, D_MODEL), 0.05),
        "g_post": 1.0 + nrm(ks[5], (DEPTH, D_MODEL), 0.05),
        "w_in": nrm(ks[6], (DEPTH, D_MODEL, D_IN), D_MODEL ** -0.5),
        "sinks": nrm(ks[7], (DEPTH, N_Q_HEADS), 1.0),
        "ln_v_g": 1.0 + nrm(ks[8], (DEPTH, D_B), 0.05),
        "ln_v_b": nrm(ks[9], (DEPTH, D_B), 0.02),
        "w_s": nrm(ks[10], (DEPTH, N_GROUPS, CHUNK, CHUNK), CHUNK ** -0.5),
        "b_s": 1.0 + nrm(ks[11], (DEPTH, N_GROUPS, CHUNK), 0.1),
        "w_proj_a": nrm(ks[12], (DEPTH, D_A, D_MODEL), D_A ** -0.5),
        "w_proj_b": nrm(ks[13], (DEPTH, D_B, D_MODEL), D_B ** -0.5),
        "w_out": nrm(ks[14], (DEPTH, D_MODEL, D_MODEL), D_MODEL ** -0.5),
    }


def reference(x, c, positions, w_ada, b_ada, g_pre, g_post, w_in, sinks,
              ln_v_g, ln_v_b, w_s, b_s, w_proj_a, w_proj_b, w_out):
    b, s, _ = x.shape
    c_act = jax.nn.silu(c)
    for l in range(DEPTH):
        ada = (c_act @ w_ada[l] + b_ada[l])[:, None, :]
        shift, scale, gate = jnp.split(ada, 3, axis=-1)
        h = _rms(x, g_pre[l]) * (1.0 + scale) + shift

        proj = h @ w_in[l]
        q, k, v, z_a, u_b, v_b, z_b, g_logits = jnp.split(proj, SPLIT_IDX, axis=-1)

        q = _rope(q.reshape(b, s, N_Q_HEADS, HEAD_DIM), positions)
        k = _rope(k.reshape(b, s, N_KV_HEADS, HEAD_DIM), positions)
        v = v.reshape(b, s, N_KV_HEADS, HEAD_DIM)
        y_a = _swa_with_sinks(q, k, v, sinks[l]) * jax.nn.silu(z_a)

        y_b = _chunked_spatial_gating(jax.nn.gelu(u_b, approximate=False),
                                      jax.nn.gelu(v_b, approximate=False),
                                      ln_v_g[l], ln_v_b[l], w_s[l], b_s[l])
        y_b = y_b * jax.nn.silu(z_b)

        gates = jax.nn.sigmoid(g_logits)
        gate_a, gate_b = jnp.split(gates, 2, axis=-1)
        merged = gate_a * (y_a @ w_proj_a[l]) + gate_b * (y_b @ w_proj_b[l])
        y = merged @ w_out[l]

        x = x + gate * _rms(y, g_post[l])
    return x
```

```python
import functools
import math

import jax
import jax.numpy as jnp
import numpy as np
from jax import lax
from jax.experimental import pallas as pl
from jax.experimental.pallas import tpu as pltpu

HEAD_DIM = 64
N_Q_HEADS = 8
N_KV_HEADS = 2
GQ = N_Q_HEADS // N_KV_HEADS
WINDOW = 128
BLOCK = 128
ROPE_THETA = 10000.0
CHUNK = 128
N_GROUPS = 4
EPS = 1e-6
NEG = -1e30

LANES = 128
SEQ_TILE = 512
VMEM_LIMIT_BYTES = 56 * 1024 * 1024

BF16 = jnp.bfloat16
F32 = jnp.float32


def _ada_kernel(c_ref, w_ref, b_ref, o_ref):
    c = c_ref[...]
    c_act = c * jax.nn.sigmoid(c)
    o_ref[...] = jnp.dot(c_act, w_ref[...], preferred_element_type=F32) + b_ref[...]


def _silu(x):
    return x * jax.nn.sigmoid(x)


def _gelu_exact(x):
    return 0.5 * x * (1.0 + lax.erf(x * (1.0 / math.sqrt(2.0))))


def _rope(t, cos, sin_signed, first_half):
    outs = []
    for c in range(t.shape[1] // LANES):
        tc = t[:, c * LANES:(c + 1) * LANES]
        rot = jnp.where(first_half,
                        pltpu.roll(tc, LANES - HEAD_DIM // 2, axis=1),
                        pltpu.roll(tc, HEAD_DIM // 2, axis=1))
        outs.append(tc * cos + rot * sin_signed)
    return outs


def _block_kernel(sinks_ref, x_ref, pos_ref, ada_ref, gpre_ref, gpost_ref, invf_ref,
                  win_ref, lng_ref, lnb_ref, ws_ref, bs_ref, wpa_ref, wpb_ref, wout_ref,
                  o_ref, k_scr, v_scr, ya_scr, yb_scr, *, d_a, d_kv, d_b, d_model):
    T = x_ref.shape[1]
    nblk = T // BLOCK
    s_idx = pl.program_id(1)

    c_q, c_k, c_v = 0, d_a, d_a + d_kv
    c_za = c_v + d_kv
    c_u = c_za + d_a
    c_vb = c_u + d_b
    c_zb = c_vb + d_b
    c_g = c_zb + d_b

    @pl.when(s_idx == 0)
    def _():
        k_scr[:, 0:BLOCK, :] = jnp.zeros((N_KV_HEADS, BLOCK, HEAD_DIM), BF16)
        v_scr[:, 0:BLOCK, :] = jnp.zeros((N_KV_HEADS, BLOCK, HEAD_DIM), BF16)

    x = x_ref[0]
    ada = ada_ref[0]
    shift, scale, gate = ada[0:1], ada[1:2], ada[2:3]

    ms = jnp.mean(x * x, axis=-1, keepdims=True)
    h = (x * lax.rsqrt(ms + EPS)) * gpre_ref[...] * (1.0 + scale) + shift
    hb = h.astype(BF16)

    def proj(lo, hi):
        return jnp.dot(hb, win_ref[:, lo:hi], preferred_element_type=F32)

    pos = pos_ref[0].astype(F32)
    ang = pos * invf_ref[...]
    lane = lax.broadcasted_iota(jnp.int32, (1, LANES), 1)
    first_half = (lane % HEAD_DIM) < (HEAD_DIM // 2)
    cos = jnp.cos(ang)
    sin = jnp.sin(ang)
    sin_signed = jnp.where(first_half, -sin, sin)

    q = proj(c_q, c_k)
    kv = proj(c_k, c_za)
    q_tiles = _rope(q, cos, sin_signed, first_half)
    k_r = _rope(kv[:, 0:d_kv], cos, sin_signed, first_half)[0]
    v_new = kv[:, d_kv:2 * d_kv]
    for g in range(N_KV_HEADS):
        k_scr[g, BLOCK:BLOCK + T, :] = k_r[:, g * HEAD_DIM:(g + 1) * HEAD_DIM].astype(BF16)
        v_scr[g, BLOCK:BLOCK + T, :] = v_new[:, g * HEAD_DIM:(g + 1) * HEAD_DIM].astype(BF16)

    sm_scale = 1.0 / math.sqrt(HEAD_DIM)
    q_heads = []
    for c in range(len(q_tiles)):
        qs = (q_tiles[c] * sm_scale).astype(BF16)
        q_heads.append(qs[:, 0:HEAD_DIM])
        q_heads.append(qs[:, HEAD_DIM:2 * HEAD_DIM])

    qi = lax.broadcasted_iota(jnp.int32, (BLOCK, 2 * BLOCK), 0)
    kj = lax.broadcasted_iota(jnp.int32, (BLOCK, 2 * BLOCK), 1)
    rel = qi + BLOCK - kj
    in_win = (rel >= 0) & (rel < WINDOW)
    first_lo = jnp.where(s_idx == 0, BLOCK, 0)

    for n in range(nblk):
        valid = in_win & (kj >= first_lo) if n == 0 else in_win
        valid4 = jnp.concatenate([valid] * GQ, axis=0)
        rows = slice(n * BLOCK, (n + 1) * BLOCK)
        band = slice(n * BLOCK, n * BLOCK + 2 * BLOCK)
        for g in range(N_KV_HEADS):
            kb = k_scr[g, band, :]
            vb = v_scr[g, band, :]
            q_stack = jnp.concatenate([q_heads[g * GQ + j][rows] for j in range(GQ)], axis=0)
            s = lax.dot_general(q_stack, kb, (((1,), (1,)), ((), ())),
                                preferred_element_type=F32)
            s = jnp.where(valid4, s, NEG)
            sink = jnp.concatenate(
                [jnp.full((BLOCK, 1), sinks_ref[g * GQ + j], F32) for j in range(GQ)], axis=0)
            m = jnp.maximum(jnp.max(s, axis=-1, keepdims=True), sink)
            p = jnp.exp(s - m)
            denom = jnp.sum(p, axis=-1, keepdims=True) + jnp.exp(sink - m)
            o = jnp.dot(p.astype(BF16), vb, preferred_element_type=F32)
            o = o * (1.0 / denom)
            for j in range(GQ):
                hq = g * GQ + j
                ya_scr[rows, hq * HEAD_DIM:(hq + 1) * HEAD_DIM] = o[j * BLOCK:(j + 1) * BLOCK]

    k_scr[:, 0:BLOCK, :] = k_scr[:, T:T + BLOCK, :]
    v_scr[:, 0:BLOCK, :] = v_scr[:, T:T + BLOCK, :]

    y_a = ya_scr[...] * _silu(proj(c_za, c_u))
    p_a = jnp.dot(y_a.astype(BF16), wpa_ref[...], preferred_element_type=F32)

    u = _gelu_exact(proj(c_u, c_vb))
    v = _gelu_exact(proj(c_vb, c_zb))
    mu = jnp.mean(v, axis=-1, keepdims=True)
    vc = v - mu
    var = jnp.mean(vc * vc, axis=-1, keepdims=True)
    vn = (vc * lax.rsqrt(var + EPS) * lng_ref[...] + lnb_ref[...]).astype(BF16)

    ti = lax.broadcasted_iota(jnp.int32, (CHUNK, CHUNK), 0)
    si = lax.broadcasted_iota(jnp.int32, (CHUNK, CHUNK), 1)
    causal = si <= ti
    group_w = d_b // N_GROUPS
    for g in range(N_GROUPS):
        w_g = jnp.where(causal, ws_ref[g], 0.0).astype(BF16)
        bias = jnp.broadcast_to(bs_ref[:, g:g + 1], (CHUNK, group_w))
        cols = slice(g * group_w, (g + 1) * group_w)
        for n in range(T // CHUNK):
            rows = slice(n * CHUNK, (n + 1) * CHUNK)
            sv = jnp.dot(w_g, vn[rows, cols], preferred_element_type=F32) + bias
            yb_scr[rows, cols] = u[rows, cols] * sv
    y_b = yb_scr[...] * _silu(proj(c_zb, c_g))
    p_b = jnp.dot(y_b.astype(BF16), wpb_ref[...], preferred_element_type=F32)

    gate_a = jax.nn.sigmoid(proj(c_g, c_g + d_model))
    gate_b = jax.nn.sigmoid(proj(c_g + d_model, c_g + 2 * d_model))
    merged = (gate_a * p_a + gate_b * p_b).astype(BF16)
    y = jnp.dot(merged, wout_ref[...], preferred_element_type=F32)
    ms_y = jnp.mean(y * y, axis=-1, keepdims=True)
    o_ref[0] = x + gate * (y * lax.rsqrt(ms_y + EPS) * gpost_ref[...])


def _const_spec(shape):
    return pl.BlockSpec(shape, lambda b, s: (0,) * len(shape), pipeline_mode=pl.Buffered(1))


def _layer(x, ada, positions, g_pre, g_post, w_in, sinks, ln_v_g, ln_v_b, w_s, b_s,
           w_proj_a, w_proj_b, w_out):
    B, S, D = x.shape
    T = SEQ_TILE
    assert S % T == 0 and T % BLOCK == 0 and T % CHUNK == 0
    d_a = N_Q_HEADS * HEAD_DIM
    d_kv = N_KV_HEADS * HEAD_DIM
    d_b = w_proj_b.shape[0]
    d_in = w_in.shape[1]
    assert d_in == 2 * d_a + 2 * d_kv + 3 * d_b + 2 * D

    half = HEAD_DIM // 2
    inv_freq = ROPE_THETA ** (-jnp.arange(half, dtype=F32) / half)
    invf = jnp.tile(inv_freq, LANES // half)[None, :]

    kern = functools.partial(_block_kernel, d_a=d_a, d_kv=d_kv, d_b=d_b, d_model=D)
    return pl.pallas_call(
        kern,
        out_shape=jax.ShapeDtypeStruct((B, S, D), x.dtype),
        grid=(B, S // T),
        in_specs=[
            pl.BlockSpec(memory_space=pltpu.SMEM),
            pl.BlockSpec((1, T, D), lambda b, s: (b, s, 0)),
            pl.BlockSpec((1, T, 1), lambda b, s: (b, s, 0)),
            pl.BlockSpec((1, 3, D), lambda b, s: (b, 0, 0)),
            _const_spec((1, D)),
            _const_spec((1, D)),
            _const_spec((1, LANES)),
            _const_spec((D, d_in)),
            _const_spec((1, d_b)),
            _const_spec((1, d_b)),
            _const_spec((N_GROUPS, CHUNK, CHUNK)),
            _const_spec((CHUNK, N_GROUPS)),
            _const_spec((d_a, D)),
            _const_spec((d_b, D)),
            _const_spec((D, D)),
        ],
        out_specs=pl.BlockSpec((1, T, D), lambda b, s: (b, s, 0)),
        scratch_shapes=[
            pltpu.VMEM((N_KV_HEADS, BLOCK + T, HEAD_DIM), BF16),
            pltpu.VMEM((N_KV_HEADS, BLOCK + T, HEAD_DIM), BF16),
            pltpu.VMEM((T, d_a), F32),
            pltpu.VMEM((T, d_b), F32),
        ],
        compiler_params=pltpu.CompilerParams(
            dimension_semantics=("arbitrary", "arbitrary"),
            vmem_limit_bytes=VMEM_LIMIT_BYTES),
        name="hybrid_block",
    )(sinks, x, positions.reshape(B, S, 1), ada.reshape(B, 3, D),
      g_pre[None, :], g_post[None, :], invf, w_in.astype(BF16),
      ln_v_g[None, :], ln_v_b[None, :], w_s, b_s.T,
      w_proj_a.astype(BF16), w_proj_b.astype(BF16), w_out.astype(BF16))


def _ada(c, w_ada, b_ada):
    B, D = c.shape
    n_out = w_ada.shape[1]
    tn = D
    return pl.pallas_call(
        _ada_kernel,
        out_shape=jax.ShapeDtypeStruct((B, n_out), F32),
        grid=(n_out // tn,),
        in_specs=[
            pl.BlockSpec((B, D), lambda j: (0, 0)),
            pl.BlockSpec((D, tn), lambda j: (0, j)),
            pl.BlockSpec((1, tn), lambda j: (0, j)),
        ],
        out_specs=pl.BlockSpec((B, tn), lambda j: (0, j)),
        compiler_params=pltpu.CompilerParams(dimension_semantics=("arbitrary",)),
        name="adaln_modulation",
    )(c, w_ada, b_ada[None, :])


def kernel(x, c, positions, w_ada, b_ada, g_pre, g_post, w_in, sinks, ln_v_g, ln_v_b, w_s, b_s,
           w_proj_a, w_proj_b, w_out):
    depth = w_in.shape[0]
    for l in range(depth):
        ada = _ada(c, w_ada[l], b_ada[l])
        x = _layer(x, ada, positions, g_pre[l], g_post[l], w_in[l], sinks[l], ln_v_g[l],
                   ln_v_b[l], w_s[l], b_s[l], w_proj_a[l], w_proj_b[l], w_out[l])
    return x
```

```python
import functools
import math

import jax
import jax.numpy as jnp
from jax import lax
from jax.experimental import pallas as pl
from jax.experimental.pallas import tpu as pltpu

HEAD_DIM = 64
N_Q_HEADS = 8
N_KV_HEADS = 2
GQ = N_Q_HEADS // N_KV_HEADS
WINDOW = 128
BLOCK = 128
ROPE_THETA = 10000.0
CHUNK = 128
N_GROUPS = 4
EPS = 1e-6
NEG = -1e30

LANES = 128
SEQ_TILE = 512
VMEM_LIMIT_BYTES = 56 * 1024 * 1024

BF16 = jnp.bfloat16
F32 = jnp.float32


def _ada_kernel(c_ref, w_ref, b_ref, o_ref):
    c = c_ref[...]
    c_act = c * jax.nn.sigmoid(c)
    o_ref[...] = jnp.dot(c_act, w_ref[...], preferred_element_type=F32) + b_ref[...]


def _silu(x):
    return x * jax.nn.sigmoid(x)


def _gelu_exact(x):
    return 0.5 * x * (1.0 + lax.erf(x * (1.0 / math.sqrt(2.0))))


def _rope(t, cos, sin_signed, first_half):
    outs = []
    for c in range(t.shape[1] // LANES):
        tc = t[:, c * LANES:(c + 1) * LANES]
        rot = jnp.where(first_half,
                        pltpu.roll(tc, LANES - HEAD_DIM // 2, axis=1),
                        pltpu.roll(tc, HEAD_DIM // 2, axis=1))
        outs.append(tc * cos + rot * sin_signed)
    return outs


def _block_kernel(sinks_ref, x_ref, pos_ref, ada_ref, gpre_ref, gpost_ref, invf_ref,
                  win_ref, lng_ref, lnb_ref, ws_ref, bs_ref, wpa_ref, wpb_ref, wout_ref,
                  o_ref, k_scr, vt_scr, ya_scr, yb_scr, *, d_a, d_kv, d_b, d_model):
    T = x_ref.shape[1]
    nblk = T // BLOCK
    s_idx = pl.program_id(1)

    c_q, c_k, c_v = 0, d_a, d_a + d_kv
    c_za = c_v + d_kv
    c_u = c_za + d_a
    c_vb = c_u + d_b
    c_zb = c_vb + d_b
    c_g = c_zb + d_b

    @pl.when(s_idx == 0)
    def _():
        k_scr[:, 0:BLOCK, :] = jnp.zeros((N_KV_HEADS, BLOCK, HEAD_DIM), BF16)
        vt_scr[:, :, 0:BLOCK] = jnp.zeros((N_KV_HEADS, HEAD_DIM, BLOCK), BF16)

    x = x_ref[0]
    ada = ada_ref[0]
    shift, scale, gate = ada[0:1], ada[1:2], ada[2:3]

    ms = jnp.mean(x * x, axis=-1, keepdims=True)
    h = (x * lax.rsqrt(ms + EPS)) * gpre_ref[...] * (1.0 + scale) + shift
    hb = h.astype(BF16)

    def proj(lo, hi):
        return jnp.dot(hb, win_ref[:, lo:hi], preferred_element_type=F32)


    pos = pos_ref[0].astype(F32)
    ang = pos * invf_ref[...]
    lane = lax.broadcasted_iota(jnp.int32, (1, LANES), 1)
    first_half = (lane % HEAD_DIM) < (HEAD_DIM // 2)
    cos = jnp.cos(ang)
    sin = jnp.sin(ang)
    sin_signed = jnp.where(first_half, -sin, sin)

    q = proj(c_q, c_k)
    kv = proj(c_k, c_za)
    v_raw = proj(c_vb, c_zb)
    q_tiles = _rope(q, cos, sin_signed, first_half)
    k_r = _rope(kv[:, 0:d_kv], cos, sin_signed, first_half)[0]
    vt_new = kv[:, d_kv:2 * d_kv].T
    for g in range(N_KV_HEADS):
        k_scr[g, BLOCK:BLOCK + T, :] = k_r[:, g * HEAD_DIM:(g + 1) * HEAD_DIM].astype(BF16)
        vt_scr[g, :, BLOCK:BLOCK + T] = vt_new[g * HEAD_DIM:(g + 1) * HEAD_DIM, :].astype(BF16)

    sm_scale = 1.0 / math.sqrt(HEAD_DIM)
    q_heads = []
    for c in range(len(q_tiles)):
        qs = (q_tiles[c] * sm_scale).astype(BF16)
        q_heads.append(qs[:, 0:HEAD_DIM])
        q_heads.append(qs[:, HEAD_DIM:2 * HEAD_DIM])

    v = _gelu_exact(v_raw)
    mu = jnp.mean(v, axis=-1, keepdims=True)
    vc = v - mu
    var = jnp.mean(vc * vc, axis=-1, keepdims=True)
    vn = (vc * lax.rsqrt(var + EPS) * lng_ref[...] + lnb_ref[...]).astype(BF16)

    kj = lax.broadcasted_iota(jnp.int32, (2 * BLOCK, BLOCK), 0)
    qi = lax.broadcasted_iota(jnp.int32, (2 * BLOCK, BLOCK), 1)
    rel = qi + BLOCK - kj
    in_win = (rel >= 0) & (rel < WINDOW)
    first_lo = jnp.where(s_idx == 0, BLOCK, 0)

    def scores(n, g):
        rows = slice(n * BLOCK, (n + 1) * BLOCK)
        band = slice(n * BLOCK, n * BLOCK + 2 * BLOCK)
        kb = k_scr[g, band, :]
        q_stack = jnp.concatenate([q_heads[g * GQ + j][rows] for j in range(GQ)], axis=0)
        return lax.dot_general(kb, q_stack, (((1,), (1,)), ((), ())),
                               preferred_element_type=F32)

    def softmax(s, n, g):
        valid = in_win & (kj >= first_lo) if n == 0 else in_win
        valid4 = jnp.concatenate([valid] * GQ, axis=1)
        s = jnp.where(valid4, s, NEG)
        sink = jnp.concatenate(
            [jnp.full((1, BLOCK), sinks_ref[g * GQ + j], F32) for j in range(GQ)], axis=1)
        m = jnp.maximum(jnp.max(s, axis=0, keepdims=True), sink)
        p = jnp.exp(s - m)
        denom = jnp.sum(p, axis=0, keepdims=True) + jnp.exp(sink - m)
        return p.astype(BF16), 1.0 / denom

    def pv(p, inv, n, g):
        rows = slice(n * BLOCK, (n + 1) * BLOCK)
        band = slice(n * BLOCK, n * BLOCK + 2 * BLOCK)
        ot = jnp.dot(vt_scr[g, :, band], p, preferred_element_type=F32) * inv
        for pair in range(GQ // 2):
            two = jnp.concatenate([ot[:, (2 * pair) * BLOCK:(2 * pair + 1) * BLOCK],
                                   ot[:, (2 * pair + 1) * BLOCK:(2 * pair + 2) * BLOCK]], axis=0)
            tile = g * (GQ // 2) + pair
            ya_scr[rows, tile * LANES:(tile + 1) * LANES] = two.T

    def attn_step(n, prev):
        if prev is not None:
            for g in range(N_KV_HEADS):
                pv(*prev[g], n - 1, g)
        if n < nblk:
            return [scores(n, g) for g in range(N_KV_HEADS)]
        return None

    def soft_step(n, ss):
        return [softmax(ss[g], n, g) for g in range(N_KV_HEADS)]

    ss = attn_step(0, None)
    u_raw = proj(c_u, c_vb)
    pp = soft_step(0, ss)
    ss = attn_step(1, pp)
    zb_raw = proj(c_zb, c_g)
    u = _gelu_exact(u_raw)
    pp = soft_step(1, ss)
    ss = attn_step(2, pp)

    ti = lax.broadcasted_iota(jnp.int32, (CHUNK, CHUNK), 0)
    si = lax.broadcasted_iota(jnp.int32, (CHUNK, CHUNK), 1)
    causal = si <= ti
    group_w = d_b // N_GROUPS
    for g in range(N_GROUPS):
        w_g = jnp.where(causal, ws_ref[g], 0.0).astype(BF16)
        bias = jnp.broadcast_to(bs_ref[:, g:g + 1], (CHUNK, group_w))
        cols = slice(g * group_w, (g + 1) * group_w)
        for n in range(T // CHUNK):
            rows = slice(n * CHUNK, (n + 1) * CHUNK)
            sv = jnp.dot(w_g, vn[rows, cols], preferred_element_type=F32) + bias
            yb_scr[rows, cols] = u[rows, cols] * sv
    pp = soft_step(2, ss)
    ss = attn_step(3, pp)
    za_raw = proj(c_za, c_u)
    y_b = yb_scr[...] * _silu(zb_raw)
    pp = soft_step(3, ss)
    attn_step(4, pp)

    k_scr[:, 0:BLOCK, :] = k_scr[:, T:T + BLOCK, :]
    vt_scr[:, :, 0:BLOCK] = vt_scr[:, :, T:T + BLOCK]

    ga_raw = proj(c_g, c_g + d_model)
    p_b = jnp.dot(y_b.astype(BF16), wpb_ref[...], preferred_element_type=F32)
    gb_raw = proj(c_g + d_model, c_g + 2 * d_model)
    y_a = ya_scr[...] * _silu(za_raw)
    p_a = jnp.dot(y_a.astype(BF16), wpa_ref[...], preferred_element_type=F32)

    merged = (jax.nn.sigmoid(ga_raw) * p_a + jax.nn.sigmoid(gb_raw) * p_b).astype(BF16)
    y = jnp.dot(merged, wout_ref[...], preferred_element_type=F32)
    ms_y = jnp.mean(y * y, axis=-1, keepdims=True)
    o_ref[0] = x + gate * (y * lax.rsqrt(ms_y + EPS) * gpost_ref[...])


def _const_spec(shape):
    return pl.BlockSpec(shape, lambda b, s: (0,) * len(shape), pipeline_mode=pl.Buffered(1))


def _layer(x, ada, positions, g_pre, g_post, w_in, sinks, ln_v_g, ln_v_b, w_s, b_s,
           w_proj_a, w_proj_b, w_out):
    B, S, D = x.shape
    T = SEQ_TILE
    assert S % T == 0 and T % BLOCK == 0 and T % CHUNK == 0
    d_a = N_Q_HEADS * HEAD_DIM
    d_kv = N_KV_HEADS * HEAD_DIM
    d_b = w_proj_b.shape[0]
    d_in = w_in.shape[1]
    assert d_in == 2 * d_a + 2 * d_kv + 3 * d_b + 2 * D

    half = HEAD_DIM // 2
    inv_freq = ROPE_THETA ** (-jnp.arange(half, dtype=F32) / half)
    invf = jnp.tile(inv_freq, LANES // half)[None, :]

    kern = functools.partial(_block_kernel, d_a=d_a, d_kv=d_kv, d_b=d_b, d_model=D)
    return pl.pallas_call(
        kern,
        out_shape=jax.ShapeDtypeStruct((B, S, D), x.dtype),
        grid=(B, S // T),
        in_specs=[
            pl.BlockSpec(memory_space=pltpu.SMEM),
            pl.BlockSpec((1, T, D), lambda b, s: (b, s, 0)),
            pl.BlockSpec((1, T, 1), lambda b, s: (b, s, 0)),
            pl.BlockSpec((1, 3, D), lambda b, s: (b, 0, 0)),
            _const_spec((1, D)),
            _const_spec((1, D)),
            _const_spec((1, LANES)),
            _const_spec((D, d_in)),
            _const_spec((1, d_b)),
            _const_spec((1, d_b)),
            _const_spec((N_GROUPS, CHUNK, CHUNK)),
            _const_spec((CHUNK, N_GROUPS)),
            _const_spec((d_a, D)),
            _const_spec((d_b, D)),
            _const_spec((D, D)),
        ],
        out_specs=pl.BlockSpec((1, T, D), lambda b, s: (b, s, 0)),
        scratch_shapes=[
            pltpu.VMEM((N_KV_HEADS, BLOCK + T, HEAD_DIM), BF16),
            pltpu.VMEM((N_KV_HEADS, HEAD_DIM, BLOCK + T), BF16),
            pltpu.VMEM((T, d_a), F32),
            pltpu.VMEM((T, d_b), F32),
        ],
        compiler_params=pltpu.CompilerParams(
            dimension_semantics=("arbitrary", "arbitrary"),
            vmem_limit_bytes=VMEM_LIMIT_BYTES),
        name="hybrid_block",
    )(sinks, x, positions.reshape(B, S, 1), ada.reshape(B, 3, D),
      g_pre[None, :], g_post[None, :], invf, w_in.astype(BF16),
      ln_v_g[None, :], ln_v_b[None, :], w_s, b_s.T,
      w_proj_a.astype(BF16), w_proj_b.astype(BF16), w_out.astype(BF16))


def _ada(c, w_ada, b_ada):
    B, D = c.shape
    n_out = w_ada.shape[1]
    tn = D
    return pl.pallas_call(
        _ada_kernel,
        out_shape=jax.ShapeDtypeStruct((B, n_out), F32),
        grid=(n_out // tn,),
        in_specs=[
            pl.BlockSpec((B, D), lambda j: (0, 0)),
            pl.BlockSpec((D, tn), lambda j: (0, j)),
            pl.BlockSpec((1, tn), lambda j: (0, j)),
        ],
        out_specs=pl.BlockSpec((B, tn), lambda j: (0, j)),
        compiler_params=pltpu.CompilerParams(dimension_semantics=("arbitrary",)),
        name="adaln_modulation",
    )(c, w_ada, b_ada[None, :])


def kernel(x, c, positions, w_ada, b_ada, g_pre, g_post, w_in, sinks, ln_v_g, ln_v_b, w_s, b_s,
           w_proj_a, w_proj_b, w_out):
    depth = w_in.shape[0]
    for l in range(depth):
        ada = _ada(c, w_ada[l], b_ada[l])
        x = _layer(x, ada, positions, g_pre[l], g_post[l], w_in[l], sinks[l], ln_v_g[l],
                   ln_v_b[l], w_s[l], b_s[l], w_proj_a[l], w_proj_b[l], w_out[l])
    return x
```

```python
import functools
import math

import jax
import jax.numpy as jnp
from jax import lax
from jax.experimental import pallas as pl
from jax.experimental.pallas import tpu as pltpu

HEAD_DIM = 64
N_Q_HEADS = 8
N_KV_HEADS = 2
GQ = N_Q_HEADS // N_KV_HEADS
WINDOW = 128
BLOCK = 128
ROPE_THETA = 10000.0
CHUNK = 128
N_GROUPS = 4
EPS = 1e-6
NEG = -1e30

LANES = 128
SEQ_TILE = 1024
VMEM_LIMIT_BYTES = 56 * 1024 * 1024

BF16 = jnp.bfloat16
F32 = jnp.float32


def _ada_kernel(c_ref, w_ref, b_ref, o_ref):
    c = c_ref[...]
    c_act = c * jax.nn.sigmoid(c)
    o_ref[...] = jnp.dot(c_act, w_ref[...], preferred_element_type=F32) + b_ref[...]


def _silu(x):
    return x * jax.nn.sigmoid(x)


def _gelu_exact(x):
    return 0.5 * x * (1.0 + lax.erf(x * (1.0 / math.sqrt(2.0))))


def _rope(t, cos, sin_signed, first_half):
    outs = []
    for c in range(t.shape[1] // LANES):
        tc = t[:, c * LANES:(c + 1) * LANES]
        rot = jnp.where(first_half,
                        pltpu.roll(tc, LANES - HEAD_DIM // 2, axis=1),
                        pltpu.roll(tc, HEAD_DIM // 2, axis=1))
        outs.append(tc * cos + rot * sin_signed)
    return outs


def _block_kernel(sinks_ref, x_ref, pos_ref, ada_ref, gpre_ref, gpost_ref, invf_ref,
                  win_ref, lng_ref, lnb_ref, ws_ref, bs_ref, wpa_ref, wpb_ref, wout_ref,
                  o_ref, k_scr, vt_scr, ya_scr, yb_scr, *, d_a, d_kv, d_b, d_model):
    T = x_ref.shape[1]
    nblk = T // BLOCK
    s_idx = pl.program_id(1)

    c_q, c_k, c_v = 0, d_a, d_a + d_kv
    c_za = c_v + d_kv
    c_u = c_za + d_a
    c_vb = c_u + d_b
    c_zb = c_vb + d_b
    c_g = c_zb + d_b

    @pl.when(s_idx == 0)
    def _():
        k_scr[:, 0:BLOCK, :] = jnp.zeros((N_KV_HEADS, BLOCK, HEAD_DIM), BF16)
        vt_scr[:, :, 0:BLOCK] = jnp.zeros((N_KV_HEADS, HEAD_DIM, BLOCK), BF16)

    x = x_ref[0]
    ada = ada_ref[0]
    shift, scale, gate = ada[0:1], ada[1:2], ada[2:3]

    ms = jnp.mean(x * x, axis=-1, keepdims=True)
    h = (x * lax.rsqrt(ms + EPS)) * gpre_ref[...] * (1.0 + scale) + shift
    hb = h.astype(BF16)

    def proj(lo, hi):
        return jnp.dot(hb, win_ref[:, lo:hi], preferred_element_type=F32)


    pos = pos_ref[0].astype(F32)
    ang = pos * invf_ref[...]
    lane = lax.broadcasted_iota(jnp.int32, (1, LANES), 1)
    first_half = (lane % HEAD_DIM) < (HEAD_DIM // 2)
    cos = jnp.cos(ang)
    sin = jnp.sin(ang)
    sin_signed = jnp.where(first_half, -sin, sin)

    q = proj(c_q, c_k)
    kv = proj(c_k, c_za)
    v_raw = proj(c_vb, c_zb)
    q_tiles = _rope(q, cos, sin_signed, first_half)
    k_r = _rope(kv[:, 0:d_kv], cos, sin_signed, first_half)[0]
    vt_new = kv[:, d_kv:2 * d_kv].T
    for g in range(N_KV_HEADS):
        k_scr[g, BLOCK:BLOCK + T, :] = k_r[:, g * HEAD_DIM:(g + 1) * HEAD_DIM].astype(BF16)
        vt_scr[g, :, BLOCK:BLOCK + T] = vt_new[g * HEAD_DIM:(g + 1) * HEAD_DIM, :].astype(BF16)

    sm_scale = 1.0 / math.sqrt(HEAD_DIM)
    q_heads = []
    for c in range(len(q_tiles)):
        qs = (q_tiles[c] * sm_scale).astype(BF16)
        q_heads.append(qs[:, 0:HEAD_DIM])
        q_heads.append(qs[:, HEAD_DIM:2 * HEAD_DIM])

    v = _gelu_exact(v_raw)
    mu = jnp.mean(v, axis=-1, keepdims=True)
    vc = v - mu
    var = jnp.mean(vc * vc, axis=-1, keepdims=True)
    vn = (vc * lax.rsqrt(var + EPS) * lng_ref[...] + lnb_ref[...]).astype(BF16)

    kj = lax.broadcasted_iota(jnp.int32, (2 * BLOCK, BLOCK), 0)
    qi = lax.broadcasted_iota(jnp.int32, (2 * BLOCK, BLOCK), 1)
    rel = qi + BLOCK - kj
    in_win = (rel >= 0) & (rel < WINDOW)
    first_lo = jnp.where(s_idx == 0, BLOCK, 0)

    def scores(n, g):
        rows = slice(n * BLOCK, (n + 1) * BLOCK)
        band = slice(n * BLOCK, n * BLOCK + 2 * BLOCK)
        kb = k_scr[g, band, :]
        q_stack = jnp.concatenate([q_heads[g * GQ + j][rows] for j in range(GQ)], axis=0)
        return lax.dot_general(kb, q_stack, (((1,), (1,)), ((), ())),
                               preferred_element_type=F32)

    def softmax(s, n, g):
        valid = in_win & (kj >= first_lo) if n == 0 else in_win
        valid4 = jnp.concatenate([valid] * GQ, axis=1)
        s = jnp.where(valid4, s, NEG)
        sink = jnp.concatenate(
            [jnp.full((1, BLOCK), sinks_ref[g * GQ + j], F32) for j in range(GQ)], axis=1)
        m = jnp.maximum(jnp.max(s, axis=0, keepdims=True), sink)
        p = jnp.exp(s - m)
        denom = jnp.sum(p, axis=0, keepdims=True) + jnp.exp(sink - m)
        return p.astype(BF16), 1.0 / denom

    def pv(p, inv, n, g):
        rows = slice(n * BLOCK, (n + 1) * BLOCK)
        band = slice(n * BLOCK, n * BLOCK + 2 * BLOCK)
        ot = jnp.dot(vt_scr[g, :, band], p, preferred_element_type=F32) * inv
        for pair in range(GQ // 2):
            two = jnp.concatenate([ot[:, (2 * pair) * BLOCK:(2 * pair + 1) * BLOCK],
                                   ot[:, (2 * pair + 1) * BLOCK:(2 * pair + 2) * BLOCK]], axis=0)
            tile = g * (GQ // 2) + pair
            ya_scr[rows, tile * LANES:(tile + 1) * LANES] = two.T

    def attn_step(n, prev):
        if prev is not None:
            for g in range(N_KV_HEADS):
                pv(*prev[g], n - 1, g)
        if n < nblk:
            return [scores(n, g) for g in range(N_KV_HEADS)]
        return None

    def soft_step(n, ss):
        return [softmax(ss[g], n, g) for g in range(N_KV_HEADS)]

    r = {}

    def f_u():
        r["u"] = _gelu_exact(proj(c_u, c_vb))

    def f_zb():
        r["zb"] = proj(c_zb, c_g)

    def f_gating():
        ti = lax.broadcasted_iota(jnp.int32, (CHUNK, CHUNK), 0)
        si = lax.broadcasted_iota(jnp.int32, (CHUNK, CHUNK), 1)
        causal = si <= ti
        group_w = d_b // N_GROUPS
        for g in range(N_GROUPS):
            w_g = jnp.where(causal, ws_ref[g], 0.0).astype(BF16)
            bias = jnp.broadcast_to(bs_ref[:, g:g + 1], (CHUNK, group_w))
            cols = slice(g * group_w, (g + 1) * group_w)
            for n in range(T // CHUNK):
                rows = slice(n * CHUNK, (n + 1) * CHUNK)
                sv = jnp.dot(w_g, vn[rows, cols], preferred_element_type=F32) + bias
                yb_scr[rows, cols] = r["u"][rows, cols] * sv

    def f_za():
        r["za"] = proj(c_za, c_u)

    def f_gate(name, lo):
        def f():
            r[name] = proj(lo, lo + d_model // 2)
        return f

    def f_pb():
        y_b = yb_scr[...] * _silu(r["zb"])
        r["pb"] = jnp.dot(y_b.astype(BF16), wpb_ref[...], preferred_element_type=F32)

    fillers = [f_u, f_zb, f_gating, f_za,
               f_gate("ga0", c_g), f_gate("ga1", c_g + d_model // 2), f_pb,
               f_gate("gb0", c_g + d_model), f_gate("gb1", c_g + d_model + d_model // 2)]

    ss = attn_step(0, None)
    for n in range(nblk):
        if fillers:
            fillers.pop(0)()
        pp = soft_step(n, ss)
        ss = attn_step(n + 1, pp)
    for f in fillers:
        f()

    k_scr[:, 0:BLOCK, :] = k_scr[:, T:T + BLOCK, :]
    vt_scr[:, :, 0:BLOCK] = vt_scr[:, :, T:T + BLOCK]

    y_a = ya_scr[...] * _silu(r["za"])
    p_a = jnp.dot(y_a.astype(BF16), wpa_ref[...], preferred_element_type=F32)

    gate_a = jax.nn.sigmoid(jnp.concatenate([r["ga0"], r["ga1"]], axis=1))
    gate_b = jax.nn.sigmoid(jnp.concatenate([r["gb0"], r["gb1"]], axis=1))
    merged = (gate_a * p_a + gate_b * r["pb"]).astype(BF16)
    y = jnp.dot(merged, wout_ref[...], preferred_element_type=F32)
    ms_y = jnp.mean(y * y, axis=-1, keepdims=True)
    o_ref[0] = x + gate * (y * lax.rsqrt(ms_y + EPS) * gpost_ref[...])


def _const_spec(shape):
    return pl.BlockSpec(shape, lambda b, s: (0,) * len(shape), pipeline_mode=pl.Buffered(1))


def _layer(x, ada, positions, g_pre, g_post, w_in, sinks, ln_v_g, ln_v_b, w_s, b_s,
           w_proj_a, w_proj_b, w_out):
    B, S, D = x.shape
    T = SEQ_TILE
    assert S % T == 0 and T % BLOCK == 0 and T % CHUNK == 0
    d_a = N_Q_HEADS * HEAD_DIM
    d_kv = N_KV_HEADS * HEAD_DIM
    d_b = w_proj_b.shape[0]
    d_in = w_in.shape[1]
    assert d_in == 2 * d_a + 2 * d_kv + 3 * d_b + 2 * D

    half = HEAD_DIM // 2
    inv_freq = ROPE_THETA ** (-jnp.arange(half, dtype=F32) / half)
    invf = jnp.tile(inv_freq, LANES // half)[None, :]

    kern = functools.partial(_block_kernel, d_a=d_a, d_kv=d_kv, d_b=d_b, d_model=D)
    return pl.pallas_call(
        kern,
        out_shape=jax.ShapeDtypeStruct((B, S, D), x.dtype),
        grid=(B, S // T),
        in_specs=[
            pl.BlockSpec(memory_space=pltpu.SMEM),
            pl.BlockSpec((1, T, D), lambda b, s: (b, s, 0)),
            pl.BlockSpec((1, T, 1), lambda b, s: (b, s, 0)),
            pl.BlockSpec((1, 3, D), lambda b, s: (b, 0, 0)),
            _const_spec((1, D)),
            _const_spec((1, D)),
            _const_spec((1, LANES)),
            _const_spec((D, d_in)),
            _const_spec((1, d_b)),
            _const_spec((1, d_b)),
            _const_spec((N_GROUPS, CHUNK, CHUNK)),
            _const_spec((CHUNK, N_GROUPS)),
            _const_spec((d_a, D)),
            _const_spec((d_b, D)),
            _const_spec((D, D)),
        ],
        out_specs=pl.BlockSpec((1, T, D), lambda b, s: (b, s, 0)),
        scratch_shapes=[
            pltpu.VMEM((N_KV_HEADS, BLOCK + T, HEAD_DIM), BF16),
            pltpu.VMEM((N_KV_HEADS, HEAD_DIM, BLOCK + T), BF16),
            pltpu.VMEM((T, d_a), F32),
            pltpu.VMEM((T, d_b), F32),
        ],
        compiler_params=pltpu.CompilerParams(
            dimension_semantics=("arbitrary", "arbitrary"),
            vmem_limit_bytes=VMEM_LIMIT_BYTES),
        name="hybrid_block",
    )(sinks, x, positions.reshape(B, S, 1), ada.reshape(B, 3, D),
      g_pre[None, :], g_post[None, :], invf, w_in.astype(BF16),
      ln_v_g[None, :], ln_v_b[None, :], w_s, b_s.T,
      w_proj_a.astype(BF16), w_proj_b.astype(BF16), w_out.astype(BF16))


def _ada(c, w_ada, b_ada):
    B, D = c.shape
    n_out = w_ada.shape[1]
    tn = D
    return pl.pallas_call(
        _ada_kernel,
        out_shape=jax.ShapeDtypeStruct((B, n_out), F32),
        grid=(n_out // tn,),
        in_specs=[
            pl.BlockSpec((B, D), lambda j: (0, 0)),
            pl.BlockSpec((D, tn), lambda j: (0, j)),
            pl.BlockSpec((1, tn), lambda j: (0, j)),
        ],
        out_specs=pl.BlockSpec((B, tn), lambda j: (0, j)),
        compiler_params=pltpu.CompilerParams(dimension_semantics=("arbitrary",)),
        name="adaln_modulation",
    )(c, w_ada, b_ada[None, :])


def kernel(x, c, positions, w_ada, b_ada, g_pre, g_post, w_in, sinks, ln_v_g, ln_v_b, w_s, b_s,
           w_proj_a, w_proj_b, w_out):
    depth = w_in.shape[0]
    for l in range(depth):
        ada = _ada(c, w_ada[l], b_ada[l])
        x = _layer(x, ada, positions, g_pre[l], g_post[l], w_in[l], sinks[l], ln_v_g[l],
                   ln_v_b[l], w_s[l], b_s[l], w_proj_a[l], w_proj_b[l], w_out[l])
    return x
```

```python
import functools
import math

import jax
import jax.numpy as jnp
from jax import lax
from jax.experimental import pallas as pl
from jax.experimental.pallas import tpu as pltpu

HEAD_DIM = 64
N_Q_HEADS = 8
N_KV_HEADS = 2
GQ = N_Q_HEADS // N_KV_HEADS
WINDOW = 128
BLOCK = 128
ROPE_THETA = 10000.0
CHUNK = 128
N_GROUPS = 4
EPS = 1e-6
NEG = -1e30
LOG2E = 1.4426950408889634

LANES = 128
SEQ_TILE = 1024
ROW_CHUNK = 256
VMEM_LIMIT_BYTES = 56 * 1024 * 1024

BF16 = jnp.bfloat16
F32 = jnp.float32


def _ada_kernel(c_ref, w_ref, b_ref, o_ref):
    c = c_ref[...]
    c_act = c * jax.nn.sigmoid(c)
    o_ref[...] = jnp.dot(c_act, w_ref[...], preferred_element_type=F32) + b_ref[...]


def _silu(x):
    return x * jax.nn.sigmoid(x)


def _gelu_exact(x):
    return 0.5 * x * (1.0 + lax.erf(x * (1.0 / math.sqrt(2.0))))


def _rope(t, cos, sin_signed, first_half):
    outs = []
    for c in range(t.shape[1] // LANES):
        tc = t[:, c * LANES:(c + 1) * LANES]
        rot = jnp.where(first_half,
                        pltpu.roll(tc, LANES - HEAD_DIM // 2, axis=1),
                        pltpu.roll(tc, HEAD_DIM // 2, axis=1))
        outs.append(tc * cos + rot * sin_signed)
    return outs


def _block_kernel(sinks_ref, x_ref, pos_ref, ada_ref, gpre_ref, gpost_ref, invf_ref,
                  win_ref, lng_ref, lnb_ref, ws_ref, bs_ref, wpa_ref, wpb_ref, wout_ref,
                  o_ref, k_scr, vt_scr, ya_scr, yb_scr, *, d_a, d_kv, d_b, d_model):
    T = x_ref.shape[1]
    nblk = T // BLOCK
    nrc = T // ROW_CHUNK
    s_idx = pl.program_id(1)

    c_q, c_k, c_v = 0, d_a, d_a + d_kv
    c_za = c_v + d_kv
    c_u = c_za + d_a
    c_vb = c_u + d_b
    c_zb = c_vb + d_b
    c_g = c_zb + d_b

    @pl.when(s_idx == 0)
    def _():
        k_scr[:, 0:BLOCK, :] = jnp.zeros((N_KV_HEADS, BLOCK, HEAD_DIM), BF16)
        vt_scr[:, :, 0:BLOCK] = jnp.zeros((N_KV_HEADS, HEAD_DIM, BLOCK), BF16)

    ada = ada_ref[0]
    shift, scale, gate = ada[0:1], ada[1:2], ada[2:3]
    pre_gain = gpre_ref[...] * (1.0 + scale)

    hb_rows, qkv_rows = [], []
    for c in range(nrc):
        xc = x_ref[0, c * ROW_CHUNK:(c + 1) * ROW_CHUNK, :]
        ms = jnp.mean(xc * xc, axis=-1, keepdims=True)
        hc = ((xc * lax.rsqrt(ms + EPS)) * pre_gain + shift).astype(BF16)
        hb_rows.append(hc)
        qkv_rows.append(jnp.dot(hc, win_ref[:, c_q:c_za], preferred_element_type=F32))
    hb = jnp.concatenate(hb_rows, axis=0)
    qkv = jnp.concatenate(qkv_rows, axis=0)

    def proj(lo, hi):
        return jnp.dot(hb, win_ref[:, lo:hi], preferred_element_type=F32)

    pos = pos_ref[0].astype(F32)
    invf = invf_ref[...]
    cos_rows, sin_rows = [], []
    for j in range(nblk):
        ang = invf * pos[:, j * BLOCK:(j + 1) * BLOCK]
        cs = jnp.cos(ang)
        sn = jnp.sin(ang)
        cos_rows.append(jnp.concatenate([cs, cs, cs, cs], axis=0).T)
        sin_rows.append(jnp.concatenate([-sn, sn, -sn, sn], axis=0).T)
    cos = jnp.concatenate(cos_rows, axis=0)
    sin_signed = jnp.concatenate(sin_rows, axis=0)
    lane = lax.broadcasted_iota(jnp.int32, (1, LANES), 1)
    first_half = (lane % HEAD_DIM) < (HEAD_DIM // 2)

    q_tiles = _rope(qkv[:, c_q:c_k], cos, sin_signed, first_half)
    k_r = _rope(qkv[:, c_k:c_v], cos, sin_signed, first_half)[0]
    vt_new = qkv[:, c_v:c_za].T
    for g in range(N_KV_HEADS):
        k_scr[g, BLOCK:BLOCK + T, :] = k_r[:, g * HEAD_DIM:(g + 1) * HEAD_DIM].astype(BF16)
        vt_scr[g, :, BLOCK:BLOCK + T] = vt_new[g * HEAD_DIM:(g + 1) * HEAD_DIM, :].astype(BF16)

    q_scale = LOG2E / math.sqrt(HEAD_DIM)
    q_heads = []
    for c in range(len(q_tiles)):
        qs = (q_tiles[c] * q_scale).astype(BF16)
        q_heads.append(qs[:, 0:HEAD_DIM])
        q_heads.append(qs[:, HEAD_DIM:2 * HEAD_DIM])

    kj = lax.broadcasted_iota(jnp.int32, (2 * BLOCK, BLOCK), 0)
    qi = lax.broadcasted_iota(jnp.int32, (2 * BLOCK, BLOCK), 1)
    rel = qi + BLOCK - kj
    in_win = (rel >= 0) & (rel < WINDOW)
    first_lo = jnp.where(s_idx == 0, BLOCK, 0)

    def scores(n, g):
        rows = slice(n * BLOCK, (n + 1) * BLOCK)
        band = slice(n * BLOCK, n * BLOCK + 2 * BLOCK)
        kb = k_scr[g, band, :]
        q_stack = jnp.concatenate([q_heads[g * GQ + j][rows] for j in range(GQ)], axis=0)
        return lax.dot_general(kb, q_stack, (((1,), (1,)), ((), ())),
                               preferred_element_type=F32)

    def softmax(s, n, g):
        valid = in_win & (kj >= first_lo) if n == 0 else in_win
        valid4 = jnp.concatenate([valid] * GQ, axis=1)
        s = jnp.where(valid4, s, NEG)
        sink = jnp.concatenate(
            [jnp.full((1, BLOCK), sinks_ref[g * GQ + j] * LOG2E, F32) for j in range(GQ)], axis=1)
        m = jnp.maximum(jnp.max(s, axis=0, keepdims=True), sink)
        p = jnp.exp2(s - m)
        denom = jnp.sum(p, axis=0, keepdims=True) + jnp.exp2(sink - m)
        return p.astype(BF16), 1.0 / denom

    def pv(p, inv, n, g):
        rows = slice(n * BLOCK, (n + 1) * BLOCK)
        band = slice(n * BLOCK, n * BLOCK + 2 * BLOCK)
        ot = jnp.dot(vt_scr[g, :, band], p, preferred_element_type=F32) * inv
        for pair in range(GQ // 2):
            two = jnp.concatenate([ot[:, (2 * pair) * BLOCK:(2 * pair + 1) * BLOCK],
                                   ot[:, (2 * pair + 1) * BLOCK:(2 * pair + 2) * BLOCK]], axis=0)
            tile = g * (GQ // 2) + pair
            ya_scr[rows, tile * LANES:(tile + 1) * LANES] = two.T

    def attn_step(n, prev):
        if prev is not None:
            for g in range(N_KV_HEADS):
                pv(*prev[g], n - 1, g)
        if n < nblk:
            return [scores(n, g) for g in range(N_KV_HEADS)]
        return None

    def soft_step(n, ss):
        return [softmax(ss[g], n, g) for g in range(N_KV_HEADS)]

    r = {}

    def f_raw(name, lo, hi):
        def f():
            r[name] = proj(lo, hi)
        return f

    def f_v():
        v = _gelu_exact(proj(c_vb, c_zb))
        mu = jnp.mean(v, axis=-1, keepdims=True)
        vc = v - mu
        var = jnp.mean(vc * vc, axis=-1, keepdims=True)
        r["vn"] = (vc * lax.rsqrt(var + EPS) * lng_ref[...] + lnb_ref[...]).astype(BF16)

    def f_u():
        r["u"] = _gelu_exact(proj(c_u, c_vb))

    def f_gating():
        ti = lax.broadcasted_iota(jnp.int32, (CHUNK, CHUNK), 0)
        si = lax.broadcasted_iota(jnp.int32, (CHUNK, CHUNK), 1)
        causal = si <= ti
        group_w = d_b // N_GROUPS
        for g in range(N_GROUPS):
            w_g = jnp.where(causal, ws_ref[g], 0.0).astype(BF16)
            bias = jnp.broadcast_to(bs_ref[:, g:g + 1], (CHUNK, group_w))
            cols = slice(g * group_w, (g + 1) * group_w)
            for n in range(T // CHUNK):
                rows = slice(n * CHUNK, (n + 1) * CHUNK)
                sv = jnp.dot(w_g, r["vn"][rows, cols], preferred_element_type=F32) + bias
                yb_scr[rows, cols] = r["u"][rows, cols] * sv

    def f_pb():
        y_b = yb_scr[...] * _silu(r["zb"])
        r["pb"] = jnp.dot(y_b.astype(BF16), wpb_ref[...], preferred_element_type=F32)

    half = d_model // 2
    fillers = [f_raw("zb", c_zb, c_g), f_raw("za", c_za, c_u),
               f_raw("ga0", c_g, c_g + half), f_raw("ga1", c_g + half, c_g + d_model),
               f_v, f_u, f_raw("gb0", c_g + d_model, c_g + d_model + half), f_gating,
               f_raw("gb1", c_g + d_model + half, c_g + 2 * d_model), f_pb]

    ss = attn_step(0, None)
    for n in range(nblk):
        if fillers:
            fillers.pop(0)()
        pp = soft_step(n, ss)
        ss = attn_step(n + 1, pp)
    for f in fillers:
        f()

    k_scr[:, 0:BLOCK, :] = k_scr[:, T:T + BLOCK, :]
    vt_scr[:, :, 0:BLOCK] = vt_scr[:, :, T:T + BLOCK]

    def pa_chunk(c):
        rows = slice(c * ROW_CHUNK, (c + 1) * ROW_CHUNK)
        y_a = ya_scr[rows, :] * _silu(r["za"][rows])
        return jnp.dot(y_a.astype(BF16), wpa_ref[...], preferred_element_type=F32)

    def out_chunk(c, p_a):
        rows = slice(c * ROW_CHUNK, (c + 1) * ROW_CHUNK)
        gate_a = jax.nn.sigmoid(jnp.concatenate([r["ga0"][rows], r["ga1"][rows]], axis=1))
        gate_b = jax.nn.sigmoid(jnp.concatenate([r["gb0"][rows], r["gb1"][rows]], axis=1))
        merged = (gate_a * p_a + gate_b * r["pb"][rows]).astype(BF16)
        y = jnp.dot(merged, wout_ref[...], preferred_element_type=F32)
        ms_y = jnp.mean(y * y, axis=-1, keepdims=True)
        o_ref[0, rows, :] = x_ref[0, rows, :] + gate * (y * lax.rsqrt(ms_y + EPS) * gpost_ref[...])

    p_as = [pa_chunk(0)]
    for c in range(nrc):
        if c + 1 < nrc:
            p_as.append(pa_chunk(c + 1))
        out_chunk(c, p_as[c])


def _const_spec(shape):
    return pl.BlockSpec(shape, lambda b, s: (0,) * len(shape), pipeline_mode=pl.Buffered(1))


def _layer(x, ada, positions, g_pre, g_post, w_in, sinks, ln_v_g, ln_v_b, w_s, b_s,
           w_proj_a, w_proj_b, w_out):
    B, S, D = x.shape
    T = SEQ_TILE
    assert S % T == 0 and T % BLOCK == 0 and T % CHUNK == 0 and T % ROW_CHUNK == 0
    d_a = N_Q_HEADS * HEAD_DIM
    d_kv = N_KV_HEADS * HEAD_DIM
    d_b = w_proj_b.shape[0]
    d_in = w_in.shape[1]
    assert d_in == 2 * d_a + 2 * d_kv + 3 * d_b + 2 * D

    half = HEAD_DIM // 2
    inv_freq = ROPE_THETA ** (-jnp.arange(half, dtype=F32) / half)
    invf = jnp.broadcast_to(inv_freq[:, None], (half, LANES))

    kern = functools.partial(_block_kernel, d_a=d_a, d_kv=d_kv, d_b=d_b, d_model=D)
    return pl.pallas_call(
        kern,
        out_shape=jax.ShapeDtypeStruct((B, S, D), x.dtype),
        grid=(B, S // T),
        in_specs=[
            pl.BlockSpec(memory_space=pltpu.SMEM),
            pl.BlockSpec((1, T, D), lambda b, s: (b, s, 0)),
            pl.BlockSpec((1, 1, T), lambda b, s: (b, 0, s)),
            pl.BlockSpec((1, 3, D), lambda b, s: (b, 0, 0)),
            _const_spec((1, D)),
            _const_spec((1, D)),
            _const_spec((half, LANES)),
            _const_spec((D, d_in)),
            _const_spec((1, d_b)),
            _const_spec((1, d_b)),
            _const_spec((N_GROUPS, CHUNK, CHUNK)),
            _const_spec((CHUNK, N_GROUPS)),
            _const_spec((d_a, D)),
            _const_spec((d_b, D)),
            _const_spec((D, D)),
        ],
        out_specs=pl.BlockSpec((1, T, D), lambda b, s: (b, s, 0)),
        scratch_shapes=[
            pltpu.VMEM((N_KV_HEADS, BLOCK + T, HEAD_DIM), BF16),
            pltpu.VMEM((N_KV_HEADS, HEAD_DIM, BLOCK + T), BF16),
            pltpu.VMEM((T, d_a), F32),
            pltpu.VMEM((T, d_b), F32),
        ],
        compiler_params=pltpu.CompilerParams(
            dimension_semantics=("arbitrary", "arbitrary"),
            vmem_limit_bytes=VMEM_LIMIT_BYTES),
        name="hybrid_block",
    )(sinks, x, positions.reshape(B, 1, S), ada.reshape(B, 3, D),
      g_pre[None, :], g_post[None, :], invf, w_in.astype(BF16),
      ln_v_g[None, :], ln_v_b[None, :], w_s, b_s.T,
      w_proj_a.astype(BF16), w_proj_b.astype(BF16), w_out.astype(BF16))


def _ada(c, w_ada, b_ada):
    B, D = c.shape
    n_out = w_ada.shape[1]
    tn = D
    return pl.pallas_call(
        _ada_kernel,
        out_shape=jax.ShapeDtypeStruct((B, n_out), F32),
        grid=(n_out // tn,),
        in_specs=[
            pl.BlockSpec((B, D), lambda j: (0, 0)),
            pl.BlockSpec((D, tn), lambda j: (0, j)),
            pl.BlockSpec((1, tn), lambda j: (0, j)),
        ],
        out_specs=pl.BlockSpec((B, tn), lambda j: (0, j)),
        compiler_params=pltpu.CompilerParams(dimension_semantics=("arbitrary",)),
        name="adaln_modulation",
    )(c, w_ada, b_ada[None, :])


def kernel(x, c, positions, w_ada, b_ada, g_pre, g_post, w_in, sinks, ln_v_g, ln_v_b, w_s, b_s,
           w_proj_a, w_proj_b, w_out):
    depth = w_in.shape[0]
    for l in range(depth):
        ada = _ada(c, w_ada[l], b_ada[l])
        x = _layer(x, ada, positions, g_pre[l], g_post[l], w_in[l], sinks[l], ln_v_g[l],
                   ln_v_b[l], w_s[l], b_s[l], w_proj_a[l], w_proj_b[l], w_out[l])
    return x
```

```python
import functools
import math

import jax
import jax.numpy as jnp
from jax import lax
from jax.experimental import pallas as pl
from jax.experimental.pallas import tpu as pltpu

HEAD_DIM = 64
N_Q_HEADS = 8
N_KV_HEADS = 2
GQ = N_Q_HEADS // N_KV_HEADS
WINDOW = 128
BLOCK = 128
ROPE_THETA = 10000.0
CHUNK = 128
N_GROUPS = 4
EPS = 1e-6
NEG = -1e30
LOG2E = 1.4426950408889634

LANES = 128
SEQ_TILE = 1024
SUB_TILE = 256
VMEM_LIMIT_BYTES = 56 * 1024 * 1024

BF16 = jnp.bfloat16
F32 = jnp.float32


def _ada_kernel(c_ref, w_ref, b_ref, o_ref):
    c = c_ref[...]
    c_act = c * jax.nn.sigmoid(c)
    o_ref[...] = jnp.dot(c_act, w_ref[...], preferred_element_type=F32) + b_ref[...]


def _silu(x):
    return x * jax.nn.sigmoid(x)


def _gelu_exact(x):
    return 0.5 * x * (1.0 + lax.erf(x * (1.0 / math.sqrt(2.0))))


def _rope(t, cos, sin_signed, first_half):
    outs = []
    for c in range(t.shape[1] // LANES):
        tc = t[:, c * LANES:(c + 1) * LANES]
        rot = jnp.where(first_half,
                        pltpu.roll(tc, LANES - HEAD_DIM // 2, axis=1),
                        pltpu.roll(tc, HEAD_DIM // 2, axis=1))
        outs.append(tc * cos + rot * sin_signed)
    return outs


def _block_kernel(sinks_ref, x_ref, pos_ref, ada_ref, gpre_ref, gpost_ref, invf_ref,
                  win_ref, lng_ref, lnb_ref, ws_ref, bs_ref, wpa_ref, wpb_ref, wout_ref,
                  o_ref, k_scr, vt_scr, *, d_a, d_kv, d_b, d_model):
    T = x_ref.shape[1]
    R = SUB_TILE
    n_sub = T // R
    blk_per_sub = R // BLOCK
    s_idx = pl.program_id(1)

    c_q, c_k, c_v = 0, d_a, d_a + d_kv
    c_za = c_v + d_kv
    c_u = c_za + d_a
    c_vb = c_u + d_b
    c_zb = c_vb + d_b
    c_g = c_zb + d_b

    @pl.when(s_idx == 0)
    def _():
        k_scr[:, 0:BLOCK, :] = jnp.zeros((N_KV_HEADS, BLOCK, HEAD_DIM), BF16)
        vt_scr[:, :, 0:BLOCK] = jnp.zeros((N_KV_HEADS, HEAD_DIM, BLOCK), BF16)

    ada = ada_ref[0]
    shift, scale, gate = ada[0:1], ada[1:2], ada[2:3]
    pre_gain = gpre_ref[...] * (1.0 + scale)
    invf = invf_ref[...]
    lane = lax.broadcasted_iota(jnp.int32, (1, LANES), 1)
    first_half = (lane % HEAD_DIM) < (HEAD_DIM // 2)

    kj = lax.broadcasted_iota(jnp.int32, (2 * BLOCK, BLOCK), 0)
    qi = lax.broadcasted_iota(jnp.int32, (2 * BLOCK, BLOCK), 1)
    rel = qi + BLOCK - kj
    in_win = (rel >= 0) & (rel < WINDOW)
    first_lo = jnp.where(s_idx == 0, BLOCK, 0)
    ti = lax.broadcasted_iota(jnp.int32, (CHUNK, CHUNK), 0)
    si = lax.broadcasted_iota(jnp.int32, (CHUNK, CHUNK), 1)
    causal = si <= ti
    group_w = d_b // N_GROUPS
    q_scale = LOG2E / math.sqrt(HEAD_DIM)

    st = [dict() for _ in range(n_sub)]

    def proj(j, lo, hi):
        return jnp.dot(st[j]["hb"], win_ref[:, lo:hi], preferred_element_type=F32)

    def head(j):
        xc = x_ref[0, j * R:(j + 1) * R, :]
        ms = jnp.mean(xc * xc, axis=-1, keepdims=True)
        st[j]["hb"] = ((xc * lax.rsqrt(ms + EPS)) * pre_gain + shift).astype(BF16)
        st[j]["qkv"] = proj(j, c_q, c_za)

    def gmlp_in(j):
        v = _gelu_exact(proj(j, c_vb, c_zb))
        mu = jnp.mean(v, axis=-1, keepdims=True)
        vc = v - mu
        var = jnp.mean(vc * vc, axis=-1, keepdims=True)
        st[j]["vn"] = (vc * lax.rsqrt(var + EPS) * lng_ref[...] + lnb_ref[...]).astype(BF16)
        st[j]["u"] = _gelu_exact(proj(j, c_u, c_vb))

    def rope_and_scores(j):
        pos = pos_ref[0, :, j * R:(j + 1) * R].astype(F32)
        cos_rows, sin_rows = [], []
        for b in range(blk_per_sub):
            ang = invf * pos[:, b * BLOCK:(b + 1) * BLOCK]
            cs = jnp.cos(ang)
            sn = jnp.sin(ang)
            cos_rows.append(jnp.concatenate([cs, cs, cs, cs], axis=0).T)
            sin_rows.append(jnp.concatenate([-sn, sn, -sn, sn], axis=0).T)
        cos = jnp.concatenate(cos_rows, axis=0)
        sin_signed = jnp.concatenate(sin_rows, axis=0)
        qkv = st[j]["qkv"]
        q_tiles = _rope(qkv[:, c_q:c_k], cos, sin_signed, first_half)
        k_r = _rope(qkv[:, c_k:c_v], cos, sin_signed, first_half)[0]
        vt_new = qkv[:, c_v:c_za].T
        lo = BLOCK + j * R
        for g in range(N_KV_HEADS):
            k_scr[g, lo:lo + R, :] = k_r[:, g * HEAD_DIM:(g + 1) * HEAD_DIM].astype(BF16)
            vt_scr[g, :, lo:lo + R] = vt_new[g * HEAD_DIM:(g + 1) * HEAD_DIM, :].astype(BF16)
        q_heads = []
        for c in range(len(q_tiles)):
            qs = (q_tiles[c] * q_scale).astype(BF16)
            q_heads.append(qs[:, 0:HEAD_DIM])
            q_heads.append(qs[:, HEAD_DIM:2 * HEAD_DIM])
        ss = []
        for b in range(blk_per_sub):
            n = j * blk_per_sub + b
            band = slice(n * BLOCK, n * BLOCK + 2 * BLOCK)
            for g in range(N_KV_HEADS):
                kb = k_scr[g, band, :]
                q_stack = jnp.concatenate(
                    [q_heads[g * GQ + h][b * BLOCK:(b + 1) * BLOCK] for h in range(GQ)], axis=0)
                ss.append(lax.dot_general(kb, q_stack, (((1,), (1,)), ((), ())),
                                          preferred_element_type=F32))
        st[j]["scores"] = ss

    def softmax_pv(j):
        outs = []
        idx = 0
        for b in range(blk_per_sub):
            n = j * blk_per_sub + b
            band = slice(n * BLOCK, n * BLOCK + 2 * BLOCK)
            tiles = []
            for g in range(N_KV_HEADS):
                s = st[j]["scores"][idx]
                idx += 1
                valid = in_win & (kj >= first_lo) if n == 0 else in_win
                s = jnp.where(jnp.concatenate([valid] * GQ, axis=1), s, NEG)
                sink = jnp.concatenate(
                    [jnp.full((1, BLOCK), sinks_ref[g * GQ + h] * LOG2E, F32) for h in range(GQ)],
                    axis=1)
                m = jnp.maximum(jnp.max(s, axis=0, keepdims=True), sink)
                p = jnp.exp2(s - m)
                denom = jnp.sum(p, axis=0, keepdims=True) + jnp.exp2(sink - m)
                ot = jnp.dot(vt_scr[g, :, band], p.astype(BF16),
                             preferred_element_type=F32) * (1.0 / denom)
                for pair in range(GQ // 2):
                    two = jnp.concatenate(
                        [ot[:, (2 * pair) * BLOCK:(2 * pair + 1) * BLOCK],
                         ot[:, (2 * pair + 1) * BLOCK:(2 * pair + 2) * BLOCK]], axis=0)
                    tiles.append(two.T)
            outs.append(jnp.concatenate(tiles, axis=1))
        st[j]["attn"] = jnp.concatenate(outs, axis=0)

    def gating(j):
        rows_out = []
        for n in range(R // CHUNK):
            rows = slice(n * CHUNK, (n + 1) * CHUNK)
            cols_out = []
            for g in range(N_GROUPS):
                w_g = jnp.where(causal, ws_ref[g], 0.0).astype(BF16)
                bias = jnp.broadcast_to(bs_ref[:, g:g + 1], (CHUNK, group_w))
                cols = slice(g * group_w, (g + 1) * group_w)
                sv = jnp.dot(w_g, st[j]["vn"][rows, cols], preferred_element_type=F32) + bias
                cols_out.append(st[j]["u"][rows, cols] * sv)
            rows_out.append(jnp.concatenate(cols_out, axis=1))
        st[j]["yb"] = jnp.concatenate(rows_out, axis=0)

    def branch_proj(j):
        y_a = st[j]["attn"] * _silu(st[j]["za"])
        st[j]["pa"] = jnp.dot(y_a.astype(BF16), wpa_ref[...], preferred_element_type=F32)
        y_b = st[j]["yb"] * _silu(st[j]["zb"])
        st[j]["pb"] = jnp.dot(y_b.astype(BF16), wpb_ref[...], preferred_element_type=F32)

    def out_proj(j):
        merged = (jax.nn.sigmoid(st[j]["ga"]) * st[j]["pa"]
                  + jax.nn.sigmoid(st[j]["gb"]) * st[j]["pb"]).astype(BF16)
        y = jnp.dot(merged, wout_ref[...], preferred_element_type=F32)
        ms_y = jnp.mean(y * y, axis=-1, keepdims=True)
        rows = slice(j * R, (j + 1) * R)
        o_ref[0, rows, :] = x_ref[0, rows, :] + gate * (y * lax.rsqrt(ms_y + EPS) * gpost_ref[...])
        st[j].clear()

    for j in range(n_sub + 1):
        if j < n_sub:
            head(j)
            gmlp_in(j)
        if j >= 1:
            out_proj(j - 1)
        if j < n_sub:
            rope_and_scores(j)
            st[j]["za"] = proj(j, c_za, c_u)
            st[j]["zb"] = proj(j, c_zb, c_g)
            st[j]["ga"] = proj(j, c_g, c_g + d_model)
            softmax_pv(j)
            gating(j)
            st[j]["gb"] = proj(j, c_g + d_model, c_g + 2 * d_model)
            branch_proj(j)

    k_scr[:, 0:BLOCK, :] = k_scr[:, T:T + BLOCK, :]
    vt_scr[:, :, 0:BLOCK] = vt_scr[:, :, T:T + BLOCK]


def _const_spec(shape):
    return pl.BlockSpec(shape, lambda b, s: (0,) * len(shape), pipeline_mode=pl.Buffered(1))


def _layer(x, ada, positions, g_pre, g_post, w_in, sinks, ln_v_g, ln_v_b, w_s, b_s,
           w_proj_a, w_proj_b, w_out):
    B, S, D = x.shape
    T = SEQ_TILE
    assert S % T == 0 and T % SUB_TILE == 0 and SUB_TILE % BLOCK == 0 and SUB_TILE % CHUNK == 0
    d_a = N_Q_HEADS * HEAD_DIM
    d_kv = N_KV_HEADS * HEAD_DIM
    d_b = w_proj_b.shape[0]
    d_in = w_in.shape[1]
    assert d_in == 2 * d_a + 2 * d_kv + 3 * d_b + 2 * D

    half = HEAD_DIM // 2
    inv_freq = ROPE_THETA ** (-jnp.arange(half, dtype=F32) / half)
    invf = jnp.broadcast_to(inv_freq[:, None], (half, LANES))

    kern = functools.partial(_block_kernel, d_a=d_a, d_kv=d_kv, d_b=d_b, d_model=D)
    return pl.pallas_call(
        kern,
        out_shape=jax.ShapeDtypeStruct((B, S, D), x.dtype),
        grid=(B, S // T),
        in_specs=[
            pl.BlockSpec(memory_space=pltpu.SMEM),
            pl.BlockSpec((1, T, D), lambda b, s: (b, s, 0)),
            pl.BlockSpec((1, 1, T), lambda b, s: (b, 0, s)),
            pl.BlockSpec((1, 3, D), lambda b, s: (b, 0, 0)),
            _const_spec((1, D)),
            _const_spec((1, D)),
            _const_spec((half, LANES)),
            _const_spec((D, d_in)),
            _const_spec((1, d_b)),
            _const_spec((1, d_b)),
            _const_spec((N_GROUPS, CHUNK, CHUNK)),
            _const_spec((CHUNK, N_GROUPS)),
            _const_spec((d_a, D)),
            _const_spec((d_b, D)),
            _const_spec((D, D)),
        ],
        out_specs=pl.BlockSpec((1, T, D), lambda b, s: (b, s, 0)),
        scratch_shapes=[
            pltpu.VMEM((N_KV_HEADS, BLOCK + T, HEAD_DIM), BF16),
            pltpu.VMEM((N_KV_HEADS, HEAD_DIM, BLOCK + T), BF16),
        ],
        compiler_params=pltpu.CompilerParams(
            dimension_semantics=("arbitrary", "arbitrary"),
            vmem_limit_bytes=VMEM_LIMIT_BYTES),
        name="hybrid_block",
    )(sinks, x, positions.reshape(B, 1, S), ada.reshape(B, 3, D),
      g_pre[None, :], g_post[None, :], invf, w_in.astype(BF16),
      ln_v_g[None, :], ln_v_b[None, :], w_s, b_s.T,
      w_proj_a.astype(BF16), w_proj_b.astype(BF16), w_out.astype(BF16))


def _ada(c, w_ada, b_ada):
    B, D = c.shape
    n_out = w_ada.shape[1]
    tn = D
    return pl.pallas_call(
        _ada_kernel,
        out_shape=jax.ShapeDtypeStruct((B, n_out), F32),
        grid=(n_out // tn,),
        in_specs=[
            pl.BlockSpec((B, D), lambda j: (0, 0)),
            pl.BlockSpec((D, tn), lambda j: (0, j)),
            pl.BlockSpec((1, tn), lambda j: (0, j)),
        ],
        out_specs=pl.BlockSpec((B, tn), lambda j: (0, j)),
        compiler_params=pltpu.CompilerParams(dimension_semantics=("arbitrary",)),
        name="adaln_modulation",
    )(c, w_ada, b_ada[None, :])


def kernel(x, c, positions, w_ada, b_ada, g_pre, g_post, w_in, sinks, ln_v_g, ln_v_b, w_s, b_s,
           w_proj_a, w_proj_b, w_out):
    depth = w_in.shape[0]
    for l in range(depth):
        ada = _ada(c, w_ada[l], b_ada[l])
        x = _layer(x, ada, positions, g_pre[l], g_post[l], w_in[l], sinks[l], ln_v_g[l],
                   ln_v_b[l], w_s[l], b_s[l], w_proj_a[l], w_proj_b[l], w_out[l])
    return x
```

```python
import functools
import math

import jax
import jax.numpy as jnp
from jax import lax
from jax.experimental import pallas as pl
from jax.experimental.pallas import tpu as pltpu

HEAD_DIM = 64
N_Q_HEADS = 8
N_KV_HEADS = 2
GQ = N_Q_HEADS // N_KV_HEADS
WINDOW = 128
BLOCK = 128
ROPE_THETA = 10000.0
CHUNK = 128
N_GROUPS = 4
EPS = 1e-6
NEG = -1e30
LOG2E = 1.4426950408889634

LANES = 128
SEQ_TILE = 1024
SUB_TILE = 256
VMEM_LIMIT_BYTES = 56 * 1024 * 1024

BF16 = jnp.bfloat16
F32 = jnp.float32


def _ada_kernel(c_ref, w_ref, b_ref, o_ref):
    c = c_ref[...]
    c_act = c * jax.nn.sigmoid(c)
    o_ref[...] = jnp.dot(c_act, w_ref[...], preferred_element_type=F32) + b_ref[...]


def _sigmoid(x):
    return 0.5 * jnp.tanh(0.5 * x) + 0.5


def _silu(x):
    return x * _sigmoid(x)


def _gelu_exact(x):
    return 0.5 * x * (1.0 + lax.erf(x * (1.0 / math.sqrt(2.0))))


def _rope(t, cos, sin_signed, first_half):
    outs = []
    for c in range(t.shape[1] // LANES):
        tc = t[:, c * LANES:(c + 1) * LANES]
        rot = jnp.where(first_half,
                        pltpu.roll(tc, LANES - HEAD_DIM // 2, axis=1),
                        pltpu.roll(tc, HEAD_DIM // 2, axis=1))
        outs.append(tc * cos + rot * sin_signed)
    return outs


def _block_kernel(sinks_ref, x_ref, pos_ref, ada_ref, gpre_ref, gpost_ref, invf_ref,
                  win_ref, lng_ref, lnb_ref, ws_ref, bs_ref, wpa_ref, wpb_ref, wout_ref,
                  o_ref, k_scr, vt_scr, *, d_a, d_kv, d_b, d_model):
    T = x_ref.shape[1]
    R = SUB_TILE
    n_sub = T // R
    blk_per_sub = R // BLOCK
    s_idx = pl.program_id(1)

    c_q, c_k, c_v = 0, d_a, d_a + d_kv
    c_za = c_v + d_kv
    c_u = c_za + d_a
    c_vb = c_u + d_b
    c_zb = c_vb + d_b
    c_g = c_zb + d_b

    @pl.when(s_idx == 0)
    def _():
        k_scr[:, 0:BLOCK, :] = jnp.zeros((N_KV_HEADS, BLOCK, HEAD_DIM), BF16)
        vt_scr[:, :, 0:BLOCK] = jnp.zeros((N_KV_HEADS, HEAD_DIM, BLOCK), BF16)

    ada = ada_ref[0]
    shift, scale, gate = ada[0:1], ada[1:2], ada[2:3]
    pre_gain = gpre_ref[...] * (1.0 + scale)
    invf = invf_ref[...]
    lane = lax.broadcasted_iota(jnp.int32, (1, LANES), 1)
    first_half = (lane % HEAD_DIM) < (HEAD_DIM // 2)

    kj = lax.broadcasted_iota(jnp.int32, (2 * BLOCK, BLOCK), 0)
    qi = lax.broadcasted_iota(jnp.int32, (2 * BLOCK, BLOCK), 1)
    rel = qi + BLOCK - kj
    in_win = (rel >= 0) & (rel < WINDOW)
    first_lo = jnp.where(s_idx == 0, BLOCK, 0)
    ti = lax.broadcasted_iota(jnp.int32, (CHUNK, CHUNK), 0)
    si = lax.broadcasted_iota(jnp.int32, (CHUNK, CHUNK), 1)
    causal = si <= ti
    group_w = d_b // N_GROUPS
    q_scale = LOG2E / math.sqrt(HEAD_DIM)

    st = [dict() for _ in range(n_sub)]

    def proj(j, lo, hi):
        return jnp.dot(st[j]["hb"], win_ref[:, lo:hi], preferred_element_type=F32)

    def head(j):
        xc = x_ref[0, j * R:(j + 1) * R, :]
        ms = jnp.mean(xc * xc, axis=-1, keepdims=True)
        st[j]["hb"] = ((xc * lax.rsqrt(ms + EPS)) * pre_gain + shift).astype(BF16)
        st[j]["qkv"] = proj(j, c_q, c_za)

    def gmlp_in(j):
        v = _gelu_exact(proj(j, c_vb, c_zb))
        mu = jnp.mean(v, axis=-1, keepdims=True)
        vc = v - mu
        var = jnp.mean(vc * vc, axis=-1, keepdims=True)
        st[j]["vn"] = (vc * lax.rsqrt(var + EPS) * lng_ref[...] + lnb_ref[...]).astype(BF16)
        st[j]["u"] = _gelu_exact(proj(j, c_u, c_vb))

    def rope_and_scores(j):
        pos = pos_ref[0, :, j * R:(j + 1) * R].astype(F32)
        cos_rows, sin_rows = [], []
        for b in range(blk_per_sub):
            ang = invf * pos[:, b * BLOCK:(b + 1) * BLOCK]
            cs = jnp.cos(ang)
            sn = jnp.sin(ang)
            cos_rows.append(jnp.concatenate([cs, cs, cs, cs], axis=0).T)
            sin_rows.append(jnp.concatenate([-sn, sn, -sn, sn], axis=0).T)
        cos = jnp.concatenate(cos_rows, axis=0)
        sin_signed = jnp.concatenate(sin_rows, axis=0)
        qkv = st[j]["qkv"]
        q_tiles = _rope(qkv[:, c_q:c_k], cos, sin_signed, first_half)
        k_r = _rope(qkv[:, c_k:c_v], cos, sin_signed, first_half)[0]
        vt_new = qkv[:, c_v:c_za].T
        lo = BLOCK + j * R
        for g in range(N_KV_HEADS):
            k_scr[g, lo:lo + R, :] = k_r[:, g * HEAD_DIM:(g + 1) * HEAD_DIM].astype(BF16)
            vt_scr[g, :, lo:lo + R] = vt_new[g * HEAD_DIM:(g + 1) * HEAD_DIM, :].astype(BF16)
        q_heads = []
        for c in range(len(q_tiles)):
            qs = (q_tiles[c] * q_scale).astype(BF16)
            q_heads.append(qs[:, 0:HEAD_DIM])
            q_heads.append(qs[:, HEAD_DIM:2 * HEAD_DIM])
        ss = []
        for b in range(blk_per_sub):
            n = j * blk_per_sub + b
            band = slice(n * BLOCK, n * BLOCK + 2 * BLOCK)
            for g in range(N_KV_HEADS):
                kb = k_scr[g, band, :]
                q_stack = jnp.concatenate(
                    [q_heads[g * GQ + h][b * BLOCK:(b + 1) * BLOCK] for h in range(GQ)], axis=0)
                ss.append(lax.dot_general(kb, q_stack, (((1,), (1,)), ((), ())),
                                          preferred_element_type=F32))
        st[j]["scores"] = ss

    def pv_unit(j, idx):
        b, g = divmod(idx, N_KV_HEADS)
        n = j * blk_per_sub + b
        band = slice(n * BLOCK, n * BLOCK + 2 * BLOCK)
        s = st[j]["scores"][idx]
        valid = in_win & (kj >= first_lo) if n == 0 else in_win
        s = jnp.where(jnp.concatenate([valid] * GQ, axis=1), s, NEG)
        sink = jnp.concatenate(
            [jnp.full((1, BLOCK), sinks_ref[g * GQ + h] * LOG2E, F32) for h in range(GQ)], axis=1)
        m = jnp.maximum(jnp.max(s, axis=0, keepdims=True), sink)
        p = jnp.exp2(s - m)
        denom = jnp.sum(p, axis=0, keepdims=True) + jnp.exp2(sink - m)
        ot = jnp.dot(vt_scr[g, :, band], p.astype(BF16),
                     preferred_element_type=F32) * (1.0 / denom)
        for pair in range(GQ // 2):
            two = jnp.concatenate(
                [ot[:, (2 * pair) * BLOCK:(2 * pair + 1) * BLOCK],
                 ot[:, (2 * pair + 1) * BLOCK:(2 * pair + 2) * BLOCK]], axis=0)
            st[j]["attn"][b][g * (GQ // 2) + pair] = two.T

    def gating(j):
        rows_out = []
        for n in range(R // CHUNK):
            rows = slice(n * CHUNK, (n + 1) * CHUNK)
            cols_out = []
            for g in range(N_GROUPS):
                w_g = jnp.where(causal, ws_ref[g], 0.0).astype(BF16)
                bias = jnp.broadcast_to(bs_ref[:, g:g + 1], (CHUNK, group_w))
                cols = slice(g * group_w, (g + 1) * group_w)
                sv = jnp.dot(w_g, st[j]["vn"][rows, cols], preferred_element_type=F32) + bias
                cols_out.append(st[j]["u"][rows, cols] * sv)
            rows_out.append(jnp.concatenate(cols_out, axis=1))
        st[j]["yb"] = jnp.concatenate(rows_out, axis=0)

    def branch_proj(j):
        attn = jnp.concatenate([jnp.concatenate(row, axis=1) for row in st[j]["attn"]], axis=0)
        y_a = attn * _silu(st[j]["za"])
        st[j]["pa"] = jnp.dot(y_a.astype(BF16), wpa_ref[...], preferred_element_type=F32)
        y_b = st[j]["yb"] * _silu(st[j]["zb"])
        st[j]["pb"] = jnp.dot(y_b.astype(BF16), wpb_ref[...], preferred_element_type=F32)

    def out_proj(j):
        merged = (_sigmoid(st[j]["ga"]) * st[j]["pa"]
                  + _sigmoid(st[j]["gb"]) * st[j]["pb"]).astype(BF16)
        y = jnp.dot(merged, wout_ref[...], preferred_element_type=F32)
        ms_y = jnp.mean(y * y, axis=-1, keepdims=True)
        rows = slice(j * R, (j + 1) * R)
        o_ref[0, rows, :] = x_ref[0, rows, :] + gate * (y * lax.rsqrt(ms_y + EPS) * gpost_ref[...])
        st[j].clear()

    half = d_model // 2
    n_units = blk_per_sub * N_KV_HEADS
    assert n_units * half == 2 * d_model
    for j in range(n_sub + 1):
        if j < n_sub:
            head(j)
            gmlp_in(j)
        if j >= 1:
            out_proj(j - 1)
        if j < n_sub:
            rope_and_scores(j)
            st[j]["attn"] = [[None] * (N_Q_HEADS // 2) for _ in range(blk_per_sub)]
            st[j]["za"] = proj(j, c_za, c_u)
            st[j]["zb"] = proj(j, c_zb, c_g)
            gates = []
            for idx in range(n_units):
                pv_unit(j, idx)
                lo = c_g + idx * half
                gates.append(proj(j, lo, lo + half))
            st[j]["ga"] = jnp.concatenate(gates[0:2], axis=1)
            gating(j)
            st[j]["gb"] = jnp.concatenate(gates[2:4], axis=1)
            branch_proj(j)

    k_scr[:, 0:BLOCK, :] = k_scr[:, T:T + BLOCK, :]
    vt_scr[:, :, 0:BLOCK] = vt_scr[:, :, T:T + BLOCK]


def _const_spec(shape):
    return pl.BlockSpec(shape, lambda b, s: (0,) * len(shape), pipeline_mode=pl.Buffered(1))


def _layer(x, ada, positions, g_pre, g_post, w_in, sinks, ln_v_g, ln_v_b, w_s, b_s,
           w_proj_a, w_proj_b, w_out):
    B, S, D = x.shape
    T = SEQ_TILE
    assert S % T == 0 and T % SUB_TILE == 0 and SUB_TILE % BLOCK == 0 and SUB_TILE % CHUNK == 0
    d_a = N_Q_HEADS * HEAD_DIM
    d_kv = N_KV_HEADS * HEAD_DIM
    d_b = w_proj_b.shape[0]
    d_in = w_in.shape[1]
    assert d_in == 2 * d_a + 2 * d_kv + 3 * d_b + 2 * D

    half = HEAD_DIM // 2
    inv_freq = ROPE_THETA ** (-jnp.arange(half, dtype=F32) / half)
    invf = jnp.broadcast_to(inv_freq[:, None], (half, LANES))

    kern = functools.partial(_block_kernel, d_a=d_a, d_kv=d_kv, d_b=d_b, d_model=D)
    return pl.pallas_call(
        kern,
        out_shape=jax.ShapeDtypeStruct((B, S, D), x.dtype),
        grid=(B, S // T),
        in_specs=[
            pl.BlockSpec(memory_space=pltpu.SMEM),
            pl.BlockSpec((1, T, D), lambda b, s: (b, s, 0)),
            pl.BlockSpec((1, 1, T), lambda b, s: (b, 0, s)),
            pl.BlockSpec((1, 3, D), lambda b, s: (b, 0, 0)),
            _const_spec((1, D)),
            _const_spec((1, D)),
            _const_spec((half, LANES)),
            _const_spec((D, d_in)),
            _const_spec((1, d_b)),
            _const_spec((1, d_b)),
            _const_spec((N_GROUPS, CHUNK, CHUNK)),
            _const_spec((CHUNK, N_GROUPS)),
            _const_spec((d_a, D)),
            _const_spec((d_b, D)),
            _const_spec((D, D)),
        ],
        out_specs=pl.BlockSpec((1, T, D), lambda b, s: (b, s, 0)),
        scratch_shapes=[
            pltpu.VMEM((N_KV_HEADS, BLOCK + T, HEAD_DIM), BF16),
            pltpu.VMEM((N_KV_HEADS, HEAD_DIM, BLOCK + T), BF16),
        ],
        compiler_params=pltpu.CompilerParams(
            dimension_semantics=("arbitrary", "arbitrary"),
            vmem_limit_bytes=VMEM_LIMIT_BYTES),
        name="hybrid_block",
    )(sinks, x, positions.reshape(B, 1, S), ada.reshape(B, 3, D),
      g_pre[None, :], g_post[None, :], invf, w_in.astype(BF16),
      ln_v_g[None, :], ln_v_b[None, :], w_s, b_s.T,
      w_proj_a.astype(BF16), w_proj_b.astype(BF16), w_out.astype(BF16))


def _ada(c, w_ada, b_ada):
    B, D = c.shape
    n_out = w_ada.shape[1]
    tn = D
    return pl.pallas_call(
        _ada_kernel,
        out_shape=jax.ShapeDtypeStruct((B, n_out), F32),
        grid=(n_out // tn,),
        in_specs=[
            pl.BlockSpec((B, D), lambda j: (0, 0)),
            pl.BlockSpec((D, tn), lambda j: (0, j)),
            pl.BlockSpec((1, tn), lambda j: (0, j)),
        ],
        out_specs=pl.BlockSpec((B, tn), lambda j: (0, j)),
        compiler_params=pltpu.CompilerParams(dimension_semantics=("arbitrary",)),
        name="adaln_modulation",
    )(c, w_ada, b_ada[None, :])


def kernel(x, c, positions, w_ada, b_ada, g_pre, g_post, w_in, sinks, ln_v_g, ln_v_b, w_s, b_s,
           w_proj_a, w_proj_b, w_out):
    depth = w_in.shape[0]
    for l in range(depth):
        ada = _ada(c, w_ada[l], b_ada[l])
        x = _layer(x, ada, positions, g_pre[l], g_post[l], w_in[l], sinks[l], ln_v_g[l],
                   ln_v_b[l], w_s[l], b_s[l], w_proj_a[l], w_proj_b[l], w_out[l])
    return x
```

```python
import functools
import math

import jax
import jax.numpy as jnp
from jax import lax
from jax.experimental import pallas as pl
from jax.experimental.pallas import tpu as pltpu

HEAD_DIM = 64
N_Q_HEADS = 8
N_KV_HEADS = 2
GQ = N_Q_HEADS // N_KV_HEADS
WINDOW = 128
BLOCK = 128
ROPE_THETA = 10000.0
CHUNK = 128
N_GROUPS = 4
EPS = 1e-6
NEG = -1e30
LOG2E = 1.4426950408889634

LANES = 128
SEQ_TILE = 1024
SUB_TILE = 256
VMEM_LIMIT_BYTES = 56 * 1024 * 1024

BF16 = jnp.bfloat16
F32 = jnp.float32


def _ada_kernel(c_ref, w_ref, b_ref, o_ref):
    c = c_ref[...]
    c_act = c * jax.nn.sigmoid(c)
    o_ref[...] = jnp.dot(c_act, w_ref[...], preferred_element_type=F32) + b_ref[...]


def _sigmoid(x):
    return 0.5 * jnp.tanh(0.5 * x) + 0.5


def _silu(x):
    return x * _sigmoid(x)


def _gelu_exact(x):
    return 0.5 * x * (1.0 + lax.erf(x * (1.0 / math.sqrt(2.0))))


def _rope(t, cos, sin_signed, first_half):
    outs = []
    for c in range(t.shape[1] // LANES):
        tc = t[:, c * LANES:(c + 1) * LANES]
        rot = jnp.where(first_half,
                        pltpu.roll(tc, LANES - HEAD_DIM // 2, axis=1),
                        pltpu.roll(tc, HEAD_DIM // 2, axis=1))
        outs.append(tc * cos + rot * sin_signed)
    return outs


def _block_kernel(sinks_ref, x_ref, pos_ref, ada_ref, gpre_ref, gpost_ref, invf_ref,
                  win_ref, lng_ref, lnb_ref, ws_ref, bs_ref, wpa_ref, wpb_ref, wout_ref,
                  o_ref, k_scr, vt_scr, *, d_a, d_kv, d_b, d_model):
    T = x_ref.shape[1]
    R = SUB_TILE
    n_sub = T // R
    blk_per_sub = R // BLOCK
    s_idx = pl.program_id(1)

    c_q, c_k, c_v = 0, d_a, d_a + d_kv
    c_za = c_v + d_kv
    c_u = c_za + d_a
    c_vb = c_u + d_b
    c_zb = c_vb + d_b
    c_g = c_zb + d_b

    @pl.when(s_idx == 0)
    def _():
        k_scr[:, 0:BLOCK, :] = jnp.zeros((N_KV_HEADS, BLOCK, HEAD_DIM), BF16)
        vt_scr[:, :, 0:BLOCK] = jnp.zeros((N_KV_HEADS, HEAD_DIM, BLOCK), BF16)

    ada = ada_ref[0]
    shift, scale, gate = ada[0:1], ada[1:2], ada[2:3]
    pre_gain = gpre_ref[...] * (1.0 + scale)
    invf = invf_ref[...]
    lane = lax.broadcasted_iota(jnp.int32, (1, LANES), 1)
    first_half = (lane % HEAD_DIM) < (HEAD_DIM // 2)

    kj = lax.broadcasted_iota(jnp.int32, (2 * BLOCK, BLOCK), 0)
    qi = lax.broadcasted_iota(jnp.int32, (2 * BLOCK, BLOCK), 1)
    rel = qi + BLOCK - kj
    in_win = (rel >= 0) & (rel < WINDOW)
    first_lo = jnp.where(s_idx == 0, BLOCK, 0)
    ti = lax.broadcasted_iota(jnp.int32, (CHUNK, CHUNK), 0)
    si = lax.broadcasted_iota(jnp.int32, (CHUNK, CHUNK), 1)
    causal = si <= ti
    group_w = d_b // N_GROUPS
    q_scale = LOG2E / math.sqrt(HEAD_DIM)

    st = [dict() for _ in range(n_sub)]

    def proj(j, lo, hi):
        return jnp.dot(st[j]["hb"], win_ref[:, lo:hi], preferred_element_type=F32)

    def head(j):
        xc = x_ref[0, j * R:(j + 1) * R, :]
        ms = jnp.mean(xc * xc, axis=-1, keepdims=True)
        st[j]["hb"] = ((xc * lax.rsqrt(ms + EPS)) * pre_gain + shift).astype(BF16)
        st[j]["qkv"] = proj(j, c_q, c_za)

    def gmlp_in(j):
        v = _gelu_exact(proj(j, c_vb, c_zb))
        mu = jnp.mean(v, axis=-1, keepdims=True)
        vc = v - mu
        var = jnp.mean(vc * vc, axis=-1, keepdims=True)
        st[j]["vn"] = (vc * lax.rsqrt(var + EPS) * lng_ref[...] + lnb_ref[...]).astype(BF16)
        st[j]["u"] = _gelu_exact(proj(j, c_u, c_vb))

    def rope_and_scores(j):
        pos = pos_ref[0, :, j * R:(j + 1) * R].astype(F32)
        cos_rows, sin_rows = [], []
        for b in range(blk_per_sub):
            ang = invf * pos[:, b * BLOCK:(b + 1) * BLOCK]
            cs = jnp.cos(ang)
            sn = jnp.sin(ang)
            cos_rows.append(jnp.concatenate([cs, cs, cs, cs], axis=0).T)
            sin_rows.append(jnp.concatenate([-sn, sn, -sn, sn], axis=0).T)
        cos = jnp.concatenate(cos_rows, axis=0)
        sin_signed = jnp.concatenate(sin_rows, axis=0)
        qkv = st[j]["qkv"]
        q_tiles = _rope(qkv[:, c_q:c_k], cos, sin_signed, first_half)
        k_r = _rope(qkv[:, c_k:c_v], cos, sin_signed, first_half)[0]
        vt_new = qkv[:, c_v:c_za].T
        lo = BLOCK + j * R
        for g in range(N_KV_HEADS):
            k_scr[g, lo:lo + R, :] = k_r[:, g * HEAD_DIM:(g + 1) * HEAD_DIM].astype(BF16)
            vt_scr[g, :, lo:lo + R] = vt_new[g * HEAD_DIM:(g + 1) * HEAD_DIM, :].astype(BF16)
        q_heads = []
        for c in range(len(q_tiles)):
            qs = (q_tiles[c] * q_scale).astype(BF16)
            q_heads.append(qs[:, 0:HEAD_DIM])
            q_heads.append(qs[:, HEAD_DIM:2 * HEAD_DIM])
        ss = []
        for b in range(blk_per_sub):
            n = j * blk_per_sub + b
            band = slice(n * BLOCK, n * BLOCK + 2 * BLOCK)
            for g in range(N_KV_HEADS):
                kb = k_scr[g, band, :]
                q_stack = jnp.concatenate(
                    [q_heads[g * GQ + h][b * BLOCK:(b + 1) * BLOCK] for h in range(GQ)], axis=0)
                ss.append(lax.dot_general(kb, q_stack, (((1,), (1,)), ((), ())),
                                          preferred_element_type=F32))
        st[j]["scores"] = ss

    def pv_unit(j, idx):
        b, g = divmod(idx, N_KV_HEADS)
        n = j * blk_per_sub + b
        band = slice(n * BLOCK, n * BLOCK + 2 * BLOCK)
        s = st[j]["scores"][idx]
        valid = in_win & (kj >= first_lo) if n == 0 else in_win
        s = jnp.where(jnp.concatenate([valid] * GQ, axis=1), s, NEG)
        sink = jnp.concatenate(
            [jnp.full((1, BLOCK), sinks_ref[g * GQ + h] * LOG2E, F32) for h in range(GQ)], axis=1)
        m = jnp.maximum(jnp.max(s, axis=0, keepdims=True), sink)
        p = jnp.exp2(s - m)
        denom = jnp.sum(p, axis=0, keepdims=True) + jnp.exp2(sink - m)
        ot = jnp.dot(vt_scr[g, :, band], p.astype(BF16),
                     preferred_element_type=F32) * (1.0 / denom)
        for pair in range(GQ // 2):
            two = jnp.concatenate(
                [ot[:, (2 * pair) * BLOCK:(2 * pair + 1) * BLOCK],
                 ot[:, (2 * pair + 1) * BLOCK:(2 * pair + 2) * BLOCK]], axis=0)
            st[j]["attn"][b][g * (GQ // 2) + pair] = two.T

    def gating(j):
        rows_out = []
        for n in range(R // CHUNK):
            rows = slice(n * CHUNK, (n + 1) * CHUNK)
            cols_out = []
            for g in range(N_GROUPS):
                w_g = jnp.where(causal, ws_ref[g], 0.0).astype(BF16)
                bias = jnp.broadcast_to(bs_ref[:, g:g + 1], (CHUNK, group_w))
                cols = slice(g * group_w, (g + 1) * group_w)
                sv = jnp.dot(w_g, st[j]["vn"][rows, cols], preferred_element_type=F32) + bias
                cols_out.append(st[j]["u"][rows, cols] * sv)
            rows_out.append(jnp.concatenate(cols_out, axis=1))
        st[j]["yb"] = jnp.concatenate(rows_out, axis=0)

    def branch_proj(j):
        attn = jnp.concatenate([jnp.concatenate(row, axis=1) for row in st[j]["attn"]], axis=0)
        y_a = attn * _silu(st[j]["za"])
        st[j]["pa"] = jnp.dot(y_a.astype(BF16), wpa_ref[...], preferred_element_type=F32)
        y_b = st[j]["yb"] * _silu(st[j]["zb"])
        st[j]["pb"] = jnp.dot(y_b.astype(BF16), wpb_ref[...], preferred_element_type=F32)

    def out_proj(j):
        merged = (_sigmoid(st[j]["ga"]) * st[j]["pa"]
                  + _sigmoid(st[j]["gb"]) * st[j]["pb"]).astype(BF16)
        y = jnp.dot(merged, wout_ref[...], preferred_element_type=F32)
        ms_y = jnp.mean(y * y, axis=-1, keepdims=True)
        rows = slice(j * R, (j + 1) * R)
        o_ref[0, rows, :] = x_ref[0, rows, :] + gate * (y * lax.rsqrt(ms_y + EPS) * gpost_ref[...])
        st[j].clear()

    n_units = blk_per_sub * N_KV_HEADS
    for j in range(n_sub + 1):
        if j < n_sub:
            head(j)
            gmlp_in(j)
        if j >= 1:
            out_proj(j - 1)
        if j < n_sub:
            rope_and_scores(j)
            st[j]["attn"] = [[None] * (N_Q_HEADS // 2) for _ in range(blk_per_sub)]
            st[j]["za"] = proj(j, c_za, c_u)
            st[j]["zb"] = proj(j, c_zb, c_g)
            st[j]["ga"] = proj(j, c_g, c_g + d_model)
            for idx in range(n_units):
                pv_unit(j, idx)
            gating(j)
            st[j]["gb"] = proj(j, c_g + d_model, c_g + 2 * d_model)
            branch_proj(j)

    k_scr[:, 0:BLOCK, :] = k_scr[:, T:T + BLOCK, :]
    vt_scr[:, :, 0:BLOCK] = vt_scr[:, :, T:T + BLOCK]


def _const_spec(shape):
    return pl.BlockSpec(shape, lambda b, s: (0,) * len(shape), pipeline_mode=pl.Buffered(1))


def _layer(x, ada, positions, g_pre, g_post, w_in, sinks, ln_v_g, ln_v_b, w_s, b_s,
           w_proj_a, w_proj_b, w_out):
    B, S, D = x.shape
    T = SEQ_TILE
    assert S % T == 0 and T % SUB_TILE == 0 and SUB_TILE % BLOCK == 0 and SUB_TILE % CHUNK == 0
    d_a = N_Q_HEADS * HEAD_DIM
    d_kv = N_KV_HEADS * HEAD_DIM
    d_b = w_proj_b.shape[0]
    d_in = w_in.shape[1]
    assert d_in == 2 * d_a + 2 * d_kv + 3 * d_b + 2 * D

    half = HEAD_DIM // 2
    inv_freq = ROPE_THETA ** (-jnp.arange(half, dtype=F32) / half)
    invf = jnp.broadcast_to(inv_freq[:, None], (half, LANES))

    kern = functools.partial(_block_kernel, d_a=d_a, d_kv=d_kv, d_b=d_b, d_model=D)
    return pl.pallas_call(
        kern,
        out_shape=jax.ShapeDtypeStruct((B, S, D), x.dtype),
        grid=(B, S // T),
        in_specs=[
            pl.BlockSpec(memory_space=pltpu.SMEM),
            pl.BlockSpec((1, T, D), lambda b, s: (b, s, 0)),
            pl.BlockSpec((1, 1, T), lambda b, s: (b, 0, s)),
            pl.BlockSpec((1, 3, D), lambda b, s: (b, 0, 0)),
            _const_spec((1, D)),
            _const_spec((1, D)),
            _const_spec((half, LANES)),
            _const_spec((D, d_in)),
            _const_spec((1, d_b)),
            _const_spec((1, d_b)),
            _const_spec((N_GROUPS, CHUNK, CHUNK)),
            _const_spec((CHUNK, N_GROUPS)),
            _const_spec((d_a, D)),
            _const_spec((d_b, D)),
            _const_spec((D, D)),
        ],
        out_specs=pl.BlockSpec((1, T, D), lambda b, s: (b, s, 0)),
        scratch_shapes=[
            pltpu.VMEM((N_KV_HEADS, BLOCK + T, HEAD_DIM), BF16),
            pltpu.VMEM((N_KV_HEADS, HEAD_DIM, BLOCK + T), BF16),
        ],
        compiler_params=pltpu.CompilerParams(
            dimension_semantics=("arbitrary", "arbitrary"),
            vmem_limit_bytes=VMEM_LIMIT_BYTES),
        name="hybrid_block",
    )(sinks, x, positions.reshape(B, 1, S), ada.reshape(B, 3, D),
      g_pre[None, :], g_post[None, :], invf, w_in.astype(BF16),
      ln_v_g[None, :], ln_v_b[None, :], w_s, b_s.T,
      w_proj_a.astype(BF16), w_proj_b.astype(BF16), w_out.astype(BF16))


def _ada(c, w_ada, b_ada):
    B, D = c.shape
    n_out = w_ada.shape[1]
    tn = D
    return pl.pallas_call(
        _ada_kernel,
        out_shape=jax.ShapeDtypeStruct((B, n_out), F32),
        grid=(n_out // tn,),
        in_specs=[
            pl.BlockSpec((B, D), lambda j: (0, 0)),
            pl.BlockSpec((D, tn), lambda j: (0, j)),
            pl.BlockSpec((1, tn), lambda j: (0, j)),
        ],
        out_specs=pl.BlockSpec((B, tn), lambda j: (0, j)),
        compiler_params=pltpu.CompilerParams(dimension_semantics=("arbitrary",)),
        name="adaln_modulation",
    )(c, w_ada, b_ada[None, :])


def kernel(x, c, positions, w_ada, b_ada, g_pre, g_post, w_in, sinks, ln_v_g, ln_v_b, w_s, b_s,
           w_proj_a, w_proj_b, w_out):
    depth = w_in.shape[0]
    for l in range(depth):
        ada = _ada(c, w_ada[l], b_ada[l])
        x = _layer(x, ada, positions, g_pre[l], g_post[l], w_in[l], sinks[l], ln_v_g[l],
                   ln_v_b[l], w_s[l], b_s[l], w_proj_a[l], w_proj_b[l], w_out[l])
    return x
```

```python
import functools
import math

import jax
import jax.numpy as jnp
from jax import lax
from jax.experimental import pallas as pl
from jax.experimental.pallas import tpu as pltpu

HEAD_DIM = 64
N_Q_HEADS = 8
N_KV_HEADS = 2
GQ = N_Q_HEADS // N_KV_HEADS
WINDOW = 128
BLOCK = 128
ROPE_THETA = 10000.0
CHUNK = 128
N_GROUPS = 4
EPS = 1e-6
NEG = -1e30
LOG2E = 1.4426950408889634

LANES = 128
SEQ_TILE = 1024
SUB_TILE = 512
VMEM_LIMIT_BYTES = 56 * 1024 * 1024

BF16 = jnp.bfloat16
F32 = jnp.float32


def _ada_kernel(c_ref, w_ref, b_ref, o_ref):
    c = c_ref[...]
    c_act = c * jax.nn.sigmoid(c)
    o_ref[...] = jnp.dot(c_act, w_ref[...], preferred_element_type=F32) + b_ref[...]


def _sigmoid(x):
    return 0.5 * jnp.tanh(0.5 * x) + 0.5


def _silu(x):
    return x * _sigmoid(x)


def _gelu_exact(x):
    return 0.5 * x * (1.0 + lax.erf(x * (1.0 / math.sqrt(2.0))))


def _rope(t, cos, sin_signed, first_half):
    outs = []
    for c in range(t.shape[1] // LANES):
        tc = t[:, c * LANES:(c + 1) * LANES]
        rot = jnp.where(first_half,
                        pltpu.roll(tc, LANES - HEAD_DIM // 2, axis=1),
                        pltpu.roll(tc, HEAD_DIM // 2, axis=1))
        outs.append(tc * cos + rot * sin_signed)
    return outs


def _block_kernel(sinks_ref, x_ref, pos_ref, ada_ref, gpre_ref, gpost_ref, invf_ref,
                  win_ref, lng_ref, lnb_ref, ws_ref, bs_ref, wpa_ref, wpb_ref, wout_ref,
                  o_ref, k_scr, vt_scr, *, d_a, d_kv, d_b, d_model):
    T = x_ref.shape[1]
    R = SUB_TILE
    n_sub = T // R
    blk_per_sub = R // BLOCK
    s_idx = pl.program_id(1)

    c_q, c_k, c_v = 0, d_a, d_a + d_kv
    c_za = c_v + d_kv
    c_u = c_za + d_a
    c_vb = c_u + d_b
    c_zb = c_vb + d_b
    c_g = c_zb + d_b

    @pl.when(s_idx == 0)
    def _():
        k_scr[:, 0:BLOCK, :] = jnp.zeros((N_KV_HEADS, BLOCK, HEAD_DIM), BF16)
        vt_scr[:, :, 0:BLOCK] = jnp.zeros((N_KV_HEADS, HEAD_DIM, BLOCK), BF16)

    ada = ada_ref[0]
    shift, scale, gate = ada[0:1], ada[1:2], ada[2:3]
    pre_gain = gpre_ref[...] * (1.0 + scale)
    invf = invf_ref[...]
    lane = lax.broadcasted_iota(jnp.int32, (1, LANES), 1)
    first_half = (lane % HEAD_DIM) < (HEAD_DIM // 2)

    kj = lax.broadcasted_iota(jnp.int32, (2 * BLOCK, BLOCK), 0)
    qi = lax.broadcasted_iota(jnp.int32, (2 * BLOCK, BLOCK), 1)
    rel = qi + BLOCK - kj
    in_win = (rel >= 0) & (rel < WINDOW)
    first_lo = jnp.where(s_idx == 0, BLOCK, 0)
    ti = lax.broadcasted_iota(jnp.int32, (CHUNK, CHUNK), 0)
    si = lax.broadcasted_iota(jnp.int32, (CHUNK, CHUNK), 1)
    causal = si <= ti
    group_w = d_b // N_GROUPS
    q_scale = LOG2E / math.sqrt(HEAD_DIM)

    st = [dict() for _ in range(n_sub)]

    def proj(j, lo, hi):
        return jnp.dot(st[j]["hb"], win_ref[:, lo:hi], preferred_element_type=F32)

    def head(j):
        xc = x_ref[0, j * R:(j + 1) * R, :]
        ms = jnp.mean(xc * xc, axis=-1, keepdims=True)
        st[j]["hb"] = ((xc * lax.rsqrt(ms + EPS)) * pre_gain + shift).astype(BF16)
        st[j]["qkv"] = proj(j, c_q, c_za)

    def gmlp_in(j):
        v = _gelu_exact(proj(j, c_vb, c_zb))
        mu = jnp.mean(v, axis=-1, keepdims=True)
        vc = v - mu
        var = jnp.mean(vc * vc, axis=-1, keepdims=True)
        st[j]["vn"] = (vc * lax.rsqrt(var + EPS) * lng_ref[...] + lnb_ref[...]).astype(BF16)
        st[j]["u"] = _gelu_exact(proj(j, c_u, c_vb))

    def rope_and_scores(j):
        pos = pos_ref[0, :, j * R:(j + 1) * R].astype(F32)
        cos_rows, sin_rows = [], []
        for b in range(blk_per_sub):
            ang = invf * pos[:, b * BLOCK:(b + 1) * BLOCK]
            cs = jnp.cos(ang)
            sn = jnp.sin(ang)
            cos_rows.append(jnp.concatenate([cs, cs, cs, cs], axis=0).T)
            sin_rows.append(jnp.concatenate([-sn, sn, -sn, sn], axis=0).T)
        cos = jnp.concatenate(cos_rows, axis=0)
        sin_signed = jnp.concatenate(sin_rows, axis=0)
        qkv = st[j]["qkv"]
        q_tiles = _rope(qkv[:, c_q:c_k], cos, sin_signed, first_half)
        k_r = _rope(qkv[:, c_k:c_v], cos, sin_signed, first_half)[0]
        vt_new = qkv[:, c_v:c_za].T
        lo = BLOCK + j * R
        for g in range(N_KV_HEADS):
            k_scr[g, lo:lo + R, :] = k_r[:, g * HEAD_DIM:(g + 1) * HEAD_DIM].astype(BF16)
            vt_scr[g, :, lo:lo + R] = vt_new[g * HEAD_DIM:(g + 1) * HEAD_DIM, :].astype(BF16)
        q_heads = []
        for c in range(len(q_tiles)):
            qs = (q_tiles[c] * q_scale).astype(BF16)
            q_heads.append(qs[:, 0:HEAD_DIM])
            q_heads.append(qs[:, HEAD_DIM:2 * HEAD_DIM])
        ss = []
        for b in range(blk_per_sub):
            n = j * blk_per_sub + b
            band = slice(n * BLOCK, n * BLOCK + 2 * BLOCK)
            for g in range(N_KV_HEADS):
                kb = k_scr[g, band, :]
                q_stack = jnp.concatenate(
                    [q_heads[g * GQ + h][b * BLOCK:(b + 1) * BLOCK] for h in range(GQ)], axis=0)
                ss.append(lax.dot_general(kb, q_stack, (((1,), (1,)), ((), ())),
                                          preferred_element_type=F32))
        st[j]["scores"] = ss

    def pv_unit(j, idx):
        b, g = divmod(idx, N_KV_HEADS)
        n = j * blk_per_sub + b
        band = slice(n * BLOCK, n * BLOCK + 2 * BLOCK)
        s = st[j]["scores"][idx]
        valid = in_win & (kj >= first_lo) if n == 0 else in_win
        s = jnp.where(jnp.concatenate([valid] * GQ, axis=1), s, NEG)
        sink = jnp.concatenate(
            [jnp.full((1, BLOCK), sinks_ref[g * GQ + h] * LOG2E, F32) for h in range(GQ)], axis=1)
        m = jnp.maximum(jnp.max(s, axis=0, keepdims=True), sink)
        p = jnp.exp2(s - m)
        denom = jnp.sum(p, axis=0, keepdims=True) + jnp.exp2(sink - m)
        ot = jnp.dot(vt_scr[g, :, band], p.astype(BF16),
                     preferred_element_type=F32) * (1.0 / denom)
        for pair in range(GQ // 2):
            two = jnp.concatenate(
                [ot[:, (2 * pair) * BLOCK:(2 * pair + 1) * BLOCK],
                 ot[:, (2 * pair + 1) * BLOCK:(2 * pair + 2) * BLOCK]], axis=0)
            st[j]["attn"][b][g * (GQ // 2) + pair] = two.T

    def gating(j):
        rows_out = []
        for n in range(R // CHUNK):
            rows = slice(n * CHUNK, (n + 1) * CHUNK)
            cols_out = []
            for g in range(N_GROUPS):
                w_g = jnp.where(causal, ws_ref[g], 0.0).astype(BF16)
                bias = jnp.broadcast_to(bs_ref[:, g:g + 1], (CHUNK, group_w))
                cols = slice(g * group_w, (g + 1) * group_w)
                sv = jnp.dot(w_g, st[j]["vn"][rows, cols], preferred_element_type=F32) + bias
                cols_out.append(st[j]["u"][rows, cols] * sv)
            rows_out.append(jnp.concatenate(cols_out, axis=1))
        st[j]["yb"] = jnp.concatenate(rows_out, axis=0)

    def branch_proj(j):
        attn = jnp.concatenate([jnp.concatenate(row, axis=1) for row in st[j]["attn"]], axis=0)
        y_a = attn * _silu(st[j]["za"])
        st[j]["pa"] = jnp.dot(y_a.astype(BF16), wpa_ref[...], preferred_element_type=F32)
        y_b = st[j]["yb"] * _silu(st[j]["zb"])
        st[j]["pb"] = jnp.dot(y_b.astype(BF16), wpb_ref[...], preferred_element_type=F32)

    def out_proj(j):
        merged = (_sigmoid(st[j]["ga"]) * st[j]["pa"]
                  + _sigmoid(st[j]["gb"]) * st[j]["pb"]).astype(BF16)
        y = jnp.dot(merged, wout_ref[...], preferred_element_type=F32)
        ms_y = jnp.mean(y * y, axis=-1, keepdims=True)
        rows = slice(j * R, (j + 1) * R)
        o_ref[0, rows, :] = x_ref[0, rows, :] + gate * (y * lax.rsqrt(ms_y + EPS) * gpost_ref[...])
        st[j].clear()

    n_units = blk_per_sub * N_KV_HEADS
    for j in range(n_sub + 1):
        if j < n_sub:
            head(j)
            gmlp_in(j)
        if j >= 1:
            out_proj(j - 1)
        if j < n_sub:
            rope_and_scores(j)
            st[j]["attn"] = [[None] * (N_Q_HEADS // 2) for _ in range(blk_per_sub)]
            st[j]["za"] = proj(j, c_za, c_u)
            st[j]["zb"] = proj(j, c_zb, c_g)
            st[j]["ga"] = proj(j, c_g, c_g + d_model)
            for idx in range(n_units):
                pv_unit(j, idx)
            gating(j)
            st[j]["gb"] = proj(j, c_g + d_model, c_g + 2 * d_model)
            branch_proj(j)

    k_scr[:, 0:BLOCK, :] = k_scr[:, T:T + BLOCK, :]
    vt_scr[:, :, 0:BLOCK] = vt_scr[:, :, T:T + BLOCK]


def _const_spec(shape):
    return pl.BlockSpec(shape, lambda b, s: (0,) * len(shape), pipeline_mode=pl.Buffered(1))


def _layer(x, ada, positions, g_pre, g_post, w_in, sinks, ln_v_g, ln_v_b, w_s, b_s,
           w_proj_a, w_proj_b, w_out):
    B, S, D = x.shape
    T = SEQ_TILE
    assert S % T == 0 and T % SUB_TILE == 0 and SUB_TILE % BLOCK == 0 and SUB_TILE % CHUNK == 0
    d_a = N_Q_HEADS * HEAD_DIM
    d_kv = N_KV_HEADS * HEAD_DIM
    d_b = w_proj_b.shape[0]
    d_in = w_in.shape[1]
    assert d_in == 2 * d_a + 2 * d_kv + 3 * d_b + 2 * D

    half = HEAD_DIM // 2
    inv_freq = ROPE_THETA ** (-jnp.arange(half, dtype=F32) / half)
    invf = jnp.broadcast_to(inv_freq[:, None], (half, LANES))

    kern = functools.partial(_block_kernel, d_a=d_a, d_kv=d_kv, d_b=d_b, d_model=D)
    return pl.pallas_call(
        kern,
        out_shape=jax.ShapeDtypeStruct((B, S, D), x.dtype),
        grid=(B, S // T),
        in_specs=[
            pl.BlockSpec(memory_space=pltpu.SMEM),
            pl.BlockSpec((1, T, D), lambda b, s: (b, s, 0)),
            pl.BlockSpec((1, 1, T), lambda b, s: (b, 0, s)),
            pl.BlockSpec((1, 3, D), lambda b, s: (b, 0, 0)),
            _const_spec((1, D)),
            _const_spec((1, D)),
            _const_spec((half, LANES)),
            _const_spec((D, d_in)),
            _const_spec((1, d_b)),
            _const_spec((1, d_b)),
            _const_spec((N_GROUPS, CHUNK, CHUNK)),
            _const_spec((CHUNK, N_GROUPS)),
            _const_spec((d_a, D)),
            _const_spec((d_b, D)),
            _const_spec((D, D)),
        ],
        out_specs=pl.BlockSpec((1, T, D), lambda b, s: (b, s, 0)),
        scratch_shapes=[
            pltpu.VMEM((N_KV_HEADS, BLOCK + T, HEAD_DIM), BF16),
            pltpu.VMEM((N_KV_HEADS, HEAD_DIM, BLOCK + T), BF16),
        ],
        compiler_params=pltpu.CompilerParams(
            dimension_semantics=("arbitrary", "arbitrary"),
            vmem_limit_bytes=VMEM_LIMIT_BYTES),
        name="hybrid_block",
    )(sinks, x, positions.reshape(B, 1, S), ada.reshape(B, 3, D),
      g_pre[None, :], g_post[None, :], invf, w_in.astype(BF16),
      ln_v_g[None, :], ln_v_b[None, :], w_s, b_s.T,
      w_proj_a.astype(BF16), w_proj_b.astype(BF16), w_out.astype(BF16))


def _ada(c, w_ada, b_ada):
    B, D = c.shape
    n_out = w_ada.shape[1]
    tn = D
    return pl.pallas_call(
        _ada_kernel,
        out_shape=jax.ShapeDtypeStruct((B, n_out), F32),
        grid=(n_out // tn,),
        in_specs=[
            pl.BlockSpec((B, D), lambda j: (0, 0)),
            pl.BlockSpec((D, tn), lambda j: (0, j)),
            pl.BlockSpec((1, tn), lambda j: (0, j)),
        ],
        out_specs=pl.BlockSpec((B, tn), lambda j: (0, j)),
        compiler_params=pltpu.CompilerParams(dimension_semantics=("arbitrary",)),
        name="adaln_modulation",
    )(c, w_ada, b_ada[None, :])


def kernel(x, c, positions, w_ada, b_ada, g_pre, g_post, w_in, sinks, ln_v_g, ln_v_b, w_s, b_s,
           w_proj_a, w_proj_b, w_out):
    depth = w_in.shape[0]
    for l in range(depth):
        ada = _ada(c, w_ada[l], b_ada[l])
        x = _layer(x, ada, positions, g_pre[l], g_post[l], w_in[l], sinks[l], ln_v_g[l],
                   ln_v_b[l], w_s[l], b_s[l], w_proj_a[l], w_proj_b[l], w_out[l])
    return x
```

```python
import functools
import math

import jax
import jax.numpy as jnp
from jax import lax
from jax.experimental import pallas as pl
from jax.experimental.pallas import tpu as pltpu

HEAD_DIM = 64
N_Q_HEADS = 8
N_KV_HEADS = 2
GQ = N_Q_HEADS // N_KV_HEADS
WINDOW = 128
BLOCK = 128
ROPE_THETA = 10000.0
CHUNK = 128
N_GROUPS = 4
EPS = 1e-6
NEG = -1e30
LOG2E = 1.4426950408889634

LANES = 128
SEQ_TILE = 2048
SUB_TILE = 256
VMEM_LIMIT_BYTES = 62 * 1024 * 1024

BF16 = jnp.bfloat16
F32 = jnp.float32


def _ada_kernel(c_ref, w_ref, b_ref, o_ref):
    c = c_ref[...]
    c_act = c * jax.nn.sigmoid(c)
    o_ref[...] = jnp.dot(c_act, w_ref[...], preferred_element_type=F32) + b_ref[...]


def _sigmoid(x):
    return 0.5 * jnp.tanh(0.5 * x) + 0.5


def _silu(x):
    return x * _sigmoid(x)


def _gelu_exact(x):
    return 0.5 * x * (1.0 + lax.erf(x * (1.0 / math.sqrt(2.0))))


def _rope(t, cos, sin_signed, first_half):
    outs = []
    for c in range(t.shape[1] // LANES):
        tc = t[:, c * LANES:(c + 1) * LANES]
        rot = jnp.where(first_half,
                        pltpu.roll(tc, LANES - HEAD_DIM // 2, axis=1),
                        pltpu.roll(tc, HEAD_DIM // 2, axis=1))
        outs.append(tc * cos + rot * sin_signed)
    return outs


def _block_kernel(sinks_ref, x_ref, pos_ref, ada_ref, gpre_ref, gpost_ref, invf_ref,
                  win_ref, lng_ref, lnb_ref, ws_ref, bs_ref, wpa_ref, wpb_ref, wout_ref,
                  o_ref, k_scr, vt_scr, *, d_a, d_kv, d_b, d_model):
    T = x_ref.shape[1]
    R = SUB_TILE
    n_sub = T // R
    blk_per_sub = R // BLOCK
    s_idx = pl.program_id(1)

    c_q, c_k, c_v = 0, d_a, d_a + d_kv
    c_za = c_v + d_kv
    c_u = c_za + d_a
    c_vb = c_u + d_b
    c_zb = c_vb + d_b
    c_g = c_zb + d_b

    @pl.when(s_idx == 0)
    def _():
        k_scr[:, 0:BLOCK, :] = jnp.zeros((N_KV_HEADS, BLOCK, HEAD_DIM), BF16)
        vt_scr[:, :, 0:BLOCK] = jnp.zeros((N_KV_HEADS, HEAD_DIM, BLOCK), BF16)

    ada = ada_ref[0]
    shift, scale, gate = ada[0:1], ada[1:2], ada[2:3]
    pre_gain = gpre_ref[...] * (1.0 + scale)
    invf = invf_ref[...]
    lane = lax.broadcasted_iota(jnp.int32, (1, LANES), 1)
    first_half = (lane % HEAD_DIM) < (HEAD_DIM // 2)

    kj = lax.broadcasted_iota(jnp.int32, (2 * BLOCK, BLOCK), 0)
    qi = lax.broadcasted_iota(jnp.int32, (2 * BLOCK, BLOCK), 1)
    rel = qi + BLOCK - kj
    in_win = (rel >= 0) & (rel < WINDOW)
    first_lo = jnp.where(s_idx == 0, BLOCK, 0)
    ti = lax.broadcasted_iota(jnp.int32, (CHUNK, CHUNK), 0)
    si = lax.broadcasted_iota(jnp.int32, (CHUNK, CHUNK), 1)
    causal = si <= ti
    group_w = d_b // N_GROUPS
    q_scale = LOG2E / math.sqrt(HEAD_DIM)

    st = [dict() for _ in range(n_sub)]

    def proj(j, lo, hi):
        return jnp.dot(st[j]["hb"], win_ref[:, lo:hi], preferred_element_type=F32)

    def head(j):
        xc = x_ref[0, j * R:(j + 1) * R, :]
        ms = jnp.mean(xc * xc, axis=-1, keepdims=True)
        st[j]["hb"] = ((xc * lax.rsqrt(ms + EPS)) * pre_gain + shift).astype(BF16)
        st[j]["qkv"] = proj(j, c_q, c_za)

    def gmlp_in(j):
        v = _gelu_exact(proj(j, c_vb, c_zb))
        mu = jnp.mean(v, axis=-1, keepdims=True)
        vc = v - mu
        var = jnp.mean(vc * vc, axis=-1, keepdims=True)
        st[j]["vn"] = (vc * lax.rsqrt(var + EPS) * lng_ref[...] + lnb_ref[...]).astype(BF16)
        st[j]["u"] = _gelu_exact(proj(j, c_u, c_vb))

    def rope_and_scores(j):
        pos = pos_ref[0, :, j * R:(j + 1) * R].astype(F32)
        cos_rows, sin_rows = [], []
        for b in range(blk_per_sub):
            ang = invf * pos[:, b * BLOCK:(b + 1) * BLOCK]
            cs = jnp.cos(ang)
            sn = jnp.sin(ang)
            cos_rows.append(jnp.concatenate([cs, cs, cs, cs], axis=0).T)
            sin_rows.append(jnp.concatenate([-sn, sn, -sn, sn], axis=0).T)
        cos = jnp.concatenate(cos_rows, axis=0)
        sin_signed = jnp.concatenate(sin_rows, axis=0)
        qkv = st[j]["qkv"]
        q_tiles = _rope(qkv[:, c_q:c_k], cos, sin_signed, first_half)
        k_r = _rope(qkv[:, c_k:c_v], cos, sin_signed, first_half)[0]
        vt_new = qkv[:, c_v:c_za].T
        lo = BLOCK + j * R
        for g in range(N_KV_HEADS):
            k_scr[g, lo:lo + R, :] = k_r[:, g * HEAD_DIM:(g + 1) * HEAD_DIM].astype(BF16)
            vt_scr[g, :, lo:lo + R] = vt_new[g * HEAD_DIM:(g + 1) * HEAD_DIM, :].astype(BF16)
        q_heads = []
        for c in range(len(q_tiles)):
            qs = (q_tiles[c] * q_scale).astype(BF16)
            q_heads.append(qs[:, 0:HEAD_DIM])
            q_heads.append(qs[:, HEAD_DIM:2 * HEAD_DIM])
        ss = []
        for b in range(blk_per_sub):
            n = j * blk_per_sub + b
            band = slice(n * BLOCK, n * BLOCK + 2 * BLOCK)
            for g in range(N_KV_HEADS):
                kb = k_scr[g, band, :]
                q_stack = jnp.concatenate(
                    [q_heads[g * GQ + h][b * BLOCK:(b + 1) * BLOCK] for h in range(GQ)], axis=0)
                ss.append(lax.dot_general(kb, q_stack, (((1,), (1,)), ((), ())),
                                          preferred_element_type=F32))
        st[j]["scores"] = ss

    def pv_unit(j, idx):
        b, g = divmod(idx, N_KV_HEADS)
        n = j * blk_per_sub + b
        band = slice(n * BLOCK, n * BLOCK + 2 * BLOCK)
        s = st[j]["scores"][idx]
        valid = in_win & (kj >= first_lo) if n == 0 else in_win
        s = jnp.where(jnp.concatenate([valid] * GQ, axis=1), s, NEG)
        sink = jnp.concatenate(
            [jnp.full((1, BLOCK), sinks_ref[g * GQ + h] * LOG2E, F32) for h in range(GQ)], axis=1)
        m = jnp.maximum(jnp.max(s, axis=0, keepdims=True), sink)
        p = jnp.exp2(s - m)
        denom = jnp.sum(p, axis=0, keepdims=True) + jnp.exp2(sink - m)
        ot = jnp.dot(vt_scr[g, :, band], p.astype(BF16),
                     preferred_element_type=F32) * (1.0 / denom)
        for pair in range(GQ // 2):
            two = jnp.concatenate(
                [ot[:, (2 * pair) * BLOCK:(2 * pair + 1) * BLOCK],
                 ot[:, (2 * pair + 1) * BLOCK:(2 * pair + 2) * BLOCK]], axis=0)
            st[j]["attn"][b][g * (GQ // 2) + pair] = two.T

    def gating(j):
        rows_out = []
        for n in range(R // CHUNK):
            rows = slice(n * CHUNK, (n + 1) * CHUNK)
            cols_out = []
            for g in range(N_GROUPS):
                w_g = jnp.where(causal, ws_ref[g], 0.0).astype(BF16)
                bias = jnp.broadcast_to(bs_ref[:, g:g + 1], (CHUNK, group_w))
                cols = slice(g * group_w, (g + 1) * group_w)
                sv = jnp.dot(w_g, st[j]["vn"][rows, cols], preferred_element_type=F32) + bias
                cols_out.append(st[j]["u"][rows, cols] * sv)
            rows_out.append(jnp.concatenate(cols_out, axis=1))
        st[j]["yb"] = jnp.concatenate(rows_out, axis=0)

    def branch_proj(j):
        attn = jnp.concatenate([jnp.concatenate(row, axis=1) for row in st[j]["attn"]], axis=0)
        y_a = attn * _silu(st[j]["za"])
        st[j]["pa"] = jnp.dot(y_a.astype(BF16), wpa_ref[...], preferred_element_type=F32)
        y_b = st[j]["yb"] * _silu(st[j]["zb"])
        st[j]["pb"] = jnp.dot(y_b.astype(BF16), wpb_ref[...], preferred_element_type=F32)

    def out_proj(j):
        merged = (_sigmoid(st[j]["ga"]) * st[j]["pa"]
                  + _sigmoid(st[j]["gb"]) * st[j]["pb"]).astype(BF16)
        y = jnp.dot(merged, wout_ref[...], preferred_element_type=F32)
        ms_y = jnp.mean(y * y, axis=-1, keepdims=True)
        rows = slice(j * R, (j + 1) * R)
        o_ref[0, rows, :] = x_ref[0, rows, :] + gate * (y * lax.rsqrt(ms_y + EPS) * gpost_ref[...])
        st[j].clear()

    n_units = blk_per_sub * N_KV_HEADS
    for j in range(n_sub + 1):
        if j < n_sub:
            head(j)
            gmlp_in(j)
        if j >= 1:
            out_proj(j - 1)
        if j < n_sub:
            rope_and_scores(j)
            st[j]["attn"] = [[None] * (N_Q_HEADS // 2) for _ in range(blk_per_sub)]
            st[j]["za"] = proj(j, c_za, c_u)
            st[j]["zb"] = proj(j, c_zb, c_g)
            st[j]["ga"] = proj(j, c_g, c_g + d_model)
            for idx in range(n_units):
                pv_unit(j, idx)
            gating(j)
            st[j]["gb"] = proj(j, c_g + d_model, c_g + 2 * d_model)
            branch_proj(j)

    k_scr[:, 0:BLOCK, :] = k_scr[:, T:T + BLOCK, :]
    vt_scr[:, :, 0:BLOCK] = vt_scr[:, :, T:T + BLOCK]


def _const_spec(shape):
    return pl.BlockSpec(shape, lambda b, s: (0,) * len(shape), pipeline_mode=pl.Buffered(1))


def _layer(x, ada, positions, g_pre, g_post, w_in, sinks, ln_v_g, ln_v_b, w_s, b_s,
           w_proj_a, w_proj_b, w_out):
    B, S, D = x.shape
    T = SEQ_TILE
    assert S % T == 0 and T % SUB_TILE == 0 and SUB_TILE % BLOCK == 0 and SUB_TILE % CHUNK == 0
    d_a = N_Q_HEADS * HEAD_DIM
    d_kv = N_KV_HEADS * HEAD_DIM
    d_b = w_proj_b.shape[0]
    d_in = w_in.shape[1]
    assert d_in == 2 * d_a + 2 * d_kv + 3 * d_b + 2 * D

    half = HEAD_DIM // 2
    inv_freq = ROPE_THETA ** (-jnp.arange(half, dtype=F32) / half)
    invf = jnp.broadcast_to(inv_freq[:, None], (half, LANES))

    kern = functools.partial(_block_kernel, d_a=d_a, d_kv=d_kv, d_b=d_b, d_model=D)
    return pl.pallas_call(
        kern,
        out_shape=jax.ShapeDtypeStruct((B, S, D), x.dtype),
        grid=(B, S // T),
        in_specs=[
            pl.BlockSpec(memory_space=pltpu.SMEM),
            pl.BlockSpec((1, T, D), lambda b, s: (b, s, 0)),
            pl.BlockSpec((1, 1, T), lambda b, s: (b, 0, s)),
            pl.BlockSpec((1, 3, D), lambda b, s: (b, 0, 0)),
            _const_spec((1, D)),
            _const_spec((1, D)),
            _const_spec((half, LANES)),
            _const_spec((D, d_in)),
            _const_spec((1, d_b)),
            _const_spec((1, d_b)),
            _const_spec((N_GROUPS, CHUNK, CHUNK)),
            _const_spec((CHUNK, N_GROUPS)),
            _const_spec((d_a, D)),
            _const_spec((d_b, D)),
            _const_spec((D, D)),
        ],
        out_specs=pl.BlockSpec((1, T, D), lambda b, s: (b, s, 0)),
        scratch_shapes=[
            pltpu.VMEM((N_KV_HEADS, BLOCK + T, HEAD_DIM), BF16),
            pltpu.VMEM((N_KV_HEADS, HEAD_DIM, BLOCK + T), BF16),
        ],
        compiler_params=pltpu.CompilerParams(
            dimension_semantics=("arbitrary", "arbitrary"),
            vmem_limit_bytes=VMEM_LIMIT_BYTES),
        name="hybrid_block",
    )(sinks, x, positions.reshape(B, 1, S), ada.reshape(B, 3, D),
      g_pre[None, :], g_post[None, :], invf, w_in.astype(BF16),
      ln_v_g[None, :], ln_v_b[None, :], w_s, b_s.T,
      w_proj_a.astype(BF16), w_proj_b.astype(BF16), w_out.astype(BF16))


def _ada(c, w_ada, b_ada):
    B, D = c.shape
    n_out = w_ada.shape[1]
    tn = D
    return pl.pallas_call(
        _ada_kernel,
        out_shape=jax.ShapeDtypeStruct((B, n_out), F32),
        grid=(n_out // tn,),
        in_specs=[
            pl.BlockSpec((B, D), lambda j: (0, 0)),
            pl.BlockSpec((D, tn), lambda j: (0, j)),
            pl.BlockSpec((1, tn), lambda j: (0, j)),
        ],
        out_specs=pl.BlockSpec((B, tn), lambda j: (0, j)),
        compiler_params=pltpu.CompilerParams(dimension_semantics=("arbitrary",)),
        name="adaln_modulation",
    )(c, w_ada, b_ada[None, :])


def kernel(x, c, positions, w_ada, b_ada, g_pre, g_post, w_in, sinks, ln_v_g, ln_v_b, w_s, b_s,
           w_proj_a, w_proj_b, w_out):
    depth = w_in.shape[0]
    for l in range(depth):
        ada = _ada(c, w_ada[l], b_ada[l])
        x = _layer(x, ada, positions, g_pre[l], g_post[l], w_in[l], sinks[l], ln_v_g[l],
                   ln_v_b[l], w_s[l], b_s[l], w_proj_a[l], w_proj_b[l], w_out[l])
    return x
```

```python
import functools
import math

import jax
import jax.numpy as jnp
from jax import lax
from jax.experimental import pallas as pl
from jax.experimental.pallas import tpu as pltpu

HEAD_DIM = 64
N_Q_HEADS = 8
N_KV_HEADS = 2
GQ = N_Q_HEADS // N_KV_HEADS
WINDOW = 128
BLOCK = 128
ROPE_THETA = 10000.0
CHUNK = 128
N_GROUPS = 4
EPS = 1e-6
NEG = -1e30
LOG2E = 1.4426950408889634

LANES = 128
SEQ_TILE = 1024
SUB_TILE = 256
VMEM_LIMIT_BYTES = 56 * 1024 * 1024

BF16 = jnp.bfloat16
F32 = jnp.float32


def _ada_kernel(c_ref, w_ref, b_ref, o_ref):
    c = c_ref[...]
    c_act = c * jax.nn.sigmoid(c)
    o_ref[...] = jnp.dot(c_act, w_ref[...], preferred_element_type=F32) + b_ref[...]


def _sigmoid(x):
    return 0.5 * jnp.tanh(0.5 * x) + 0.5


def _silu(x):
    return x * _sigmoid(x)


def _gelu_exact(x):
    return 0.5 * x * (1.0 + lax.erf(x * (1.0 / math.sqrt(2.0))))


def _rope(t, cos, sin_signed, first_half):
    outs = []
    for c in range(t.shape[1] // LANES):
        tc = t[:, c * LANES:(c + 1) * LANES]
        rot = jnp.where(first_half,
                        pltpu.roll(tc, LANES - HEAD_DIM // 2, axis=1),
                        pltpu.roll(tc, HEAD_DIM // 2, axis=1))
        outs.append(tc * cos + rot * sin_signed)
    return outs


def _block_kernel(sinks_ref, x_ref, pos_ref, ada_ref, gpre_ref, gpost_ref, invf_ref,
                  win_ref, lng_ref, lnb_ref, ws_ref, bs_ref, wpa_ref, wpb_ref, wout_ref,
                  o_ref, k_scr, vt_scr, *, d_a, d_kv, d_b, d_model):
    T = x_ref.shape[1]
    R = SUB_TILE
    n_sub = T // R
    blk_per_sub = R // BLOCK
    s_idx = pl.program_id(1)

    c_q, c_k, c_v = 0, d_a, d_a + d_kv
    c_za = c_v + d_kv
    c_u = c_za + d_a
    c_vb = c_u + d_b
    c_zb = c_vb + d_b
    c_g = c_zb + d_b

    @pl.when(s_idx == 0)
    def _():
        k_scr[:, 0:BLOCK, :] = jnp.zeros((N_KV_HEADS, BLOCK, HEAD_DIM), BF16)
        vt_scr[:, :, 0:BLOCK] = jnp.zeros((N_KV_HEADS, HEAD_DIM, BLOCK), BF16)

    b_idx = pl.program_id(0)
    ada = ada_ref[pl.ds(b_idx, 1), :]
    shift = ada[:, 0:d_model]
    scale = ada[:, d_model:2 * d_model]
    gate = ada[:, 2 * d_model:3 * d_model]
    pre_gain = gpre_ref[...] * (1.0 + scale)
    invf = invf_ref[...]
    lane = lax.broadcasted_iota(jnp.int32, (1, LANES), 1)
    first_half = (lane % HEAD_DIM) < (HEAD_DIM // 2)

    kj = lax.broadcasted_iota(jnp.int32, (2 * BLOCK, BLOCK), 0)
    qi = lax.broadcasted_iota(jnp.int32, (2 * BLOCK, BLOCK), 1)
    rel = qi + BLOCK - kj
    in_win = (rel >= 0) & (rel < WINDOW)
    first_lo = jnp.where(s_idx == 0, BLOCK, 0)
    ti = lax.broadcasted_iota(jnp.int32, (CHUNK, CHUNK), 0)
    si = lax.broadcasted_iota(jnp.int32, (CHUNK, CHUNK), 1)
    causal = si <= ti
    group_w = d_b // N_GROUPS
    assert group_w == CHUNK and N_GROUPS % 2 == 0
    w_pairs, bias_pairs = [], []
    for pr in range(N_GROUPS // 2):
        w_pairs.append(jnp.concatenate(
            [jnp.where(causal, ws_ref[2 * pr + i], 0.0).astype(BF16) for i in range(2)], axis=1))
        bias_pairs.append(jnp.concatenate(
            [jnp.broadcast_to(bs_ref[:, 2 * pr + i:2 * pr + i + 1], (CHUNK, group_w))
             for i in range(2)], axis=1))
    q_scale = LOG2E / math.sqrt(HEAD_DIM)

    st = [dict() for _ in range(n_sub)]

    def proj(j, lo, hi):
        return jnp.dot(st[j]["hb"], win_ref[:, lo:hi], preferred_element_type=F32)

    def head(j):
        xc = x_ref[0, j * R:(j + 1) * R, :]
        ms = jnp.mean(xc * xc, axis=-1, keepdims=True)
        st[j]["hb"] = ((xc * lax.rsqrt(ms + EPS)) * pre_gain + shift).astype(BF16)
        st[j]["qkv"] = proj(j, c_q, c_za)

    def gmlp_in(j):
        v = _gelu_exact(proj(j, c_vb, c_zb))
        mu = jnp.mean(v, axis=-1, keepdims=True)
        vc = v - mu
        var = jnp.mean(vc * vc, axis=-1, keepdims=True)
        st[j]["vn"] = (vc * lax.rsqrt(var + EPS) * lng_ref[...] + lnb_ref[...]).astype(BF16)
        st[j]["u"] = _gelu_exact(proj(j, c_u, c_vb))

    def rope_and_scores(j):
        pos = pos_ref[pl.ds(b_idx, 1), j * R:(j + 1) * R].astype(F32)
        cos_rows, sin_rows = [], []
        for b in range(blk_per_sub):
            ang = invf * pos[:, b * BLOCK:(b + 1) * BLOCK]
            cs = jnp.cos(ang)
            sn = jnp.sin(ang)
            cos_rows.append(jnp.concatenate([cs, cs, cs, cs], axis=0).T)
            sin_rows.append(jnp.concatenate([-sn, sn, -sn, sn], axis=0).T)
        cos = jnp.concatenate(cos_rows, axis=0)
        sin_signed = jnp.concatenate(sin_rows, axis=0)
        qkv = st[j]["qkv"]
        q_tiles = _rope(qkv[:, c_q:c_k], cos, sin_signed, first_half)
        k_r = _rope(qkv[:, c_k:c_v], cos, sin_signed, first_half)[0]
        vt_new = qkv[:, c_v:c_za].T
        lo = BLOCK + j * R
        for g in range(N_KV_HEADS):
            k_scr[g, lo:lo + R, :] = k_r[:, g * HEAD_DIM:(g + 1) * HEAD_DIM].astype(BF16)
            vt_scr[g, :, lo:lo + R] = vt_new[g * HEAD_DIM:(g + 1) * HEAD_DIM, :].astype(BF16)
        q_heads = []
        for c in range(len(q_tiles)):
            qs = (q_tiles[c] * q_scale).astype(BF16)
            q_heads.append(qs[:, 0:HEAD_DIM])
            q_heads.append(qs[:, HEAD_DIM:2 * HEAD_DIM])
        ss = []
        for b in range(blk_per_sub):
            n = j * blk_per_sub + b
            band = slice(n * BLOCK, n * BLOCK + 2 * BLOCK)
            for g in range(N_KV_HEADS):
                kb = k_scr[g, band, :]
                q_stack = jnp.concatenate(
                    [q_heads[g * GQ + h][b * BLOCK:(b + 1) * BLOCK] for h in range(GQ)], axis=0)
                ss.append(lax.dot_general(kb, q_stack, (((1,), (1,)), ((), ())),
                                          preferred_element_type=F32))
        st[j]["scores"] = ss

    def pv_unit(j, idx):
        b, g = divmod(idx, N_KV_HEADS)
        n = j * blk_per_sub + b
        band = slice(n * BLOCK, n * BLOCK + 2 * BLOCK)
        s = st[j]["scores"][idx]
        valid = in_win & (kj >= first_lo) if n == 0 else in_win
        s = jnp.where(jnp.concatenate([valid] * GQ, axis=1), s, NEG)
        sink = jnp.concatenate(
            [jnp.full((1, BLOCK), sinks_ref[g * GQ + h] * LOG2E, F32) for h in range(GQ)], axis=1)
        m = jnp.maximum(jnp.max(s, axis=0, keepdims=True), sink)
        p = jnp.exp2(s - m)
        denom = jnp.sum(p, axis=0, keepdims=True) + jnp.exp2(sink - m)
        ot = jnp.dot(vt_scr[g, :, band], p.astype(BF16),
                     preferred_element_type=F32) * (1.0 / denom)
        for pair in range(GQ // 2):
            two = jnp.concatenate(
                [ot[:, (2 * pair) * BLOCK:(2 * pair + 1) * BLOCK],
                 ot[:, (2 * pair + 1) * BLOCK:(2 * pair + 2) * BLOCK]], axis=0)
            st[j]["attn"][b][g * (GQ // 2) + pair] = two.T

    def gating(j):
        zero = jnp.zeros((CHUNK, group_w), BF16)
        rows_out = []
        for n in range(R // CHUNK):
            rows = slice(n * CHUNK, (n + 1) * CHUNK)
            cols_out = []
            for pr in range(N_GROUPS // 2):
                cols = slice(2 * pr * group_w, (2 * pr + 2) * group_w)
                vn2 = st[j]["vn"][rows, cols]
                rhs = jnp.concatenate(
                    [jnp.concatenate([vn2[:, 0:group_w], zero], axis=1),
                     jnp.concatenate([zero, vn2[:, group_w:2 * group_w]], axis=1)], axis=0)
                sv = jnp.dot(w_pairs[pr], rhs, preferred_element_type=F32) + bias_pairs[pr]
                cols_out.append(st[j]["u"][rows, cols] * sv)
            rows_out.append(jnp.concatenate(cols_out, axis=1))
        st[j]["yb"] = jnp.concatenate(rows_out, axis=0)

    def branch_proj(j):
        attn = jnp.concatenate([jnp.concatenate(row, axis=1) for row in st[j]["attn"]], axis=0)
        y_a = attn * _silu(st[j]["za"])
        st[j]["pa"] = jnp.dot(y_a.astype(BF16), wpa_ref[...], preferred_element_type=F32)
        y_b = st[j]["yb"] * _silu(st[j]["zb"])
        st[j]["pb"] = jnp.dot(y_b.astype(BF16), wpb_ref[...], preferred_element_type=F32)

    def out_proj(j):
        merged = (_sigmoid(st[j]["ga"]) * st[j]["pa"]
                  + _sigmoid(st[j]["gb"]) * st[j]["pb"]).astype(BF16)
        y = jnp.dot(merged, wout_ref[...], preferred_element_type=F32)
        ms_y = jnp.mean(y * y, axis=-1, keepdims=True)
        rows = slice(j * R, (j + 1) * R)
        o_ref[0, rows, :] = x_ref[0, rows, :] + gate * (y * lax.rsqrt(ms_y + EPS) * gpost_ref[...])
        st[j].clear()

    n_units = blk_per_sub * N_KV_HEADS
    for j in range(n_sub + 1):
        if j < n_sub:
            head(j)
            gmlp_in(j)
        if j >= 1:
            out_proj(j - 1)
        if j < n_sub:
            rope_and_scores(j)
            st[j]["attn"] = [[None] * (N_Q_HEADS // 2) for _ in range(blk_per_sub)]
            st[j]["za"] = proj(j, c_za, c_u)
            st[j]["zb"] = proj(j, c_zb, c_g)
            st[j]["ga"] = proj(j, c_g, c_g + d_model)
            for idx in range(n_units):
                pv_unit(j, idx)
            gating(j)
            st[j]["gb"] = proj(j, c_g + d_model, c_g + 2 * d_model)
            branch_proj(j)

    k_scr[:, 0:BLOCK, :] = k_scr[:, T:T + BLOCK, :]
    vt_scr[:, :, 0:BLOCK] = vt_scr[:, :, T:T + BLOCK]


def _const_spec(shape):
    return pl.BlockSpec(shape, lambda b, s: (0,) * len(shape), pipeline_mode=pl.Buffered(1))


def _layer(x, ada, positions, g_pre, g_post, w_in, sinks, ln_v_g, ln_v_b, w_s, b_s,
           w_proj_a, w_proj_b, w_out):
    B, S, D = x.shape
    T = SEQ_TILE
    assert S % T == 0 and T % SUB_TILE == 0 and SUB_TILE % BLOCK == 0 and SUB_TILE % CHUNK == 0
    d_a = N_Q_HEADS * HEAD_DIM
    d_kv = N_KV_HEADS * HEAD_DIM
    d_b = w_proj_b.shape[0]
    d_in = w_in.shape[1]
    assert d_in == 2 * d_a + 2 * d_kv + 3 * d_b + 2 * D

    half = HEAD_DIM // 2
    inv_freq = ROPE_THETA ** (-jnp.arange(half, dtype=F32) / half)
    invf = jnp.broadcast_to(inv_freq[:, None], (half, LANES))

    kern = functools.partial(_block_kernel, d_a=d_a, d_kv=d_kv, d_b=d_b, d_model=D)
    return pl.pallas_call(
        kern,
        out_shape=jax.ShapeDtypeStruct((B, S, D), x.dtype),
        grid=(B, S // T),
        in_specs=[
            pl.BlockSpec(memory_space=pltpu.SMEM),
            pl.BlockSpec((1, T, D), lambda b, s: (b, s, 0)),
            pl.BlockSpec((B, T), lambda b, s: (0, s)),
            _const_spec((B, 3 * D)),
            _const_spec((1, D)),
            _const_spec((1, D)),
            _const_spec((half, LANES)),
            _const_spec((D, d_in)),
            _const_spec((1, d_b)),
            _const_spec((1, d_b)),
            _const_spec((N_GROUPS, CHUNK, CHUNK)),
            _const_spec((CHUNK, N_GROUPS)),
            _const_spec((d_a, D)),
            _const_spec((d_b, D)),
            _const_spec((D, D)),
        ],
        out_specs=pl.BlockSpec((1, T, D), lambda b, s: (b, s, 0)),
        scratch_shapes=[
            pltpu.VMEM((N_KV_HEADS, BLOCK + T, HEAD_DIM), BF16),
            pltpu.VMEM((N_KV_HEADS, HEAD_DIM, BLOCK + T), BF16),
        ],
        compiler_params=pltpu.CompilerParams(
            dimension_semantics=("arbitrary", "arbitrary"),
            vmem_limit_bytes=VMEM_LIMIT_BYTES),
        name="hybrid_block",
    )(sinks, x, positions, ada,
      g_pre[None, :], g_post[None, :], invf, w_in.astype(BF16),
      ln_v_g[None, :], ln_v_b[None, :], w_s, b_s.T,
      w_proj_a.astype(BF16), w_proj_b.astype(BF16), w_out.astype(BF16))


def _ada(c, w_ada, b_ada):
    B, D = c.shape
    n_out = w_ada.shape[1]
    tn = D
    return pl.pallas_call(
        _ada_kernel,
        out_shape=jax.ShapeDtypeStruct((B, n_out), F32),
        grid=(n_out // tn,),
        in_specs=[
            pl.BlockSpec((B, D), lambda j: (0, 0)),
            pl.BlockSpec((D, tn), lambda j: (0, j)),
            pl.BlockSpec((1, tn), lambda j: (0, j)),
        ],
        out_specs=pl.BlockSpec((B, tn), lambda j: (0, j)),
        compiler_params=pltpu.CompilerParams(dimension_semantics=("arbitrary",)),
        name="adaln_modulation",
    )(c, w_ada, b_ada[None, :])


def kernel(x, c, positions, w_ada, b_ada, g_pre, g_post, w_in, sinks, ln_v_g, ln_v_b, w_s, b_s,
           w_proj_a, w_proj_b, w_out):
    depth = w_in.shape[0]
    for l in range(depth):
        ada = _ada(c, w_ada[l], b_ada[l])
        x = _layer(x, ada, positions, g_pre[l], g_post[l], w_in[l], sinks[l], ln_v_g[l],
                   ln_v_b[l], w_s[l], b_s[l], w_proj_a[l], w_proj_b[l], w_out[l])
    return x
```

```python
import functools
import math

import jax
import jax.numpy as jnp
from jax import lax
from jax.experimental import pallas as pl
from jax.experimental.pallas import tpu as pltpu

HEAD_DIM = 64
N_Q_HEADS = 8
N_KV_HEADS = 2
GQ = N_Q_HEADS // N_KV_HEADS
WINDOW = 128
BLOCK = 128
ROPE_THETA = 10000.0
CHUNK = 128
N_GROUPS = 4
EPS = 1e-6
NEG = -1e30
LOG2E = 1.4426950408889634

LANES = 128
SEQ_TILE = 1024
SUB_TILE = 256
VMEM_LIMIT_BYTES = 56 * 1024 * 1024

BF16 = jnp.bfloat16
F32 = jnp.float32


def _ada_kernel(c_ref, w_ref, b_ref, o_ref):
    c = c_ref[...]
    c_act = c * jax.nn.sigmoid(c)
    o_ref[...] = jnp.dot(c_act, w_ref[...], preferred_element_type=F32) + b_ref[...]


def _sigmoid(x):
    return 0.5 * jnp.tanh(0.5 * x) + 0.5


def _silu(x):
    return x * _sigmoid(x)


def _gelu_exact(x):
    return 0.5 * x * (1.0 + lax.erf(x * (1.0 / math.sqrt(2.0))))


def _rope(t, cos, sin_signed, first_half):
    outs = []
    for c in range(t.shape[1] // LANES):
        tc = t[:, c * LANES:(c + 1) * LANES]
        rot = jnp.where(first_half,
                        pltpu.roll(tc, LANES - HEAD_DIM // 2, axis=1),
                        pltpu.roll(tc, HEAD_DIM // 2, axis=1))
        outs.append(tc * cos + rot * sin_signed)
    return outs


def _block_kernel(sinks_ref, x_ref, pos_ref, ada_ref, gpre_ref, gpost_ref, invf_ref,
                  win_ref, lng_ref, lnb_ref, ws_ref, bs_ref, wpa_ref, wpb_ref, wout_ref,
                  o_ref, k_scr, vt_scr, *, d_a, d_kv, d_b, d_model):
    T = x_ref.shape[1]
    R = SUB_TILE
    n_sub = T // R
    blk_per_sub = R // BLOCK
    s_idx = pl.program_id(1)

    c_q, c_k, c_v = 0, d_a, d_a + d_kv
    c_za = c_v + d_kv
    c_u = c_za + d_a
    c_vb = c_u + d_b
    c_zb = c_vb + d_b
    c_g = c_zb + d_b

    @pl.when(s_idx == 0)
    def _():
        k_scr[:, 0:BLOCK, :] = jnp.zeros((N_KV_HEADS, BLOCK, HEAD_DIM), BF16)
        vt_scr[:, :, 0:BLOCK] = jnp.zeros((N_KV_HEADS, HEAD_DIM, BLOCK), BF16)

    b_idx = pl.program_id(0)
    ada = ada_ref[pl.ds(b_idx, 1), :]
    shift = ada[:, 0:d_model]
    scale = ada[:, d_model:2 * d_model]
    gate = ada[:, 2 * d_model:3 * d_model]
    pre_gain = gpre_ref[...] * (1.0 + scale)
    invf = invf_ref[...]
    lane = lax.broadcasted_iota(jnp.int32, (1, LANES), 1)
    first_half = (lane % HEAD_DIM) < (HEAD_DIM // 2)

    kj = lax.broadcasted_iota(jnp.int32, (2 * BLOCK, BLOCK), 0)
    qi = lax.broadcasted_iota(jnp.int32, (2 * BLOCK, BLOCK), 1)
    rel = qi + BLOCK - kj
    in_win = (rel >= 0) & (rel < WINDOW)
    first_lo = jnp.where(s_idx == 0, BLOCK, 0)
    ti = lax.broadcasted_iota(jnp.int32, (CHUNK, CHUNK), 0)
    si = lax.broadcasted_iota(jnp.int32, (CHUNK, CHUNK), 1)
    causal = si <= ti
    group_w = d_b // N_GROUPS
    q_scale = LOG2E / math.sqrt(HEAD_DIM)

    st = [dict() for _ in range(n_sub)]

    def proj(j, lo, hi):
        return jnp.dot(st[j]["hb"], win_ref[:, lo:hi], preferred_element_type=F32)

    def head(j):
        xc = x_ref[0, j * R:(j + 1) * R, :]
        ms = jnp.mean(xc * xc, axis=-1, keepdims=True)
        st[j]["hb"] = ((xc * lax.rsqrt(ms + EPS)) * pre_gain + shift).astype(BF16)
        st[j]["qkv"] = proj(j, c_q, c_za)

    def gmlp_in(j):
        v = _gelu_exact(proj(j, c_vb, c_zb))
        mu = jnp.mean(v, axis=-1, keepdims=True)
        vc = v - mu
        var = jnp.mean(vc * vc, axis=-1, keepdims=True)
        st[j]["vn"] = (vc * lax.rsqrt(var + EPS) * lng_ref[...] + lnb_ref[...]).astype(BF16)
        st[j]["u"] = _gelu_exact(proj(j, c_u, c_vb))

    def rope_and_scores(j):
        pos = pos_ref[pl.ds(b_idx, 1), j * R:(j + 1) * R].astype(F32)
        cos_rows, sin_rows = [], []
        for b in range(blk_per_sub):
            ang = invf * pos[:, b * BLOCK:(b + 1) * BLOCK]
            cs = jnp.cos(ang)
            sn = jnp.sin(ang)
            cos_rows.append(jnp.concatenate([cs, cs, cs, cs], axis=0).T)
            sin_rows.append(jnp.concatenate([-sn, sn, -sn, sn], axis=0).T)
        cos = jnp.concatenate(cos_rows, axis=0)
        sin_signed = jnp.concatenate(sin_rows, axis=0)
        qkv = st[j]["qkv"]
        q_tiles = _rope(qkv[:, c_q:c_k], cos, sin_signed, first_half)
        k_r = _rope(qkv[:, c_k:c_v], cos, sin_signed, first_half)[0]
        vt_new = qkv[:, c_v:c_za].T
        lo = BLOCK + j * R
        for g in range(N_KV_HEADS):
            k_scr[g, lo:lo + R, :] = k_r[:, g * HEAD_DIM:(g + 1) * HEAD_DIM].astype(BF16)
            vt_scr[g, :, lo:lo + R] = vt_new[g * HEAD_DIM:(g + 1) * HEAD_DIM, :].astype(BF16)
        q_heads = []
        for c in range(len(q_tiles)):
            qs = (q_tiles[c] * q_scale).astype(BF16)
            q_heads.append(qs[:, 0:HEAD_DIM])
            q_heads.append(qs[:, HEAD_DIM:2 * HEAD_DIM])
        ss = []
        for b in range(blk_per_sub):
            n = j * blk_per_sub + b
            band = slice(n * BLOCK, n * BLOCK + 2 * BLOCK)
            for g in range(N_KV_HEADS):
                kb = k_scr[g, band, :]
                q_stack = jnp.concatenate(
                    [q_heads[g * GQ + h][b * BLOCK:(b + 1) * BLOCK] for h in range(GQ)], axis=0)
                ss.append(lax.dot_general(kb, q_stack, (((1,), (1,)), ((), ())),
                                          preferred_element_type=F32))
        st[j]["scores"] = ss

    def pv_unit(j, idx):
        b, g = divmod(idx, N_KV_HEADS)
        n = j * blk_per_sub + b
        band = slice(n * BLOCK, n * BLOCK + 2 * BLOCK)
        s = st[j]["scores"][idx]
        valid = in_win & (kj >= first_lo) if n == 0 else in_win
        s = jnp.where(jnp.concatenate([valid] * GQ, axis=1), s, NEG)
        sink = jnp.concatenate(
            [jnp.full((1, BLOCK), sinks_ref[g * GQ + h] * LOG2E, F32) for h in range(GQ)], axis=1)
        m = jnp.maximum(jnp.max(s, axis=0, keepdims=True), sink)
        p = jnp.exp2(s - m)
        denom = jnp.sum(p, axis=0, keepdims=True) + jnp.exp2(sink - m)
        ot = jnp.dot(vt_scr[g, :, band], p.astype(BF16),
                     preferred_element_type=F32) * (1.0 / denom)
        for pair in range(GQ // 2):
            two = jnp.concatenate(
                [ot[:, (2 * pair) * BLOCK:(2 * pair + 1) * BLOCK],
                 ot[:, (2 * pair + 1) * BLOCK:(2 * pair + 2) * BLOCK]], axis=0)
            st[j]["attn"][b][g * (GQ // 2) + pair] = two.T

    def gating(j):
        rows_out = []
        for n in range(R // CHUNK):
            rows = slice(n * CHUNK, (n + 1) * CHUNK)
            cols_out = []
            for g in range(N_GROUPS):
                w_g = jnp.where(causal, ws_ref[g], 0.0).astype(BF16)
                bias = jnp.broadcast_to(bs_ref[:, g:g + 1], (CHUNK, group_w))
                cols = slice(g * group_w, (g + 1) * group_w)
                sv = jnp.dot(w_g, st[j]["vn"][rows, cols], preferred_element_type=F32) + bias
                cols_out.append(st[j]["u"][rows, cols] * sv)
            rows_out.append(jnp.concatenate(cols_out, axis=1))
        st[j]["yb"] = jnp.concatenate(rows_out, axis=0)

    def branch_proj(j):
        attn = jnp.concatenate([jnp.concatenate(row, axis=1) for row in st[j]["attn"]], axis=0)
        y_a = attn * _silu(st[j]["za"])
        st[j]["pa"] = jnp.dot(y_a.astype(BF16), wpa_ref[...], preferred_element_type=F32)
        y_b = st[j]["yb"] * _silu(st[j]["zb"])
        st[j]["pb"] = jnp.dot(y_b.astype(BF16), wpb_ref[...], preferred_element_type=F32)

    def out_proj(j):
        merged = (_sigmoid(st[j]["ga"]) * st[j]["pa"]
                  + _sigmoid(st[j]["gb"]) * st[j]["pb"]).astype(BF16)
        y = jnp.dot(merged, wout_ref[...], preferred_element_type=F32)
        ms_y = jnp.mean(y * y, axis=-1, keepdims=True)
        rows = slice(j * R, (j + 1) * R)
        o_ref[0, rows, :] = x_ref[0, rows, :] + gate * (y * lax.rsqrt(ms_y + EPS) * gpost_ref[...])
        st[j].clear()

    n_units = blk_per_sub * N_KV_HEADS
    for j in range(n_sub + 1):
        if j < n_sub:
            head(j)
            gmlp_in(j)
        if j >= 1:
            out_proj(j - 1)
        if j < n_sub:
            rope_and_scores(j)
            st[j]["attn"] = [[None] * (N_Q_HEADS // 2) for _ in range(blk_per_sub)]
            st[j]["za"] = proj(j, c_za, c_u)
            st[j]["zb"] = proj(j, c_zb, c_g)
            st[j]["ga"] = proj(j, c_g, c_g + d_model)
            for idx in range(n_units):
                pv_unit(j, idx)
            gating(j)
            st[j]["gb"] = proj(j, c_g + d_model, c_g + 2 * d_model)
            branch_proj(j)

    k_scr[:, 0:BLOCK, :] = k_scr[:, T:T + BLOCK, :]
    vt_scr[:, :, 0:BLOCK] = vt_scr[:, :, T:T + BLOCK]


def _const_spec(shape):
    return pl.BlockSpec(shape, lambda b, s: (0,) * len(shape), pipeline_mode=pl.Buffered(1))


def _layer(x, ada, positions, g_pre, g_post, w_in, sinks, ln_v_g, ln_v_b, w_s, b_s,
           w_proj_a, w_proj_b, w_out):
    B, S, D = x.shape
    T = SEQ_TILE
    assert S % T == 0 and T % SUB_TILE == 0 and SUB_TILE % BLOCK == 0 and SUB_TILE % CHUNK == 0
    d_a = N_Q_HEADS * HEAD_DIM
    d_kv = N_KV_HEADS * HEAD_DIM
    d_b = w_proj_b.shape[0]
    d_in = w_in.shape[1]
    assert d_in == 2 * d_a + 2 * d_kv + 3 * d_b + 2 * D

    half = HEAD_DIM // 2
    inv_freq = ROPE_THETA ** (-jnp.arange(half, dtype=F32) / half)
    invf = jnp.broadcast_to(inv_freq[:, None], (half, LANES))

    kern = functools.partial(_block_kernel, d_a=d_a, d_kv=d_kv, d_b=d_b, d_model=D)
    return pl.pallas_call(
        kern,
        out_shape=jax.ShapeDtypeStruct((B, S, D), x.dtype),
        grid=(B, S // T),
        in_specs=[
            pl.BlockSpec(memory_space=pltpu.SMEM),
            pl.BlockSpec((1, T, D), lambda b, s: (b, s, 0)),
            pl.BlockSpec((B, T), lambda b, s: (0, s)),
            _const_spec((B, 3 * D)),
            _const_spec((1, D)),
            _const_spec((1, D)),
            _const_spec((half, LANES)),
            _const_spec((D, d_in)),
            _const_spec((1, d_b)),
            _const_spec((1, d_b)),
            _const_spec((N_GROUPS, CHUNK, CHUNK)),
            _const_spec((CHUNK, N_GROUPS)),
            _const_spec((d_a, D)),
            _const_spec((d_b, D)),
            _const_spec((D, D)),
        ],
        out_specs=pl.BlockSpec((1, T, D), lambda b, s: (b, s, 0)),
        scratch_shapes=[
            pltpu.VMEM((N_KV_HEADS, BLOCK + T, HEAD_DIM), BF16),
            pltpu.VMEM((N_KV_HEADS, HEAD_DIM, BLOCK + T), BF16),
        ],
        compiler_params=pltpu.CompilerParams(
            dimension_semantics=("arbitrary", "arbitrary"),
            vmem_limit_bytes=VMEM_LIMIT_BYTES),
        name="hybrid_block",
    )(sinks, x, positions, ada,
      g_pre[None, :], g_post[None, :], invf, w_in.astype(BF16),
      ln_v_g[None, :], ln_v_b[None, :], w_s, b_s.T,
      w_proj_a.astype(BF16), w_proj_b.astype(BF16), w_out.astype(BF16))


def _ada(c, w_ada, b_ada):
    B, D = c.shape
    n_out = w_ada.shape[1]
    tn = D
    return pl.pallas_call(
        _ada_kernel,
        out_shape=jax.ShapeDtypeStruct((B, n_out), F32),
        grid=(n_out // tn,),
        in_specs=[
            pl.BlockSpec((B, D), lambda j: (0, 0)),
            pl.BlockSpec((D, tn), lambda j: (0, j)),
            pl.BlockSpec((1, tn), lambda j: (0, j)),
        ],
        out_specs=pl.BlockSpec((B, tn), lambda j: (0, j)),
        compiler_params=pltpu.CompilerParams(dimension_semantics=("arbitrary",)),
        name="adaln_modulation",
    )(c, w_ada, b_ada[None, :])


def kernel(x, c, positions, w_ada, b_ada, g_pre, g_post, w_in, sinks, ln_v_g, ln_v_b, w_s, b_s,
           w_proj_a, w_proj_b, w_out):
    depth = w_in.shape[0]
    for l in range(depth):
        ada = _ada(c, w_ada[l], b_ada[l])
        x = _layer(x, ada, positions, g_pre[l], g_post[l], w_in[l], sinks[l], ln_v_g[l],
                   ln_v_b[l], w_s[l], b_s[l], w_proj_a[l], w_proj_b[l], w_out[l])
    return x
```

```python
import functools
import math

import jax
import jax.numpy as jnp
from jax import lax
from jax.experimental import pallas as pl
from jax.experimental.pallas import tpu as pltpu

HEAD_DIM = 64
N_Q_HEADS = 8
N_KV_HEADS = 2
GQ = N_Q_HEADS // N_KV_HEADS
WINDOW = 128
BLOCK = 128
ROPE_THETA = 10000.0
CHUNK = 128
N_GROUPS = 4
EPS = 1e-6
NEG = -1e30
LOG2E = 1.4426950408889634

LANES = 128
SEQ_TILE = 1024
SUB_TILE = 256
VMEM_LIMIT_BYTES = 56 * 1024 * 1024

BF16 = jnp.bfloat16
F32 = jnp.float32


def _ada_kernel(c_ref, w_ref, b_ref, o_ref):
    c = c_ref[...]
    c_act = c * jax.nn.sigmoid(c)
    o_ref[...] = jnp.dot(c_act, w_ref[...], preferred_element_type=F32) + b_ref[...]


def _sigmoid(x):
    return 0.5 * jnp.tanh(0.5 * x) + 0.5


def _silu(x):
    return x * _sigmoid(x)


def _gelu_exact(x):
    return 0.5 * x * (1.0 + lax.erf(x * (1.0 / math.sqrt(2.0))))


def _rope(t, cos, sin_signed, first_half):
    outs = []
    for c in range(t.shape[1] // LANES):
        tc = t[:, c * LANES:(c + 1) * LANES]
        rot = jnp.where(first_half,
                        pltpu.roll(tc, LANES - HEAD_DIM // 2, axis=1),
                        pltpu.roll(tc, HEAD_DIM // 2, axis=1))
        outs.append(tc * cos + rot * sin_signed)
    return outs


def _next_tile(b, s, n_b, n_s):
    last = s + 1 == n_s
    return jnp.where(last, jnp.minimum(b + 1, n_b - 1), b), jnp.where(last, 0, s + 1)


def _block_kernel(sinks_ref, x_ref, xn_ref, pos_ref, posn_ref, ada_ref, gpre_ref, gpost_ref,
                  invf_ref, win_ref, lng_ref, lnb_ref, ws_ref, bs_ref, wpa_ref, wpb_ref,
                  wout_ref, o_ref, k_scr, vt_scr, merged_scr, *, d_a, d_kv, d_b, d_model):
    T = x_ref.shape[1]
    R = SUB_TILE
    n_sub = T // R
    blk_per_sub = R // BLOCK
    n_units = blk_per_sub * N_KV_HEADS
    b_idx = pl.program_id(0)
    s_idx = pl.program_id(1)
    b_nxt, s_nxt = _next_tile(b_idx, s_idx, pl.num_programs(0), pl.num_programs(1))

    c_q, c_k, c_v = 0, d_a, d_a + d_kv
    c_za = c_v + d_kv
    c_u = c_za + d_a
    c_vb = c_u + d_b
    c_zb = c_vb + d_b
    c_g = c_zb + d_b

    invf = invf_ref[...]
    lane = lax.broadcasted_iota(jnp.int32, (1, LANES), 1)
    first_half = (lane % HEAD_DIM) < (HEAD_DIM // 2)

    kj = lax.broadcasted_iota(jnp.int32, (2 * BLOCK, BLOCK), 0)
    qi = lax.broadcasted_iota(jnp.int32, (2 * BLOCK, BLOCK), 1)
    rel = qi + BLOCK - kj
    in_win = (rel >= 0) & (rel < WINDOW)
    ti = lax.broadcasted_iota(jnp.int32, (CHUNK, CHUNK), 0)
    si = lax.broadcasted_iota(jnp.int32, (CHUNK, CHUNK), 1)
    causal = si <= ti
    group_w = d_b // N_GROUPS
    q_scale = LOG2E / math.sqrt(HEAD_DIM)

    def modulation(b):
        ada = ada_ref[pl.ds(b, 1), :]
        shift = ada[:, 0:d_model]
        scale = ada[:, d_model:2 * d_model]
        gate = ada[:, 2 * d_model:3 * d_model]
        return shift, gpre_ref[...] * (1.0 + scale), gate

    def proj(f, lo, hi):
        return jnp.dot(f["hb"], win_ref[:, lo:hi], preferred_element_type=F32)

    def head(f):
        xc = f["x"]()
        ms = jnp.mean(xc * xc, axis=-1, keepdims=True)
        f["hb"] = ((xc * lax.rsqrt(ms + EPS)) * f["pre_gain"] + f["shift"]).astype(BF16)
        f["qkv"] = proj(f, c_q, c_za)

    def gmlp_in(f):
        v = _gelu_exact(proj(f, c_vb, c_zb))
        mu = jnp.mean(v, axis=-1, keepdims=True)
        vc = v - mu
        var = jnp.mean(vc * vc, axis=-1, keepdims=True)
        f["vn"] = (vc * lax.rsqrt(var + EPS) * lng_ref[...] + lnb_ref[...]).astype(BF16)
        f["u"] = _gelu_exact(proj(f, c_u, c_vb))

    def rope_and_scores(f):
        pos = f["pos"]().astype(F32)
        cos_rows, sin_rows = [], []
        for b in range(blk_per_sub):
            ang = invf * pos[:, b * BLOCK:(b + 1) * BLOCK]
            cs = jnp.cos(ang)
            sn = jnp.sin(ang)
            cos_rows.append(jnp.concatenate([cs, cs, cs, cs], axis=0).T)
            sin_rows.append(jnp.concatenate([-sn, sn, -sn, sn], axis=0).T)
        cos = jnp.concatenate(cos_rows, axis=0)
        sin_signed = jnp.concatenate(sin_rows, axis=0)
        qkv = f["qkv"]
        q_tiles = _rope(qkv[:, c_q:c_k], cos, sin_signed, first_half)
        k_r = _rope(qkv[:, c_k:c_v], cos, sin_signed, first_half)[0]
        vt_new = qkv[:, c_v:c_za].T
        lo = BLOCK + f["j"] * R
        for g in range(N_KV_HEADS):
            k_scr[g, lo:lo + R, :] = k_r[:, g * HEAD_DIM:(g + 1) * HEAD_DIM].astype(BF16)
            vt_scr[g, :, lo:lo + R] = vt_new[g * HEAD_DIM:(g + 1) * HEAD_DIM, :].astype(BF16)
        q_heads = []
        for c in range(len(q_tiles)):
            qs = (q_tiles[c] * q_scale).astype(BF16)
            q_heads.append(qs[:, 0:HEAD_DIM])
            q_heads.append(qs[:, HEAD_DIM:2 * HEAD_DIM])
        ss = []
        for b in range(blk_per_sub):
            n = f["j"] * blk_per_sub + b
            band = slice(n * BLOCK, n * BLOCK + 2 * BLOCK)
            for g in range(N_KV_HEADS):
                kb = k_scr[g, band, :]
                q_stack = jnp.concatenate(
                    [q_heads[g * GQ + h][b * BLOCK:(b + 1) * BLOCK] for h in range(GQ)], axis=0)
                ss.append(lax.dot_general(kb, q_stack, (((1,), (1,)), ((), ())),
                                          preferred_element_type=F32))
        f["scores"] = ss
        f["attn"] = [[None] * (N_Q_HEADS // 2) for _ in range(blk_per_sub)]

    def pv_unit(f, idx):
        b, g = divmod(idx, N_KV_HEADS)
        n = f["j"] * blk_per_sub + b
        band = slice(n * BLOCK, n * BLOCK + 2 * BLOCK)
        s = f["scores"][idx]
        valid = in_win & (kj >= f["first_lo"]) if (b == 0 and f["first_lo"] is not None) else in_win
        s = jnp.where(jnp.concatenate([valid] * GQ, axis=1), s, NEG)
        sink = jnp.concatenate(
            [jnp.full((1, BLOCK), sinks_ref[g * GQ + h] * LOG2E, F32) for h in range(GQ)], axis=1)
        m = jnp.maximum(jnp.max(s, axis=0, keepdims=True), sink)
        p = jnp.exp2(s - m)
        denom = jnp.sum(p, axis=0, keepdims=True) + jnp.exp2(sink - m)
        ot = jnp.dot(vt_scr[g, :, band], p.astype(BF16),
                     preferred_element_type=F32) * (1.0 / denom)
        for pair in range(GQ // 2):
            two = jnp.concatenate(
                [ot[:, (2 * pair) * BLOCK:(2 * pair + 1) * BLOCK],
                 ot[:, (2 * pair + 1) * BLOCK:(2 * pair + 2) * BLOCK]], axis=0)
            f["attn"][b][g * (GQ // 2) + pair] = two.T

    def gating(f):
        rows_out = []
        for n in range(R // CHUNK):
            rows = slice(n * CHUNK, (n + 1) * CHUNK)
            cols_out = []
            for g in range(N_GROUPS):
                w_g = jnp.where(causal, ws_ref[g], 0.0).astype(BF16)
                bias = jnp.broadcast_to(bs_ref[:, g:g + 1], (CHUNK, group_w))
                cols = slice(g * group_w, (g + 1) * group_w)
                sv = jnp.dot(w_g, f["vn"][rows, cols], preferred_element_type=F32) + bias
                cols_out.append(f["u"][rows, cols] * sv)
            rows_out.append(jnp.concatenate(cols_out, axis=1))
        f["yb"] = jnp.concatenate(rows_out, axis=0)

    def branch_proj_and_merge(f):
        attn = jnp.concatenate([jnp.concatenate(row, axis=1) for row in f["attn"]], axis=0)
        y_a = attn * _silu(f["za"])
        p_a = jnp.dot(y_a.astype(BF16), wpa_ref[...], preferred_element_type=F32)
        y_b = f["yb"] * _silu(f["zb"])
        p_b = jnp.dot(y_b.astype(BF16), wpb_ref[...], preferred_element_type=F32)
        merged = (_sigmoid(f["ga"]) * p_a + _sigmoid(f["gb"]) * p_b).astype(BF16)
        f.clear()
        f["merged"] = merged

    def front_mid(f):
        rope_and_scores(f)
        f["za"] = proj(f, c_za, c_u)
        f["zb"] = proj(f, c_zb, c_g)
        f["ga"] = proj(f, c_g, c_g + d_model)
        for idx in range(n_units):
            pv_unit(f, idx)
        gating(f)
        f["gb"] = proj(f, c_g + d_model, c_g + 2 * d_model)
        branch_proj_and_merge(f)

    def out_proj(merged, j, gate):
        y = jnp.dot(merged, wout_ref[...], preferred_element_type=F32)
        ms_y = jnp.mean(y * y, axis=-1, keepdims=True)
        rows = slice(j * R, (j + 1) * R)
        o_ref[0, rows, :] = x_ref[0, rows, :] + gate * (y * lax.rsqrt(ms_y + EPS) * gpost_ref[...])

    shift_c, gain_c, gate_c = modulation(b_idx)
    shift_n, gain_n, _ = modulation(b_nxt)

    def front_config(j):
        if j < n_sub:
            return dict(j=j, x=lambda: x_ref[0, j * R:(j + 1) * R, :],
                        pos=lambda: pos_ref[pl.ds(b_idx, 1), j * R:(j + 1) * R],
                        shift=shift_c, pre_gain=gain_c,
                        first_lo=BLOCK if j == 0 else None)
        return dict(j=j, x=lambda: xn_ref[0],
                    pos=lambda: posn_ref[pl.ds(b_nxt, 1), :],
                    shift=shift_n, pre_gain=gain_n,
                    first_lo=jnp.where(s_nxt == 0, BLOCK, 0))

    @pl.when((b_idx == 0) & (s_idx == 0))
    def _():
        k_scr[:, 0:BLOCK, :] = jnp.zeros((N_KV_HEADS, BLOCK, HEAD_DIM), BF16)
        vt_scr[:, :, 0:BLOCK] = jnp.zeros((N_KV_HEADS, HEAD_DIM, BLOCK), BF16)
        f0 = front_config(0)
        head(f0)
        gmlp_in(f0)
        front_mid(f0)
        merged_scr[...] = f0["merged"]

    merged_prev = merged_scr[...]
    for j in range(1, n_sub + 1):
        f = front_config(j)
        head(f)
        gmlp_in(f)
        out_proj(merged_prev, j - 1, gate_c)
        front_mid(f)
        merged_prev = f["merged"]
    merged_scr[...] = merged_prev

    keep = BLOCK + R
    k_scr[:, 0:keep, :] = k_scr[:, T:T + keep, :]
    vt_scr[:, :, 0:keep] = vt_scr[:, :, T:T + keep]


def _const_spec(shape):
    return pl.BlockSpec(shape, lambda b, s: (0,) * len(shape), pipeline_mode=pl.Buffered(1))


def _layer(x, ada, positions, g_pre, g_post, w_in, sinks, ln_v_g, ln_v_b, w_s, b_s,
           w_proj_a, w_proj_b, w_out):
    B, S, D = x.shape
    T = SEQ_TILE
    R = SUB_TILE
    assert S % T == 0 and T % R == 0 and R % BLOCK == 0 and R % CHUNK == 0
    n_s = S // T
    d_a = N_Q_HEADS * HEAD_DIM
    d_kv = N_KV_HEADS * HEAD_DIM
    d_b = w_proj_b.shape[0]
    d_in = w_in.shape[1]
    assert d_in == 2 * d_a + 2 * d_kv + 3 * d_b + 2 * D

    half = HEAD_DIM // 2
    inv_freq = ROPE_THETA ** (-jnp.arange(half, dtype=F32) / half)
    invf = jnp.broadcast_to(inv_freq[:, None], (half, LANES))

    def next_x(b, s):
        nb, ns = _next_tile(b, s, B, n_s)
        return nb, ns * (T // R), 0

    def next_pos(b, s):
        _, ns = _next_tile(b, s, B, n_s)
        return 0, ns * (T // R)

    kern = functools.partial(_block_kernel, d_a=d_a, d_kv=d_kv, d_b=d_b, d_model=D)
    return pl.pallas_call(
        kern,
        out_shape=jax.ShapeDtypeStruct((B, S, D), x.dtype),
        grid=(B, n_s),
        in_specs=[
            pl.BlockSpec(memory_space=pltpu.SMEM),
            pl.BlockSpec((1, T, D), lambda b, s: (b, s, 0)),
            pl.BlockSpec((1, R, D), next_x),
            pl.BlockSpec((B, T), lambda b, s: (0, s)),
            pl.BlockSpec((B, R), next_pos),
            _const_spec((B, 3 * D)),
            _const_spec((1, D)),
            _const_spec((1, D)),
            _const_spec((half, LANES)),
            _const_spec((D, d_in)),
            _const_spec((1, d_b)),
            _const_spec((1, d_b)),
            _const_spec((N_GROUPS, CHUNK, CHUNK)),
            _const_spec((CHUNK, N_GROUPS)),
            _const_spec((d_a, D)),
            _const_spec((d_b, D)),
            _const_spec((D, D)),
        ],
        out_specs=pl.BlockSpec((1, T, D), lambda b, s: (b, s, 0)),
        scratch_shapes=[
            pltpu.VMEM((N_KV_HEADS, BLOCK + T + R, HEAD_DIM), BF16),
            pltpu.VMEM((N_KV_HEADS, HEAD_DIM, BLOCK + T + R), BF16),
            pltpu.VMEM((R, D), BF16),
        ],
        compiler_params=pltpu.CompilerParams(
            dimension_semantics=("arbitrary", "arbitrary"),
            vmem_limit_bytes=VMEM_LIMIT_BYTES),
        name="hybrid_block",
    )(sinks, x, x, positions, positions, ada,
      g_pre[None, :], g_post[None, :], invf, w_in.astype(BF16),
      ln_v_g[None, :], ln_v_b[None, :], w_s, b_s.T,
      w_proj_a.astype(BF16), w_proj_b.astype(BF16), w_out.astype(BF16))


def _ada(c, w_ada, b_ada):
    B, D = c.shape
    n_out = w_ada.shape[1]
    tn = D
    return pl.pallas_call(
        _ada_kernel,
        out_shape=jax.ShapeDtypeStruct((B, n_out), F32),
        grid=(n_out // tn,),
        in_specs=[
            pl.BlockSpec((B, D), lambda j: (0, 0)),
            pl.BlockSpec((D, tn), lambda j: (0, j)),
            pl.BlockSpec((1, tn), lambda j: (0, j)),
        ],
        out_specs=pl.BlockSpec((B, tn), lambda j: (0, j)),
        compiler_params=pltpu.CompilerParams(dimension_semantics=("arbitrary",)),
        name="adaln_modulation",
    )(c, w_ada, b_ada[None, :])


def kernel(x, c, positions, w_ada, b_ada, g_pre, g_post, w_in, sinks, ln_v_g, ln_v_b, w_s, b_s,
           w_proj_a, w_proj_b, w_out):
    depth = w_in.shape[0]
    for l in range(depth):
        ada = _ada(c, w_ada[l], b_ada[l])
        x = _layer(x, ada, positions, g_pre[l], g_post[l], w_in[l], sinks[l], ln_v_g[l],
                   ln_v_b[l], w_s[l], b_s[l], w_proj_a[l], w_proj_b[l], w_out[l])
    return x
```

```python
import functools
import math

import jax
import jax.numpy as jnp
from jax import lax
from jax.experimental import pallas as pl
from jax.experimental.pallas import tpu as pltpu

HEAD_DIM = 64
N_Q_HEADS = 8
N_KV_HEADS = 2
GQ = N_Q_HEADS // N_KV_HEADS
WINDOW = 128
BLOCK = 128
ROPE_THETA = 10000.0
CHUNK = 128
N_GROUPS = 4
EPS = 1e-6
NEG = -1e30
LOG2E = 1.4426950408889634

LANES = 128
SEQ_TILE = 1024
SUB_TILE = 256
VMEM_LIMIT_BYTES = 56 * 1024 * 1024

BF16 = jnp.bfloat16
F32 = jnp.float32


def _ada_kernel(c_ref, w_ref, b_ref, o_ref):
    c = c_ref[...]
    c_act = c * jax.nn.sigmoid(c)
    o_ref[...] = jnp.dot(c_act, w_ref[...], preferred_element_type=F32) + b_ref[...]


def _two_sigmoid_of_half(h):
    return jnp.tanh(h) + 1.0


def _silu_of_half(h):
    return h * _two_sigmoid_of_half(h)


def _gelu_exact(x):
    return 0.5 * x * (1.0 + lax.erf(x * (1.0 / math.sqrt(2.0))))


def _rope(t, cos, sin_signed, first_half):
    outs = []
    for c in range(t.shape[1] // LANES):
        tc = t[:, c * LANES:(c + 1) * LANES]
        rot = jnp.where(first_half,
                        pltpu.roll(tc, LANES - HEAD_DIM // 2, axis=1),
                        pltpu.roll(tc, HEAD_DIM // 2, axis=1))
        outs.append(tc * cos + rot * sin_signed)
    return outs


def _block_kernel(sinks_ref, x_ref, pos_ref, ada_ref, gpre_ref, gpost_ref, invf_ref,
                  win_ref, lng_ref, lnb_ref, ws_ref, bs_ref, wpa_ref, wpb_ref, wout_ref,
                  o_ref, k_scr, vt_scr, *, d_a, d_kv, d_b, d_model):
    T = x_ref.shape[1]
    R = SUB_TILE
    n_sub = T // R
    blk_per_sub = R // BLOCK
    s_idx = pl.program_id(1)

    c_q, c_k, c_v = 0, d_a, d_a + d_kv
    c_za = c_v + d_kv
    c_u = c_za + d_a
    c_vb = c_u + d_b
    c_zb = c_vb + d_b
    c_g = c_zb + d_b

    @pl.when(s_idx == 0)
    def _():
        k_scr[:, 0:BLOCK, :] = jnp.zeros((N_KV_HEADS, BLOCK, HEAD_DIM), BF16)
        vt_scr[:, :, 0:BLOCK] = jnp.zeros((N_KV_HEADS, HEAD_DIM, BLOCK), BF16)

    b_idx = pl.program_id(0)
    ada = ada_ref[pl.ds(b_idx, 1), :]
    shift = ada[:, 0:d_model]
    scale = ada[:, d_model:2 * d_model]
    gate = ada[:, 2 * d_model:3 * d_model]
    pre_gain = gpre_ref[...] * (1.0 + scale)
    post_gain = gate * gpost_ref[...]
    invf = invf_ref[...]
    lane = lax.broadcasted_iota(jnp.int32, (1, LANES), 1)
    first_half = (lane % HEAD_DIM) < (HEAD_DIM // 2)

    kj = lax.broadcasted_iota(jnp.int32, (2 * BLOCK, BLOCK), 0)
    qi = lax.broadcasted_iota(jnp.int32, (2 * BLOCK, BLOCK), 1)
    rel = qi + BLOCK - kj
    in_win = (rel >= 0) & (rel < WINDOW)
    first_lo = jnp.where(s_idx == 0, BLOCK, 0)
    bias_win = jnp.where(in_win, 0.0, NEG)
    bias_first = jnp.where(in_win & (kj >= first_lo), 0.0, NEG)
    ti = lax.broadcasted_iota(jnp.int32, (CHUNK, CHUNK), 0)
    si = lax.broadcasted_iota(jnp.int32, (CHUNK, CHUNK), 1)
    causal = si <= ti
    group_w = d_b // N_GROUPS
    q_scale = LOG2E / math.sqrt(HEAD_DIM)

    st = [dict() for _ in range(n_sub)]

    def proj(j, lo, hi):
        return jnp.dot(st[j]["hb"], win_ref[:, lo:hi], preferred_element_type=F32)

    def head(j):
        xc = x_ref[0, j * R:(j + 1) * R, :]
        ms = jnp.mean(xc * xc, axis=-1, keepdims=True)
        st[j]["hb"] = ((xc * lax.rsqrt(ms + EPS)) * pre_gain + shift).astype(BF16)
        st[j]["qkv"] = proj(j, c_q, c_za)

    def gmlp_in(j):
        v = _gelu_exact(proj(j, c_vb, c_zb))
        mu = jnp.mean(v, axis=-1, keepdims=True)
        vc = v - mu
        var = jnp.mean(vc * vc, axis=-1, keepdims=True)
        st[j]["vn"] = (vc * lax.rsqrt(var + EPS) * lng_ref[...] + lnb_ref[...]).astype(BF16)
        st[j]["u"] = _gelu_exact(proj(j, c_u, c_vb))

    def rope_and_scores(j):
        pos = pos_ref[pl.ds(b_idx, 1), j * R:(j + 1) * R].astype(F32)
        cos_rows, sin_rows = [], []
        for b in range(blk_per_sub):
            ang = invf * pos[:, b * BLOCK:(b + 1) * BLOCK]
            cs = jnp.cos(ang)
            sn = jnp.sin(ang)
            cos_rows.append(jnp.concatenate([cs, cs, cs, cs], axis=0).T)
            sin_rows.append(jnp.concatenate([-sn, sn, -sn, sn], axis=0).T)
        cos = jnp.concatenate(cos_rows, axis=0)
        sin_signed = jnp.concatenate(sin_rows, axis=0)
        qkv = st[j]["qkv"]
        q_tiles = _rope(qkv[:, c_q:c_k], cos, sin_signed, first_half)
        k_r = _rope(qkv[:, c_k:c_v], cos, sin_signed, first_half)[0]
        vt_new = qkv[:, c_v:c_za].T
        lo = BLOCK + j * R
        for g in range(N_KV_HEADS):
            k_scr[g, lo:lo + R, :] = k_r[:, g * HEAD_DIM:(g + 1) * HEAD_DIM].astype(BF16)
            vt_scr[g, :, lo:lo + R] = vt_new[g * HEAD_DIM:(g + 1) * HEAD_DIM, :].astype(BF16)
        q_heads = []
        for c in range(len(q_tiles)):
            qs = (q_tiles[c] * q_scale).astype(BF16)
            q_heads.append(qs[:, 0:HEAD_DIM])
            q_heads.append(qs[:, HEAD_DIM:2 * HEAD_DIM])
        ss = []
        for b in range(blk_per_sub):
            n = j * blk_per_sub + b
            band = slice(n * BLOCK, n * BLOCK + 2 * BLOCK)
            for g in range(N_KV_HEADS):
                kb = k_scr[g, band, :]
                q_stack = jnp.concatenate(
                    [q_heads[g * GQ + h][b * BLOCK:(b + 1) * BLOCK] for h in range(GQ)], axis=0)
                ss.append(lax.dot_general(kb, q_stack, (((1,), (1,)), ((), ())),
                                          preferred_element_type=F32))
        st[j]["scores"] = ss

    def pv_unit(j, idx):
        b, g = divmod(idx, N_KV_HEADS)
        n = j * blk_per_sub + b
        band = slice(n * BLOCK, n * BLOCK + 2 * BLOCK)
        s = st[j]["scores"][idx]
        bias = bias_first if n == 0 else bias_win
        s = s + jnp.concatenate([bias] * GQ, axis=1)
        sink = jnp.concatenate(
            [jnp.full((1, BLOCK), sinks_ref[g * GQ + h] * LOG2E, F32) for h in range(GQ)], axis=1)
        m = jnp.maximum(jnp.max(s, axis=0, keepdims=True), sink)
        p = jnp.exp2(s - m)
        denom = jnp.sum(p, axis=0, keepdims=True) + jnp.exp2(sink - m)
        ot = jnp.dot(vt_scr[g, :, band], p.astype(BF16),
                     preferred_element_type=F32) * (1.0 / denom)
        for pair in range(GQ // 2):
            two = jnp.concatenate(
                [ot[:, (2 * pair) * BLOCK:(2 * pair + 1) * BLOCK],
                 ot[:, (2 * pair + 1) * BLOCK:(2 * pair + 2) * BLOCK]], axis=0)
            st[j]["attn"][b][g * (GQ // 2) + pair] = two.T

    def gating(j):
        rows_out = []
        for n in range(R // CHUNK):
            rows = slice(n * CHUNK, (n + 1) * CHUNK)
            cols_out = []
            for g in range(N_GROUPS):
                w_g = jnp.where(causal, ws_ref[g], 0.0).astype(BF16)
                bias = jnp.broadcast_to(bs_ref[:, g:g + 1], (CHUNK, group_w))
                cols = slice(g * group_w, (g + 1) * group_w)
                sv = jnp.dot(w_g, st[j]["vn"][rows, cols], preferred_element_type=F32) + bias
                cols_out.append(st[j]["u"][rows, cols] * sv)
            rows_out.append(jnp.concatenate(cols_out, axis=1))
        st[j]["yb"] = jnp.concatenate(rows_out, axis=0)

    def branch_proj(j):
        attn = jnp.concatenate([jnp.concatenate(row, axis=1) for row in st[j]["attn"]], axis=0)
        y_a = attn * _silu_of_half(st[j]["za"])
        st[j]["pa"] = jnp.dot(y_a.astype(BF16), wpa_ref[...], preferred_element_type=F32)
        y_b = st[j]["yb"] * _silu_of_half(st[j]["zb"])
        st[j]["pb"] = jnp.dot(y_b.astype(BF16), wpb_ref[...], preferred_element_type=F32)

    def out_proj(j):
        merged2 = (_two_sigmoid_of_half(st[j]["ga"]) * st[j]["pa"]
                   + _two_sigmoid_of_half(st[j]["gb"]) * st[j]["pb"]).astype(BF16)
        y = jnp.dot(merged2, wout_ref[...], preferred_element_type=F32)
        ms_y = jnp.mean(y * y, axis=-1, keepdims=True)
        rows = slice(j * R, (j + 1) * R)
        o_ref[0, rows, :] = x_ref[0, rows, :] + (y * lax.rsqrt(ms_y + EPS)) * post_gain
        st[j].clear()

    n_units = blk_per_sub * N_KV_HEADS
    for j in range(n_sub + 1):
        if j < n_sub:
            head(j)
            gmlp_in(j)
        if j >= 1:
            out_proj(j - 1)
        if j < n_sub:
            rope_and_scores(j)
            st[j]["attn"] = [[None] * (N_Q_HEADS // 2) for _ in range(blk_per_sub)]
            st[j]["za"] = proj(j, c_za, c_u)
            st[j]["zb"] = proj(j, c_zb, c_g)
            st[j]["ga"] = proj(j, c_g, c_g + d_model)
            for idx in range(n_units):
                pv_unit(j, idx)
            gating(j)
            st[j]["gb"] = proj(j, c_g + d_model, c_g + 2 * d_model)
            branch_proj(j)

    k_scr[:, 0:BLOCK, :] = k_scr[:, T:T + BLOCK, :]
    vt_scr[:, :, 0:BLOCK] = vt_scr[:, :, T:T + BLOCK]


def _const_spec(shape):
    return pl.BlockSpec(shape, lambda b, s: (0,) * len(shape), pipeline_mode=pl.Buffered(1))


def _layer(x, ada, positions, g_pre, g_post, w_in, sinks, ln_v_g, ln_v_b, w_s, b_s,
           w_proj_a, w_proj_b, w_out):
    B, S, D = x.shape
    T = SEQ_TILE
    assert S % T == 0 and T % SUB_TILE == 0 and SUB_TILE % BLOCK == 0 and SUB_TILE % CHUNK == 0
    d_a = N_Q_HEADS * HEAD_DIM
    d_kv = N_KV_HEADS * HEAD_DIM
    d_b = w_proj_b.shape[0]
    d_in = w_in.shape[1]
    assert d_in == 2 * d_a + 2 * d_kv + 3 * d_b + 2 * D

    half = HEAD_DIM // 2
    inv_freq = ROPE_THETA ** (-jnp.arange(half, dtype=F32) / half)
    invf = jnp.broadcast_to(inv_freq[:, None], (half, LANES))

    c_za, c_u = d_a + 2 * d_kv, 2 * d_a + 2 * d_kv
    c_zb = c_u + 2 * d_b
    col = jnp.arange(d_in)
    halved = ((col >= c_za) & (col < c_u)) | (col >= c_zb)
    w_in_b = (w_in * jnp.where(halved, 0.5, 1.0).astype(F32)[None, :]).astype(BF16)
    w_out_b = (w_out * 0.5).astype(BF16)

    kern = functools.partial(_block_kernel, d_a=d_a, d_kv=d_kv, d_b=d_b, d_model=D)
    return pl.pallas_call(
        kern,
        out_shape=jax.ShapeDtypeStruct((B, S, D), x.dtype),
        grid=(B, S // T),
        in_specs=[
            pl.BlockSpec(memory_space=pltpu.SMEM),
            pl.BlockSpec((1, T, D), lambda b, s: (b, s, 0)),
            pl.BlockSpec((B, T), lambda b, s: (0, s)),
            _const_spec((B, 3 * D)),
            _const_spec((1, D)),
            _const_spec((1, D)),
            _const_spec((half, LANES)),
            _const_spec((D, d_in)),
            _const_spec((1, d_b)),
            _const_spec((1, d_b)),
            _const_spec((N_GROUPS, CHUNK, CHUNK)),
            _const_spec((CHUNK, N_GROUPS)),
            _const_spec((d_a, D)),
            _const_spec((d_b, D)),
            _const_spec((D, D)),
        ],
        out_specs=pl.BlockSpec((1, T, D), lambda b, s: (b, s, 0)),
        scratch_shapes=[
            pltpu.VMEM((N_KV_HEADS, BLOCK + T, HEAD_DIM), BF16),
            pltpu.VMEM((N_KV_HEADS, HEAD_DIM, BLOCK + T), BF16),
        ],
        compiler_params=pltpu.CompilerParams(
            dimension_semantics=("arbitrary", "arbitrary"),
            vmem_limit_bytes=VMEM_LIMIT_BYTES),
        name="hybrid_block",
    )(sinks, x, positions, ada,
      g_pre[None, :], g_post[None, :], invf, w_in_b,
      ln_v_g[None, :], ln_v_b[None, :], w_s, b_s.T,
      w_proj_a.astype(BF16), w_proj_b.astype(BF16), w_out_b)


def _ada(c, w_ada, b_ada):
    B, D = c.shape
    n_out = w_ada.shape[1]
    tn = D
    return pl.pallas_call(
        _ada_kernel,
        out_shape=jax.ShapeDtypeStruct((B, n_out), F32),
        grid=(n_out // tn,),
        in_specs=[
            pl.BlockSpec((B, D), lambda j: (0, 0)),
            pl.BlockSpec((D, tn), lambda j: (0, j)),
            pl.BlockSpec((1, tn), lambda j: (0, j)),
        ],
        out_specs=pl.BlockSpec((B, tn), lambda j: (0, j)),
        compiler_params=pltpu.CompilerParams(dimension_semantics=("arbitrary",)),
        name="adaln_modulation",
    )(c, w_ada, b_ada[None, :])


def kernel(x, c, positions, w_ada, b_ada, g_pre, g_post, w_in, sinks, ln_v_g, ln_v_b, w_s, b_s,
           w_proj_a, w_proj_b, w_out):
    depth = w_in.shape[0]
    for l in range(depth):
        ada = _ada(c, w_ada[l], b_ada[l])
        x = _layer(x, ada, positions, g_pre[l], g_post[l], w_in[l], sinks[l], ln_v_g[l],
                   ln_v_b[l], w_s[l], b_s[l], w_proj_a[l], w_proj_b[l], w_out[l])
    return x
```

```python
import functools
import math

import jax
import jax.numpy as jnp
from jax import lax
from jax.experimental import pallas as pl
from jax.experimental.pallas import tpu as pltpu

HEAD_DIM = 64
N_Q_HEADS = 8
N_KV_HEADS = 2
GQ = N_Q_HEADS // N_KV_HEADS
WINDOW = 128
BLOCK = 128
ROPE_THETA = 10000.0
CHUNK = 128
N_GROUPS = 4
EPS = 1e-6
NEG = -1e30
LOG2E = 1.4426950408889634

LANES = 128
SEQ_TILE = 1024
SUB_TILE = 256
VMEM_LIMIT_BYTES = 56 * 1024 * 1024

BF16 = jnp.bfloat16
F32 = jnp.float32


def _ada_kernel(c_ref, w_ref, b_ref, o_ref):
    c = c_ref[...]
    c_act = c * jax.nn.sigmoid(c)
    o_ref[...] = jnp.dot(c_act, w_ref[...], preferred_element_type=F32) + b_ref[...]


def _two_sigmoid_of_half(h):
    return jnp.tanh(h) + 1.0


def _silu_of_half(h):
    return h * _two_sigmoid_of_half(h)


def _gelu_times_sqrt2(t):
    return t * (1.0 + lax.erf(t))


def _rope(t, cos, sin_signed, first_half):
    outs = []
    for c in range(t.shape[1] // LANES):
        tc = t[:, c * LANES:(c + 1) * LANES]
        rot = jnp.where(first_half,
                        pltpu.roll(tc, LANES - HEAD_DIM // 2, axis=1),
                        pltpu.roll(tc, HEAD_DIM // 2, axis=1))
        outs.append(tc * cos + rot * sin_signed)
    return outs


def _block_kernel(sinks_ref, x_ref, pos_ref, ada_ref, gpre_ref, gpost_ref, invf_ref,
                  win_ref, lng_ref, lnb_ref, ws_ref, bs_ref, wpa_ref, wpb_ref, wout_ref,
                  o_ref, k_scr, vt_scr, *, d_a, d_kv, d_b, d_model):
    T = x_ref.shape[1]
    R = SUB_TILE
    n_sub = T // R
    blk_per_sub = R // BLOCK
    s_idx = pl.program_id(1)

    c_q, c_k, c_v = 0, d_a, d_a + d_kv
    c_za = c_v + d_kv
    c_u = c_za + d_a
    c_vb = c_u + d_b
    c_zb = c_vb + d_b
    c_g = c_zb + d_b

    @pl.when(s_idx == 0)
    def _():
        k_scr[:, 0:BLOCK, :] = jnp.zeros((N_KV_HEADS, BLOCK, HEAD_DIM), BF16)
        vt_scr[:, :, 0:BLOCK] = jnp.zeros((N_KV_HEADS, HEAD_DIM, BLOCK), BF16)

    b_idx = pl.program_id(0)
    ada = ada_ref[pl.ds(b_idx, 1), :]
    shift = ada[:, 0:d_model]
    scale = ada[:, d_model:2 * d_model]
    gate = ada[:, 2 * d_model:3 * d_model]
    pre_gain = gpre_ref[...] * (1.0 + scale)
    post_gain = gate * gpost_ref[...]
    invf = invf_ref[...]
    lane = lax.broadcasted_iota(jnp.int32, (1, LANES), 1)
    first_half = (lane % HEAD_DIM) < (HEAD_DIM // 2)

    kj = lax.broadcasted_iota(jnp.int32, (2 * BLOCK, BLOCK), 0)
    qi = lax.broadcasted_iota(jnp.int32, (2 * BLOCK, BLOCK), 1)
    rel = qi + BLOCK - kj
    in_win = (rel >= 0) & (rel < WINDOW)
    first_lo = jnp.where(s_idx == 0, BLOCK, 0)
    ti = lax.broadcasted_iota(jnp.int32, (CHUNK, CHUNK), 0)
    si = lax.broadcasted_iota(jnp.int32, (CHUNK, CHUNK), 1)
    causal = si <= ti
    group_w = d_b // N_GROUPS
    q_scale = LOG2E / math.sqrt(HEAD_DIM)

    st = [dict() for _ in range(n_sub)]

    def proj(j, lo, hi):
        return jnp.dot(st[j]["hb"], win_ref[:, lo:hi], preferred_element_type=F32)

    def head(j):
        xc = x_ref[0, j * R:(j + 1) * R, :]
        ms = jnp.mean(xc * xc, axis=-1, keepdims=True)
        st[j]["hb"] = ((xc * lax.rsqrt(ms + EPS)) * pre_gain + shift).astype(BF16)
        st[j]["qkv"] = proj(j, c_q, c_za)

    def gmlp_in(j):
        v = _gelu_times_sqrt2(proj(j, c_vb, c_zb))
        mu = jnp.mean(v, axis=-1, keepdims=True)
        vc = v - mu
        var = jnp.mean(vc * vc, axis=-1, keepdims=True)
        st[j]["vn"] = (vc * lax.rsqrt(var + 2.0 * EPS) * lng_ref[...] + lnb_ref[...]).astype(BF16)
        st[j]["u"] = _gelu_times_sqrt2(proj(j, c_u, c_vb))

    def rope_and_scores(j):
        pos = pos_ref[pl.ds(b_idx, 1), j * R:(j + 1) * R].astype(F32)
        cos_rows, sin_rows = [], []
        for b in range(blk_per_sub):
            ang = invf * pos[:, b * BLOCK:(b + 1) * BLOCK]
            cs = jnp.cos(ang)
            sn = jnp.sin(ang)
            cos_rows.append(jnp.concatenate([cs, cs, cs, cs], axis=0).T)
            sin_rows.append(jnp.concatenate([-sn, sn, -sn, sn], axis=0).T)
        cos = jnp.concatenate(cos_rows, axis=0)
        sin_signed = jnp.concatenate(sin_rows, axis=0)
        qkv = st[j]["qkv"]
        q_tiles = _rope(qkv[:, c_q:c_k], cos, sin_signed, first_half)
        k_r = _rope(qkv[:, c_k:c_v], cos, sin_signed, first_half)[0]
        vt_new = qkv[:, c_v:c_za].T
        lo = BLOCK + j * R
        for g in range(N_KV_HEADS):
            k_scr[g, lo:lo + R, :] = k_r[:, g * HEAD_DIM:(g + 1) * HEAD_DIM].astype(BF16)
            vt_scr[g, :, lo:lo + R] = vt_new[g * HEAD_DIM:(g + 1) * HEAD_DIM, :].astype(BF16)
        q_heads = []
        for c in range(len(q_tiles)):
            qs = (q_tiles[c] * q_scale).astype(BF16)
            q_heads.append(qs[:, 0:HEAD_DIM])
            q_heads.append(qs[:, HEAD_DIM:2 * HEAD_DIM])
        ss = []
        for b in range(blk_per_sub):
            n = j * blk_per_sub + b
            band = slice(n * BLOCK, n * BLOCK + 2 * BLOCK)
            for g in range(N_KV_HEADS):
                kb = k_scr[g, band, :]
                q_stack = jnp.concatenate(
                    [q_heads[g * GQ + h][b * BLOCK:(b + 1) * BLOCK] for h in range(GQ)], axis=0)
                ss.append(lax.dot_general(kb, q_stack, (((1,), (1,)), ((), ())),
                                          preferred_element_type=F32))
        st[j]["scores"] = ss

    def pv_unit(j, idx):
        b, g = divmod(idx, N_KV_HEADS)
        n = j * blk_per_sub + b
        band = slice(n * BLOCK, n * BLOCK + 2 * BLOCK)
        s = st[j]["scores"][idx]
        valid = in_win & (kj >= first_lo) if n == 0 else in_win
        s = jnp.where(jnp.concatenate([valid] * GQ, axis=1), s, NEG)
        sink = jnp.concatenate(
            [jnp.full((1, BLOCK), sinks_ref[g * GQ + h] * LOG2E, F32) for h in range(GQ)], axis=1)
        m = jnp.maximum(jnp.max(s, axis=0, keepdims=True), sink)
        p = jnp.exp2(s - m)
        denom = jnp.sum(p, axis=0, keepdims=True) + jnp.exp2(sink - m)
        ot = jnp.dot(vt_scr[g, :, band], p.astype(BF16),
                     preferred_element_type=F32) * (1.0 / denom)
        for pair in range(GQ // 2):
            two = jnp.concatenate(
                [ot[:, (2 * pair) * BLOCK:(2 * pair + 1) * BLOCK],
                 ot[:, (2 * pair + 1) * BLOCK:(2 * pair + 2) * BLOCK]], axis=0)
            st[j]["attn"][b][g * (GQ // 2) + pair] = two.T

    def gating(j):
        rows_out = []
        for n in range(R // CHUNK):
            rows = slice(n * CHUNK, (n + 1) * CHUNK)
            cols_out = []
            for g in range(N_GROUPS):
                w_g = jnp.where(causal, ws_ref[g], 0.0).astype(BF16)
                bias = jnp.broadcast_to(bs_ref[:, g:g + 1], (CHUNK, group_w))
                cols = slice(g * group_w, (g + 1) * group_w)
                sv = jnp.dot(w_g, st[j]["vn"][rows, cols], preferred_element_type=F32) + bias
                cols_out.append(st[j]["u"][rows, cols] * sv)
            rows_out.append(jnp.concatenate(cols_out, axis=1))
        st[j]["yb"] = jnp.concatenate(rows_out, axis=0)

    def branch_proj(j):
        attn = jnp.concatenate([jnp.concatenate(row, axis=1) for row in st[j]["attn"]], axis=0)
        y_a = attn * _silu_of_half(st[j]["za"])
        st[j]["pa"] = jnp.dot(y_a.astype(BF16), wpa_ref[...], preferred_element_type=F32)
        y_b = st[j]["yb"] * _silu_of_half(st[j]["zb"])
        st[j]["pb"] = jnp.dot(y_b.astype(BF16), wpb_ref[...], preferred_element_type=F32)

    def out_proj(j):
        merged2 = (_two_sigmoid_of_half(st[j]["ga"]) * st[j]["pa"]
                   + _two_sigmoid_of_half(st[j]["gb"]) * st[j]["pb"]).astype(BF16)
        y = jnp.dot(merged2, wout_ref[...], preferred_element_type=F32)
        ms_y = jnp.mean(y * y, axis=-1, keepdims=True)
        rows = slice(j * R, (j + 1) * R)
        o_ref[0, rows, :] = x_ref[0, rows, :] + (y * lax.rsqrt(ms_y + EPS)) * post_gain
        st[j].clear()

    n_units = blk_per_sub * N_KV_HEADS
    for j in range(n_sub + 1):
        if j < n_sub:
            head(j)
            gmlp_in(j)
        if j >= 1:
            out_proj(j - 1)
        if j < n_sub:
            rope_and_scores(j)
            st[j]["attn"] = [[None] * (N_Q_HEADS // 2) for _ in range(blk_per_sub)]
            st[j]["za"] = proj(j, c_za, c_u)
            st[j]["zb"] = proj(j, c_zb, c_g)
            st[j]["ga"] = proj(j, c_g, c_g + d_model)
            for idx in range(n_units):
                pv_unit(j, idx)
            gating(j)
            st[j]["gb"] = proj(j, c_g + d_model, c_g + 2 * d_model)
            branch_proj(j)

    k_scr[:, 0:BLOCK, :] = k_scr[:, T:T + BLOCK, :]
    vt_scr[:, :, 0:BLOCK] = vt_scr[:, :, T:T + BLOCK]


def _const_spec(shape):
    return pl.BlockSpec(shape, lambda b, s: (0,) * len(shape), pipeline_mode=pl.Buffered(1))


def _layer(x, ada, positions, g_pre, g_post, w_in, sinks, ln_v_g, ln_v_b, w_s, b_s,
           w_proj_a, w_proj_b, w_out):
    B, S, D = x.shape
    T = SEQ_TILE
    assert S % T == 0 and T % SUB_TILE == 0 and SUB_TILE % BLOCK == 0 and SUB_TILE % CHUNK == 0
    d_a = N_Q_HEADS * HEAD_DIM
    d_kv = N_KV_HEADS * HEAD_DIM
    d_b = w_proj_b.shape[0]
    d_in = w_in.shape[1]
    assert d_in == 2 * d_a + 2 * d_kv + 3 * d_b + 2 * D

    half = HEAD_DIM // 2
    inv_freq = ROPE_THETA ** (-jnp.arange(half, dtype=F32) / half)
    invf = jnp.broadcast_to(inv_freq[:, None], (half, LANES))

    c_za, c_u = d_a + 2 * d_kv, 2 * d_a + 2 * d_kv
    c_zb = c_u + 2 * d_b
    col = jnp.arange(d_in)
    halved = ((col >= c_za) & (col < c_u)) | (col >= c_zb)
    inv_sqrt2 = 1.0 / math.sqrt(2.0)
    col_scale = jnp.where(halved, 0.5, jnp.where((col >= c_u) & (col < c_zb), inv_sqrt2, 1.0))
    w_in_b = (w_in * col_scale.astype(F32)[None, :]).astype(BF16)
    w_out_b = (w_out * 0.5).astype(BF16)

    kern = functools.partial(_block_kernel, d_a=d_a, d_kv=d_kv, d_b=d_b, d_model=D)
    return pl.pallas_call(
        kern,
        out_shape=jax.ShapeDtypeStruct((B, S, D), x.dtype),
        grid=(B, S // T),
        in_specs=[
            pl.BlockSpec(memory_space=pltpu.SMEM),
            pl.BlockSpec((1, T, D), lambda b, s: (b, s, 0)),
            pl.BlockSpec((B, T), lambda b, s: (0, s)),
            _const_spec((B, 3 * D)),
            _const_spec((1, D)),
            _const_spec((1, D)),
            _const_spec((half, LANES)),
            _const_spec((D, d_in)),
            _const_spec((1, d_b)),
            _const_spec((1, d_b)),
            _const_spec((N_GROUPS, CHUNK, CHUNK)),
            _const_spec((CHUNK, N_GROUPS)),
            _const_spec((d_a, D)),
            _const_spec((d_b, D)),
            _const_spec((D, D)),
        ],
        out_specs=pl.BlockSpec((1, T, D), lambda b, s: (b, s, 0)),
        scratch_shapes=[
            pltpu.VMEM((N_KV_HEADS, BLOCK + T, HEAD_DIM), BF16),
            pltpu.VMEM((N_KV_HEADS, HEAD_DIM, BLOCK + T), BF16),
        ],
        compiler_params=pltpu.CompilerParams(
            dimension_semantics=("arbitrary", "arbitrary"),
            vmem_limit_bytes=VMEM_LIMIT_BYTES),
        name="hybrid_block",
    )(sinks, x, positions, ada,
      g_pre[None, :], g_post[None, :], invf, w_in_b,
      ln_v_g[None, :], ln_v_b[None, :], w_s, b_s.T,
      w_proj_a.astype(BF16), (w_proj_b * inv_sqrt2).astype(BF16), w_out_b)


def _ada(c, w_ada, b_ada):
    B, D = c.shape
    n_out = w_ada.shape[1]
    tn = D
    return pl.pallas_call(
        _ada_kernel,
        out_shape=jax.ShapeDtypeStruct((B, n_out), F32),
        grid=(n_out // tn,),
        in_specs=[
            pl.BlockSpec((B, D), lambda j: (0, 0)),
            pl.BlockSpec((D, tn), lambda j: (0, j)),
            pl.BlockSpec((1, tn), lambda j: (0, j)),
        ],
        out_specs=pl.BlockSpec((B, tn), lambda j: (0, j)),
        compiler_params=pltpu.CompilerParams(dimension_semantics=("arbitrary",)),
        name="adaln_modulation",
    )(c, w_ada, b_ada[None, :])


def kernel(x, c, positions, w_ada, b_ada, g_pre, g_post, w_in, sinks, ln_v_g, ln_v_b, w_s, b_s,
           w_proj_a, w_proj_b, w_out):
    depth = w_in.shape[0]
    for l in range(depth):
        ada = _ada(c, w_ada[l], b_ada[l])
        x = _layer(x, ada, positions, g_pre[l], g_post[l], w_in[l], sinks[l], ln_v_g[l],
                   ln_v_b[l], w_s[l], b_s[l], w_proj_a[l], w_proj_b[l], w_out[l])
    return x
```

```python
import functools
import math

import jax
import jax.numpy as jnp
from jax import lax
from jax.experimental import pallas as pl
from jax.experimental.pallas import tpu as pltpu

HEAD_DIM = 64
N_Q_HEADS = 8
N_KV_HEADS = 2
GQ = N_Q_HEADS // N_KV_HEADS
WINDOW = 128
BLOCK = 128
ROPE_THETA = 10000.0
CHUNK = 128
N_GROUPS = 4
EPS = 1e-6
NEG = -1e30
LOG2E = 1.4426950408889634

LANES = 128
SEQ_TILE = 1024
SUB_TILE = 256
VMEM_LIMIT_BYTES = 56 * 1024 * 1024

BF16 = jnp.bfloat16
F32 = jnp.float32


def _ada_kernel(c_ref, w_ref, b_ref, o_ref):
    c = c_ref[...]
    c_act = c * jax.nn.sigmoid(c)
    o_ref[...] = jnp.dot(c_act, w_ref[...], preferred_element_type=F32) + b_ref[...]


def _two_sigmoid_of_half(h):
    return jnp.tanh(h) + 1.0


def _silu_of_half(h):
    return h * _two_sigmoid_of_half(h)


def _gelu_times_sqrt2(t):
    return t * (1.0 + lax.erf(t))


def _rope(t, cos, sin_signed, first_half):
    outs = []
    for c in range(t.shape[1] // LANES):
        tc = t[:, c * LANES:(c + 1) * LANES]
        rot = jnp.where(first_half,
                        pltpu.roll(tc, LANES - HEAD_DIM // 2, axis=1),
                        pltpu.roll(tc, HEAD_DIM // 2, axis=1))
        outs.append(tc * cos + rot * sin_signed)
    return outs


def _block_kernel(sinks_ref, x_ref, pos_ref, ada_ref, gpre_ref, gpost_ref, invf_ref,
                  win_ref, lng_ref, lnb_ref, ws_ref, bs_ref, wpa_ref, wpb_ref, wout_ref,
                  o_ref, k_scr, vt_scr, *, d_a, d_kv, d_b, d_model):
    T = x_ref.shape[1]
    R = SUB_TILE
    n_sub = T // R
    blk_per_sub = R // BLOCK
    s_idx = pl.program_id(1)

    c_q, c_k, c_v = 0, d_a, d_a + d_kv
    c_za = c_v + d_kv
    c_u = c_za + d_a
    c_vb = c_u + d_b
    c_zb = c_vb + d_b
    c_g = c_zb + d_b

    @pl.when(s_idx == 0)
    def _():
        k_scr[:, 0:BLOCK, :] = jnp.zeros((N_KV_HEADS, BLOCK, HEAD_DIM), BF16)
        vt_scr[:, :, 0:BLOCK] = jnp.zeros((N_KV_HEADS, HEAD_DIM, BLOCK), BF16)

    b_idx = pl.program_id(0)
    ada = ada_ref[pl.ds(b_idx, 1), :]
    shift = ada[:, 0:d_model]
    scale = ada[:, d_model:2 * d_model]
    gate = ada[:, 2 * d_model:3 * d_model]
    pre_gain = gpre_ref[...] * (1.0 + scale)
    post_gain = gate * gpost_ref[...]
    invf = invf_ref[...]
    lane = lax.broadcasted_iota(jnp.int32, (1, LANES), 1)
    first_half = (lane % HEAD_DIM) < (HEAD_DIM // 2)

    kj = lax.broadcasted_iota(jnp.int32, (2 * BLOCK, BLOCK), 0)
    qi = lax.broadcasted_iota(jnp.int32, (2 * BLOCK, BLOCK), 1)
    rel = qi + BLOCK - kj
    in_win = (rel >= 0) & (rel < WINDOW)
    first_lo = jnp.where(s_idx == 0, BLOCK, 0)
    ti = lax.broadcasted_iota(jnp.int32, (CHUNK, CHUNK), 0)
    si = lax.broadcasted_iota(jnp.int32, (CHUNK, CHUNK), 1)
    causal = si <= ti
    group_w = d_b // N_GROUPS
    q_scale = LOG2E / math.sqrt(HEAD_DIM)

    st = [dict() for _ in range(n_sub)]

    def proj(j, lo, hi):
        return jnp.dot(st[j]["hb"], win_ref[:, lo:hi], preferred_element_type=F32)

    def norm_in(j):
        xc = x_ref[0, j * R:(j + 1) * R, :]
        ms = jnp.mean(xc * xc, axis=-1, keepdims=True)
        st[j]["hb"] = ((xc * lax.rsqrt(ms + EPS)) * pre_gain + shift).astype(BF16)

    def head(j):
        st[j]["qkv"] = proj(j, c_q, c_za)

    def gmlp_in(j):
        v = _gelu_times_sqrt2(proj(j, c_vb, c_zb))
        mu = jnp.mean(v, axis=-1, keepdims=True)
        vc = v - mu
        var = jnp.mean(vc * vc, axis=-1, keepdims=True)
        st[j]["vn"] = (vc * lax.rsqrt(var + 2.0 * EPS) * lng_ref[...] + lnb_ref[...]).astype(BF16)
        st[j]["u"] = _gelu_times_sqrt2(proj(j, c_u, c_vb))

    def rope_and_scores(j):
        pos = pos_ref[pl.ds(b_idx, 1), j * R:(j + 1) * R].astype(F32)
        cos_rows, sin_rows = [], []
        for b in range(blk_per_sub):
            ang = invf * pos[:, b * BLOCK:(b + 1) * BLOCK]
            cs = jnp.cos(ang)
            sn = jnp.sin(ang)
            cos_rows.append(jnp.concatenate([cs, cs, cs, cs], axis=0).T)
            sin_rows.append(jnp.concatenate([-sn, sn, -sn, sn], axis=0).T)
        cos = jnp.concatenate(cos_rows, axis=0)
        sin_signed = jnp.concatenate(sin_rows, axis=0)
        qkv = st[j]["qkv"]
        q_tiles = _rope(qkv[:, c_q:c_k], cos, sin_signed, first_half)
        k_r = _rope(qkv[:, c_k:c_v], cos, sin_signed, first_half)[0]
        vt_new = qkv[:, c_v:c_za].T
        lo = BLOCK + j * R
        for g in range(N_KV_HEADS):
            k_scr[g, lo:lo + R, :] = k_r[:, g * HEAD_DIM:(g + 1) * HEAD_DIM].astype(BF16)
            vt_scr[g, :, lo:lo + R] = vt_new[g * HEAD_DIM:(g + 1) * HEAD_DIM, :].astype(BF16)
        q_heads = []
        for c in range(len(q_tiles)):
            qs = (q_tiles[c] * q_scale).astype(BF16)
            q_heads.append(qs[:, 0:HEAD_DIM])
            q_heads.append(qs[:, HEAD_DIM:2 * HEAD_DIM])
        ss = []
        for b in range(blk_per_sub):
            n = j * blk_per_sub + b
            band = slice(n * BLOCK, n * BLOCK + 2 * BLOCK)
            for g in range(N_KV_HEADS):
                kb = k_scr[g, band, :]
                q_stack = jnp.concatenate(
                    [q_heads[g * GQ + h][b * BLOCK:(b + 1) * BLOCK] for h in range(GQ)], axis=0)
                ss.append(lax.dot_general(kb, q_stack, (((1,), (1,)), ((), ())),
                                          preferred_element_type=F32))
        st[j]["scores"] = ss

    def pv_unit(j, idx):
        b, g = divmod(idx, N_KV_HEADS)
        n = j * blk_per_sub + b
        band = slice(n * BLOCK, n * BLOCK + 2 * BLOCK)
        s = st[j]["scores"][idx]
        valid = in_win & (kj >= first_lo) if n == 0 else in_win
        s = jnp.where(jnp.concatenate([valid] * GQ, axis=1), s, NEG)
        sink = jnp.concatenate(
            [jnp.full((1, BLOCK), sinks_ref[g * GQ + h] * LOG2E, F32) for h in range(GQ)], axis=1)
        m = jnp.maximum(jnp.max(s, axis=0, keepdims=True), sink)
        p = jnp.exp2(s - m)
        denom = jnp.sum(p, axis=0, keepdims=True) + jnp.exp2(sink - m)
        ot = jnp.dot(vt_scr[g, :, band], p.astype(BF16),
                     preferred_element_type=F32) * (1.0 / denom)
        for pair in range(GQ // 2):
            two = jnp.concatenate(
                [ot[:, (2 * pair) * BLOCK:(2 * pair + 1) * BLOCK],
                 ot[:, (2 * pair + 1) * BLOCK:(2 * pair + 2) * BLOCK]], axis=0)
            st[j]["attn"][b][g * (GQ // 2) + pair] = two.T

    def gating(j):
        rows_out = []
        for n in range(R // CHUNK):
            rows = slice(n * CHUNK, (n + 1) * CHUNK)
            cols_out = []
            for g in range(N_GROUPS):
                w_g = jnp.where(causal, ws_ref[g], 0.0).astype(BF16)
                bias = jnp.broadcast_to(bs_ref[:, g:g + 1], (CHUNK, group_w))
                cols = slice(g * group_w, (g + 1) * group_w)
                sv = jnp.dot(w_g, st[j]["vn"][rows, cols], preferred_element_type=F32) + bias
                cols_out.append(st[j]["u"][rows, cols] * sv)
            rows_out.append(jnp.concatenate(cols_out, axis=1))
        st[j]["yb"] = jnp.concatenate(rows_out, axis=0)

    def branch_proj(j):
        attn = jnp.concatenate([jnp.concatenate(row, axis=1) for row in st[j]["attn"]], axis=0)
        y_a = attn * _silu_of_half(st[j]["za"])
        st[j]["pa"] = jnp.dot(y_a.astype(BF16), wpa_ref[...], preferred_element_type=F32)
        y_b = st[j]["yb"] * _silu_of_half(st[j]["zb"])
        st[j]["pb"] = jnp.dot(y_b.astype(BF16), wpb_ref[...], preferred_element_type=F32)

    def out_proj(j):
        merged2 = (_two_sigmoid_of_half(st[j]["ga"]) * st[j]["pa"]
                   + _two_sigmoid_of_half(st[j]["gb"]) * st[j]["pb"]).astype(BF16)
        y = jnp.dot(merged2, wout_ref[...], preferred_element_type=F32)
        ms_y = jnp.mean(y * y, axis=-1, keepdims=True)
        rows = slice(j * R, (j + 1) * R)
        o_ref[0, rows, :] = x_ref[0, rows, :] + (y * lax.rsqrt(ms_y + EPS)) * post_gain
        st[j].clear()

    n_units = blk_per_sub * N_KV_HEADS
    for j in range(n_sub):
        norm_in(j)
    for j in range(n_sub + 1):
        if j < n_sub:
            head(j)
            gmlp_in(j)
        if j >= 1:
            out_proj(j - 1)
        if j < n_sub:
            rope_and_scores(j)
            st[j]["attn"] = [[None] * (N_Q_HEADS // 2) for _ in range(blk_per_sub)]
            st[j]["za"] = proj(j, c_za, c_u)
            st[j]["zb"] = proj(j, c_zb, c_g)
            st[j]["ga"] = proj(j, c_g, c_g + d_model)
            for idx in range(n_units):
                pv_unit(j, idx)
            gating(j)
            st[j]["gb"] = proj(j, c_g + d_model, c_g + 2 * d_model)
            branch_proj(j)

    k_scr[:, 0:BLOCK, :] = k_scr[:, T:T + BLOCK, :]
    vt_scr[:, :, 0:BLOCK] = vt_scr[:, :, T:T + BLOCK]


def _const_spec(shape):
    return pl.BlockSpec(shape, lambda b, s: (0,) * len(shape), pipeline_mode=pl.Buffered(1))


def _layer(x, ada, positions, g_pre, g_post, w_in, sinks, ln_v_g, ln_v_b, w_s, b_s,
           w_proj_a, w_proj_b, w_out):
    B, S, D = x.shape
    T = SEQ_TILE
    assert S % T == 0 and T % SUB_TILE == 0 and SUB_TILE % BLOCK == 0 and SUB_TILE % CHUNK == 0
    d_a = N_Q_HEADS * HEAD_DIM
    d_kv = N_KV_HEADS * HEAD_DIM
    d_b = w_proj_b.shape[0]
    d_in = w_in.shape[1]
    assert d_in == 2 * d_a + 2 * d_kv + 3 * d_b + 2 * D

    half = HEAD_DIM // 2
    inv_freq = ROPE_THETA ** (-jnp.arange(half, dtype=F32) / half)
    invf = jnp.broadcast_to(inv_freq[:, None], (half, LANES))

    c_za, c_u = d_a + 2 * d_kv, 2 * d_a + 2 * d_kv
    c_zb = c_u + 2 * d_b
    col = jnp.arange(d_in)
    halved = ((col >= c_za) & (col < c_u)) | (col >= c_zb)
    inv_sqrt2 = 1.0 / math.sqrt(2.0)
    col_scale = jnp.where(halved, 0.5, jnp.where((col >= c_u) & (col < c_zb), inv_sqrt2, 1.0))
    w_in_b = (w_in * col_scale.astype(F32)[None, :]).astype(BF16)
    w_out_b = (w_out * 0.5).astype(BF16)

    kern = functools.partial(_block_kernel, d_a=d_a, d_kv=d_kv, d_b=d_b, d_model=D)
    return pl.pallas_call(
        kern,
        out_shape=jax.ShapeDtypeStruct((B, S, D), x.dtype),
        grid=(B, S // T),
        in_specs=[
            pl.BlockSpec(memory_space=pltpu.SMEM),
            pl.BlockSpec((1, T, D), lambda b, s: (b, s, 0)),
            pl.BlockSpec((B, T), lambda b, s: (0, s)),
            _const_spec((B, 3 * D)),
            _const_spec((1, D)),
            _const_spec((1, D)),
            _const_spec((half, LANES)),
            _const_spec((D, d_in)),
            _const_spec((1, d_b)),
            _const_spec((1, d_b)),
            _const_spec((N_GROUPS, CHUNK, CHUNK)),
            _const_spec((CHUNK, N_GROUPS)),
            _const_spec((d_a, D)),
            _const_spec((d_b, D)),
            _const_spec((D, D)),
        ],
        out_specs=pl.BlockSpec((1, T, D), lambda b, s: (b, s, 0)),
        scratch_shapes=[
            pltpu.VMEM((N_KV_HEADS, BLOCK + T, HEAD_DIM), BF16),
            pltpu.VMEM((N_KV_HEADS, HEAD_DIM, BLOCK + T), BF16),
        ],
        compiler_params=pltpu.CompilerParams(
            dimension_semantics=("arbitrary", "arbitrary"),
            vmem_limit_bytes=VMEM_LIMIT_BYTES),
        name="hybrid_block",
    )(sinks, x, positions, ada,
      g_pre[None, :], g_post[None, :], invf, w_in_b,
      ln_v_g[None, :], ln_v_b[None, :], w_s, b_s.T,
      w_proj_a.astype(BF16), (w_proj_b * inv_sqrt2).astype(BF16), w_out_b)


def _ada(c, w_ada, b_ada):
    B, D = c.shape
    n_out = w_ada.shape[1]
    tn = D
    return pl.pallas_call(
        _ada_kernel,
        out_shape=jax.ShapeDtypeStruct((B, n_out), F32),
        grid=(n_out // tn,),
        in_specs=[
            pl.BlockSpec((B, D), lambda j: (0, 0)),
            pl.BlockSpec((D, tn), lambda j: (0, j)),
            pl.BlockSpec((1, tn), lambda j: (0, j)),
        ],
        out_specs=pl.BlockSpec((B, tn), lambda j: (0, j)),
        compiler_params=pltpu.CompilerParams(dimension_semantics=("arbitrary",)),
        name="adaln_modulation",
    )(c, w_ada, b_ada[None, :])


def kernel(x, c, positions, w_ada, b_ada, g_pre, g_post, w_in, sinks, ln_v_g, ln_v_b, w_s, b_s,
           w_proj_a, w_proj_b, w_out):
    depth = w_in.shape[0]
    for l in range(depth):
        ada = _ada(c, w_ada[l], b_ada[l])
        x = _layer(x, ada, positions, g_pre[l], g_post[l], w_in[l], sinks[l], ln_v_g[l],
                   ln_v_b[l], w_s[l], b_s[l], w_proj_a[l], w_proj_b[l], w_out[l])
    return x
```

```python
import functools
import math

import jax
import jax.numpy as jnp
from jax import lax
from jax.experimental import pallas as pl
from jax.experimental.pallas import tpu as pltpu

HEAD_DIM = 64
N_Q_HEADS = 8
N_KV_HEADS = 2
GQ = N_Q_HEADS // N_KV_HEADS
WINDOW = 128
BLOCK = 128
ROPE_THETA = 10000.0
CHUNK = 128
N_GROUPS = 4
EPS = 1e-6
NEG = -1e30
LOG2E = 1.4426950408889634

LANES = 128
SEQ_TILE = 1024
SUB_TILE = 256
VMEM_LIMIT_BYTES = 44 * 1024 * 1024

BF16 = jnp.bfloat16
F32 = jnp.float32


def _ada_kernel(c_ref, w_ref, b_ref, o_ref):
    c = c_ref[...]
    c_act = c * jax.nn.sigmoid(c)
    o_ref[...] = jnp.dot(c_act, w_ref[...], preferred_element_type=F32) + b_ref[...]


def _two_sigmoid_of_half(h):
    return jnp.tanh(h) + 1.0


def _silu_of_half(h):
    return h * _two_sigmoid_of_half(h)


def _gelu_times_sqrt2(t):
    return t * (1.0 + lax.erf(t))


def _rope(t, cos, sin_signed, first_half):
    outs = []
    for c in range(t.shape[1] // LANES):
        tc = t[:, c * LANES:(c + 1) * LANES]
        rot = jnp.where(first_half,
                        pltpu.roll(tc, LANES - HEAD_DIM // 2, axis=1),
                        pltpu.roll(tc, HEAD_DIM // 2, axis=1))
        outs.append(tc * cos + rot * sin_signed)
    return outs


def _block_kernel(sinks_ref, x_ref, pos_ref, ada_ref, gpre_ref, gpost_ref, invf_ref,
                  win_ref, lng_ref, lnb_ref, ws_ref, bs_ref, wpa_ref, wpb_ref, wout_ref,
                  o_ref, k_scr, vt_scr, *, d_a, d_kv, d_b, d_model):
    T = x_ref.shape[1]
    R = SUB_TILE
    n_sub = T // R
    blk_per_sub = R // BLOCK
    s_idx = pl.program_id(1)

    c_q, c_k, c_v = 0, d_a, d_a + d_kv
    c_za = c_v + d_kv
    c_u = c_za + d_a
    c_vb = c_u + d_b
    c_zb = c_vb + d_b
    c_g = c_zb + d_b

    @pl.when(s_idx == 0)
    def _():
        k_scr[:, 0:BLOCK, :] = jnp.zeros((N_KV_HEADS, BLOCK, HEAD_DIM), BF16)
        vt_scr[:, :, 0:BLOCK] = jnp.zeros((N_KV_HEADS, HEAD_DIM, BLOCK), BF16)

    b_idx = pl.program_id(0)
    ada = ada_ref[pl.ds(b_idx, 1), :]
    shift = ada[:, 0:d_model]
    scale = ada[:, d_model:2 * d_model]
    gate = ada[:, 2 * d_model:3 * d_model]
    pre_gain = gpre_ref[...] * (1.0 + scale)
    post_gain = gate * gpost_ref[...]
    invf = invf_ref[...]
    lane = lax.broadcasted_iota(jnp.int32, (1, LANES), 1)
    first_half = (lane % HEAD_DIM) < (HEAD_DIM // 2)

    kj = lax.broadcasted_iota(jnp.int32, (2 * BLOCK, BLOCK), 0)
    qi = lax.broadcasted_iota(jnp.int32, (2 * BLOCK, BLOCK), 1)
    rel = qi + BLOCK - kj
    in_win = (rel >= 0) & (rel < WINDOW)
    first_lo = jnp.where(s_idx == 0, BLOCK, 0)
    ti = lax.broadcasted_iota(jnp.int32, (CHUNK, CHUNK), 0)
    si = lax.broadcasted_iota(jnp.int32, (CHUNK, CHUNK), 1)
    causal = si <= ti
    group_w = d_b // N_GROUPS
    q_scale = LOG2E / math.sqrt(HEAD_DIM)

    st = [dict() for _ in range(n_sub)]

    def proj(j, lo, hi):
        return jnp.dot(st[j]["hb"], win_ref[:, lo:hi], preferred_element_type=F32)

    def head(j):
        xc = x_ref[0, j * R:(j + 1) * R, :]
        ms = jnp.mean(xc * xc, axis=-1, keepdims=True)
        st[j]["hb"] = ((xc * lax.rsqrt(ms + EPS)) * pre_gain + shift).astype(BF16)
        st[j]["qkv"] = proj(j, c_q, c_za)

    def gmlp_in(j):
        v = _gelu_times_sqrt2(proj(j, c_vb, c_zb))
        mu = jnp.mean(v, axis=-1, keepdims=True)
        vc = v - mu
        var = jnp.mean(vc * vc, axis=-1, keepdims=True)
        st[j]["vn"] = (vc * lax.rsqrt(var + 2.0 * EPS) * lng_ref[...] + lnb_ref[...]).astype(BF16)
        st[j]["u"] = _gelu_times_sqrt2(proj(j, c_u, c_vb))

    def rope_and_scores(j):
        pos = pos_ref[pl.ds(b_idx, 1), j * R:(j + 1) * R].astype(F32)
        cos_rows, sin_rows = [], []
        for b in range(blk_per_sub):
            ang = invf * pos[:, b * BLOCK:(b + 1) * BLOCK]
            cs = jnp.cos(ang)
            sn = jnp.sin(ang)
            cos_rows.append(jnp.concatenate([cs, cs, cs, cs], axis=0).T)
            sin_rows.append(jnp.concatenate([-sn, sn, -sn, sn], axis=0).T)
        cos = jnp.concatenate(cos_rows, axis=0)
        sin_signed = jnp.concatenate(sin_rows, axis=0)
        qkv = st[j]["qkv"]
        q_tiles = _rope(qkv[:, c_q:c_k], cos, sin_signed, first_half)
        k_r = _rope(qkv[:, c_k:c_v], cos, sin_signed, first_half)[0]
        vt_new = qkv[:, c_v:c_za].T
        lo = BLOCK + j * R
        for g in range(N_KV_HEADS):
            k_scr[g, lo:lo + R, :] = k_r[:, g * HEAD_DIM:(g + 1) * HEAD_DIM].astype(BF16)
            vt_scr[g, :, lo:lo + R] = vt_new[g * HEAD_DIM:(g + 1) * HEAD_DIM, :].astype(BF16)
        q_heads = []
        for c in range(len(q_tiles)):
            qs = (q_tiles[c] * q_scale).astype(BF16)
            q_heads.append(qs[:, 0:HEAD_DIM])
            q_heads.append(qs[:, HEAD_DIM:2 * HEAD_DIM])
        ss = []
        for b in range(blk_per_sub):
            n = j * blk_per_sub + b
            band = slice(n * BLOCK, n * BLOCK + 2 * BLOCK)
            for g in range(N_KV_HEADS):
                kb = k_scr[g, band, :]
                q_stack = jnp.concatenate(
                    [q_heads[g * GQ + h][b * BLOCK:(b + 1) * BLOCK] for h in range(GQ)], axis=0)
                ss.append(lax.dot_general(kb, q_stack, (((1,), (1,)), ((), ())),
                                          preferred_element_type=F32))
        st[j]["scores"] = ss

    def pv_unit(j, idx):
        b, g = divmod(idx, N_KV_HEADS)
        n = j * blk_per_sub + b
        band = slice(n * BLOCK, n * BLOCK + 2 * BLOCK)
        s = st[j]["scores"][idx]
        valid = in_win & (kj >= first_lo) if n == 0 else in_win
        s = jnp.where(jnp.concatenate([valid] * GQ, axis=1), s, NEG)
        sink = jnp.concatenate(
            [jnp.full((1, BLOCK), sinks_ref[g * GQ + h] * LOG2E, F32) for h in range(GQ)], axis=1)
        m = jnp.maximum(jnp.max(s, axis=0, keepdims=True), sink)
        p = jnp.exp2(s - m)
        denom = jnp.sum(p, axis=0, keepdims=True) + jnp.exp2(sink - m)
        ot = jnp.dot(vt_scr[g, :, band], p.astype(BF16),
                     preferred_element_type=F32) * (1.0 / denom)
        for pair in range(GQ // 2):
            two = jnp.concatenate(
                [ot[:, (2 * pair) * BLOCK:(2 * pair + 1) * BLOCK],
                 ot[:, (2 * pair + 1) * BLOCK:(2 * pair + 2) * BLOCK]], axis=0)
            st[j]["attn"][b][g * (GQ // 2) + pair] = two.T

    def gating(j):
        rows_out = []
        for n in range(R // CHUNK):
            rows = slice(n * CHUNK, (n + 1) * CHUNK)
            cols_out = []
            for g in range(N_GROUPS):
                w_g = jnp.where(causal, ws_ref[g], 0.0).astype(BF16)
                bias = jnp.broadcast_to(bs_ref[:, g:g + 1], (CHUNK, group_w))
                cols = slice(g * group_w, (g + 1) * group_w)
                sv = jnp.dot(w_g, st[j]["vn"][rows, cols], preferred_element_type=F32) + bias
                cols_out.append(st[j]["u"][rows, cols] * sv)
            rows_out.append(jnp.concatenate(cols_out, axis=1))
        st[j]["yb"] = jnp.concatenate(rows_out, axis=0)

    def branch_proj(j):
        attn = jnp.concatenate([jnp.concatenate(row, axis=1) for row in st[j]["attn"]], axis=0)
        y_a = attn * _silu_of_half(st[j]["za"])
        st[j]["pa"] = jnp.dot(y_a.astype(BF16), wpa_ref[...], preferred_element_type=F32)
        y_b = st[j]["yb"] * _silu_of_half(st[j]["zb"])
        st[j]["pb"] = jnp.dot(y_b.astype(BF16), wpb_ref[...], preferred_element_type=F32)

    def out_proj(j):
        merged2 = (_two_sigmoid_of_half(st[j]["ga"]) * st[j]["pa"]
                   + _two_sigmoid_of_half(st[j]["gb"]) * st[j]["pb"]).astype(BF16)
        y = jnp.dot(merged2, wout_ref[...], preferred_element_type=F32)
        ms_y = jnp.mean(y * y, axis=-1, keepdims=True)
        rows = slice(j * R, (j + 1) * R)
        o_ref[0, rows, :] = x_ref[0, rows, :] + (y * lax.rsqrt(ms_y + EPS)) * post_gain
        st[j].clear()

    n_units = blk_per_sub * N_KV_HEADS
    for j in range(n_sub + 1):
        if j < n_sub:
            head(j)
            gmlp_in(j)
        if j >= 1:
            out_proj(j - 1)
        if j < n_sub:
            rope_and_scores(j)
            st[j]["attn"] = [[None] * (N_Q_HEADS // 2) for _ in range(blk_per_sub)]
            st[j]["za"] = proj(j, c_za, c_u)
            st[j]["zb"] = proj(j, c_zb, c_g)
            st[j]["ga"] = proj(j, c_g, c_g + d_model)
            for idx in range(n_units):
                pv_unit(j, idx)
            gating(j)
            st[j]["gb"] = proj(j, c_g + d_model, c_g + 2 * d_model)
            branch_proj(j)

    k_scr[:, 0:BLOCK, :] = k_scr[:, T:T + BLOCK, :]
    vt_scr[:, :, 0:BLOCK] = vt_scr[:, :, T:T + BLOCK]


def _const_spec(shape):
    return pl.BlockSpec(shape, lambda b, s: (0,) * len(shape), pipeline_mode=pl.Buffered(1))


def _layer(x, ada, positions, g_pre, g_post, w_in, sinks, ln_v_g, ln_v_b, w_s, b_s,
           w_proj_a, w_proj_b, w_out):
    B, S, D = x.shape
    T = SEQ_TILE
    assert S % T == 0 and T % SUB_TILE == 0 and SUB_TILE % BLOCK == 0 and SUB_TILE % CHUNK == 0
    d_a = N_Q_HEADS * HEAD_DIM
    d_kv = N_KV_HEADS * HEAD_DIM
    d_b = w_proj_b.shape[0]
    d_in = w_in.shape[1]
    assert d_in == 2 * d_a + 2 * d_kv + 3 * d_b + 2 * D

    half = HEAD_DIM // 2
    inv_freq = ROPE_THETA ** (-jnp.arange(half, dtype=F32) / half)
    invf = jnp.broadcast_to(inv_freq[:, None], (half, LANES))

    c_za, c_u = d_a + 2 * d_kv, 2 * d_a + 2 * d_kv
    c_zb = c_u + 2 * d_b
    col = jnp.arange(d_in)
    halved = ((col >= c_za) & (col < c_u)) | (col >= c_zb)
    inv_sqrt2 = 1.0 / math.sqrt(2.0)
    col_scale = jnp.where(halved, 0.5, jnp.where((col >= c_u) & (col < c_zb), inv_sqrt2, 1.0))
    w_in_b = (w_in * col_scale.astype(F32)[None, :]).astype(BF16)
    w_out_b = (w_out * 0.5).astype(BF16)

    kern = functools.partial(_block_kernel, d_a=d_a, d_kv=d_kv, d_b=d_b, d_model=D)
    return pl.pallas_call(
        kern,
        out_shape=jax.ShapeDtypeStruct((B, S, D), x.dtype),
        grid=(B, S // T),
        in_specs=[
            pl.BlockSpec(memory_space=pltpu.SMEM),
            pl.BlockSpec((1, T, D), lambda b, s: (b, s, 0)),
            pl.BlockSpec((B, T), lambda b, s: (0, s)),
            _const_spec((B, 3 * D)),
            _const_spec((1, D)),
            _const_spec((1, D)),
            _const_spec((half, LANES)),
            _const_spec((D, d_in)),
            _const_spec((1, d_b)),
            _const_spec((1, d_b)),
            _const_spec((N_GROUPS, CHUNK, CHUNK)),
            _const_spec((CHUNK, N_GROUPS)),
            _const_spec((d_a, D)),
            _const_spec((d_b, D)),
            _const_spec((D, D)),
        ],
        out_specs=pl.BlockSpec((1, T, D), lambda b, s: (b, s, 0)),
        scratch_shapes=[
            pltpu.VMEM((N_KV_HEADS, BLOCK + T, HEAD_DIM), BF16),
            pltpu.VMEM((N_KV_HEADS, HEAD_DIM, BLOCK + T), BF16),
        ],
        compiler_params=pltpu.CompilerParams(
            dimension_semantics=("arbitrary", "arbitrary"),
            vmem_limit_bytes=VMEM_LIMIT_BYTES),
        name="hybrid_block",
    )(sinks, x, positions, ada,
      g_pre[None, :], g_post[None, :], invf, w_in_b,
      ln_v_g[None, :], ln_v_b[None, :], w_s, b_s.T,
      w_proj_a.astype(BF16), (w_proj_b * inv_sqrt2).astype(BF16), w_out_b)


def _ada(c, w_ada, b_ada):
    B, D = c.shape
    n_out = w_ada.shape[1]
    tn = D
    return pl.pallas_call(
        _ada_kernel,
        out_shape=jax.ShapeDtypeStruct((B, n_out), F32),
        grid=(n_out // tn,),
        in_specs=[
            pl.BlockSpec((B, D), lambda j: (0, 0)),
            pl.BlockSpec((D, tn), lambda j: (0, j)),
            pl.BlockSpec((1, tn), lambda j: (0, j)),
        ],
        out_specs=pl.BlockSpec((B, tn), lambda j: (0, j)),
        compiler_params=pltpu.CompilerParams(dimension_semantics=("arbitrary",)),
        name="adaln_modulation",
    )(c, w_ada, b_ada[None, :])


def kernel(x, c, positions, w_ada, b_ada, g_pre, g_post, w_in, sinks, ln_v_g, ln_v_b, w_s, b_s,
           w_proj_a, w_proj_b, w_out):
    depth = w_in.shape[0]
    for l in range(depth):
        ada = _ada(c, w_ada[l], b_ada[l])
        x = _layer(x, ada, positions, g_pre[l], g_post[l], w_in[l], sinks[l], ln_v_g[l],
                   ln_v_b[l], w_s[l], b_s[l], w_proj_a[l], w_proj_b[l], w_out[l])
    return x
```

```python
import functools
import math

import jax
import jax.numpy as jnp
from jax import lax
from jax.experimental import pallas as pl
from jax.experimental.pallas import tpu as pltpu

HEAD_DIM = 64
N_Q_HEADS = 8
N_KV_HEADS = 2
GQ = N_Q_HEADS // N_KV_HEADS
WINDOW = 128
BLOCK = 128
ROPE_THETA = 10000.0
CHUNK = 128
N_GROUPS = 4
EPS = 1e-6
NEG = -1e30
LOG2E = 1.4426950408889634

LANES = 128
SEQ_TILE = 1024
SUB_TILE = 256
VMEM_LIMIT_BYTES = 40 * 1024 * 1024

BF16 = jnp.bfloat16
F32 = jnp.float32


def _ada_kernel(c_ref, w_ref, b_ref, o_ref):
    c = c_ref[...]
    c_act = c * jax.nn.sigmoid(c)
    o_ref[...] = jnp.dot(c_act, w_ref[...], preferred_element_type=F32) + b_ref[...]


def _two_sigmoid_of_half(h):
    return jnp.tanh(h) + 1.0


def _silu_of_half(h):
    return h * _two_sigmoid_of_half(h)


def _gelu_times_sqrt2(t):
    return t * (1.0 + lax.erf(t))


def _rope(t, cos, sin_signed, first_half):
    outs = []
    for c in range(t.shape[1] // LANES):
        tc = t[:, c * LANES:(c + 1) * LANES]
        rot = jnp.where(first_half,
                        pltpu.roll(tc, LANES - HEAD_DIM // 2, axis=1),
                        pltpu.roll(tc, HEAD_DIM // 2, axis=1))
        outs.append(tc * cos + rot * sin_signed)
    return outs


def _block_kernel(sinks_ref, x_ref, pos_ref, ada_ref, gpre_ref, gpost_ref, invf_ref,
                  win_ref, lng_ref, lnb_ref, ws_ref, bs_ref, wpa_ref, wpb_ref, wout_ref,
                  o_ref, k_scr, vt_scr, *, d_a, d_kv, d_b, d_model):
    T = x_ref.shape[1]
    R = SUB_TILE
    n_sub = T // R
    blk_per_sub = R // BLOCK
    s_idx = pl.program_id(1)

    c_q, c_k, c_v = 0, d_a, d_a + d_kv
    c_za = c_v + d_kv
    c_u = c_za + d_a
    c_vb = c_u + d_b
    c_zb = c_vb + d_b
    c_g = c_zb + d_b

    @pl.when(s_idx == 0)
    def _():
        k_scr[:, 0:BLOCK, :] = jnp.zeros((N_KV_HEADS, BLOCK, HEAD_DIM), BF16)
        vt_scr[:, :, 0:BLOCK] = jnp.zeros((N_KV_HEADS, HEAD_DIM, BLOCK), BF16)

    b_idx = pl.program_id(0)
    ada = ada_ref[pl.ds(b_idx, 1), :]
    shift = ada[:, 0:d_model]
    scale = ada[:, d_model:2 * d_model]
    gate = ada[:, 2 * d_model:3 * d_model]
    pre_gain = gpre_ref[...] * (1.0 + scale)
    post_gain = gate * gpost_ref[...]
    invf = invf_ref[...]
    lane = lax.broadcasted_iota(jnp.int32, (1, LANES), 1)
    first_half = (lane % HEAD_DIM) < (HEAD_DIM // 2)

    kj = lax.broadcasted_iota(jnp.int32, (2 * BLOCK, BLOCK), 0)
    qi = lax.broadcasted_iota(jnp.int32, (2 * BLOCK, BLOCK), 1)
    rel = qi + BLOCK - kj
    in_win = (rel >= 0) & (rel < WINDOW)
    first_lo = jnp.where(s_idx == 0, BLOCK, 0)
    ti = lax.broadcasted_iota(jnp.int32, (CHUNK, CHUNK), 0)
    si = lax.broadcasted_iota(jnp.int32, (CHUNK, CHUNK), 1)
    causal = si <= ti
    group_w = d_b // N_GROUPS
    q_scale = LOG2E / math.sqrt(HEAD_DIM)

    st = [dict() for _ in range(n_sub)]

    def proj(j, lo, hi):
        return jnp.dot(st[j]["hb"], win_ref[:, lo:hi], preferred_element_type=F32)

    def head(j):
        xc = x_ref[0, j * R:(j + 1) * R, :]
        ms = jnp.mean(xc * xc, axis=-1, keepdims=True)
        st[j]["hb"] = ((xc * lax.rsqrt(ms + EPS)) * pre_gain + shift).astype(BF16)
        st[j]["qkv"] = proj(j, c_q, c_za)

    def gmlp_in(j):
        v = _gelu_times_sqrt2(proj(j, c_vb, c_zb))
        mu = jnp.mean(v, axis=-1, keepdims=True)
        vc = v - mu
        var = jnp.mean(vc * vc, axis=-1, keepdims=True)
        st[j]["vn"] = (vc * lax.rsqrt(var + 2.0 * EPS) * lng_ref[...] + lnb_ref[...]).astype(BF16)
        st[j]["u"] = _gelu_times_sqrt2(proj(j, c_u, c_vb))

    def rope_and_scores(j):
        pos = pos_ref[pl.ds(b_idx, 1), j * R:(j + 1) * R].astype(F32)
        cos_rows, sin_rows = [], []
        for b in range(blk_per_sub):
            ang = invf * pos[:, b * BLOCK:(b + 1) * BLOCK]
            cs = jnp.cos(ang)
            sn = jnp.sin(ang)
            cos_rows.append(jnp.concatenate([cs, cs, cs, cs], axis=0).T)
            sin_rows.append(jnp.concatenate([-sn, sn, -sn, sn], axis=0).T)
        cos = jnp.concatenate(cos_rows, axis=0)
        sin_signed = jnp.concatenate(sin_rows, axis=0)
        qkv = st[j]["qkv"]
        q_tiles = _rope(qkv[:, c_q:c_k], cos, sin_signed, first_half)
        k_r = _rope(qkv[:, c_k:c_v], cos, sin_signed, first_half)[0]
        vt_new = qkv[:, c_v:c_za].T
        lo = BLOCK + j * R
        for g in range(N_KV_HEADS):
            k_scr[g, lo:lo + R, :] = k_r[:, g * HEAD_DIM:(g + 1) * HEAD_DIM].astype(BF16)
            vt_scr[g, :, lo:lo + R] = vt_new[g * HEAD_DIM:(g + 1) * HEAD_DIM, :].astype(BF16)
        q_heads = []
        for c in range(len(q_tiles)):
            qs = (q_tiles[c] * q_scale).astype(BF16)
            q_heads.append(qs[:, 0:HEAD_DIM])
            q_heads.append(qs[:, HEAD_DIM:2 * HEAD_DIM])
        ss = []
        for b in range(blk_per_sub):
            n = j * blk_per_sub + b
            band = slice(n * BLOCK, n * BLOCK + 2 * BLOCK)
            for g in range(N_KV_HEADS):
                kb = k_scr[g, band, :]
                q_stack = jnp.concatenate(
                    [q_heads[g * GQ + h][b * BLOCK:(b + 1) * BLOCK] for h in range(GQ)], axis=0)
                ss.append(lax.dot_general(kb, q_stack, (((1,), (1,)), ((), ())),
                                          preferred_element_type=F32))
        st[j]["scores"] = ss

    def pv_unit(j, idx):
        b, g = divmod(idx, N_KV_HEADS)
        n = j * blk_per_sub + b
        band = slice(n * BLOCK, n * BLOCK + 2 * BLOCK)
        s = st[j]["scores"][idx]
        valid = in_win & (kj >= first_lo) if n == 0 else in_win
        s = jnp.where(jnp.concatenate([valid] * GQ, axis=1), s, NEG)
        sink = jnp.concatenate(
            [jnp.full((1, BLOCK), sinks_ref[g * GQ + h] * LOG2E, F32) for h in range(GQ)], axis=1)
        m = jnp.maximum(jnp.max(s, axis=0, keepdims=True), sink)
        p = jnp.exp2(s - m)
        denom = jnp.sum(p, axis=0, keepdims=True) + jnp.exp2(sink - m)
        ot = jnp.dot(vt_scr[g, :, band], p.astype(BF16),
                     preferred_element_type=F32) * (1.0 / denom)
        for pair in range(GQ // 2):
            two = jnp.concatenate(
                [ot[:, (2 * pair) * BLOCK:(2 * pair + 1) * BLOCK],
                 ot[:, (2 * pair + 1) * BLOCK:(2 * pair + 2) * BLOCK]], axis=0)
            st[j]["attn"][b][g * (GQ // 2) + pair] = two.T

    def gating(j):
        rows_out = []
        for n in range(R // CHUNK):
            rows = slice(n * CHUNK, (n + 1) * CHUNK)
            cols_out = []
            for g in range(N_GROUPS):
                w_g = jnp.where(causal, ws_ref[g], 0.0).astype(BF16)
                bias = jnp.broadcast_to(bs_ref[:, g:g + 1], (CHUNK, group_w))
                cols = slice(g * group_w, (g + 1) * group_w)
                sv = jnp.dot(w_g, st[j]["vn"][rows, cols], preferred_element_type=F32) + bias
                cols_out.append(st[j]["u"][rows, cols] * sv)
            rows_out.append(jnp.concatenate(cols_out, axis=1))
        st[j]["yb"] = jnp.concatenate(rows_out, axis=0)

    def branch_proj(j):
        attn = jnp.concatenate([jnp.concatenate(row, axis=1) for row in st[j]["attn"]], axis=0)
        y_a = attn * _silu_of_half(st[j]["za"])
        st[j]["pa"] = jnp.dot(y_a.astype(BF16), wpa_ref[...], preferred_element_type=F32)
        y_b = st[j]["yb"] * _silu_of_half(st[j]["zb"])
        st[j]["pb"] = jnp.dot(y_b.astype(BF16), wpb_ref[...], preferred_element_type=F32)

    def out_proj(j):
        merged2 = (_two_sigmoid_of_half(st[j]["ga"]) * st[j]["pa"]
                   + _two_sigmoid_of_half(st[j]["gb"]) * st[j]["pb"]).astype(BF16)
        y = jnp.dot(merged2, wout_ref[...], preferred_element_type=F32)
        ms_y = jnp.mean(y * y, axis=-1, keepdims=True)
        rows = slice(j * R, (j + 1) * R)
        o_ref[0, rows, :] = x_ref[0, rows, :] + (y * lax.rsqrt(ms_y + EPS)) * post_gain
        st[j].clear()

    n_units = blk_per_sub * N_KV_HEADS
    for j in range(n_sub + 1):
        if j < n_sub:
            head(j)
            gmlp_in(j)
        if j >= 1:
            out_proj(j - 1)
        if j < n_sub:
            rope_and_scores(j)
            st[j]["attn"] = [[None] * (N_Q_HEADS // 2) for _ in range(blk_per_sub)]
            st[j]["za"] = proj(j, c_za, c_u)
            st[j]["zb"] = proj(j, c_zb, c_g)
            st[j]["ga"] = proj(j, c_g, c_g + d_model)
            for idx in range(n_units):
                pv_unit(j, idx)
            gating(j)
            st[j]["gb"] = proj(j, c_g + d_model, c_g + 2 * d_model)
            branch_proj(j)

    k_scr[:, 0:BLOCK, :] = k_scr[:, T:T + BLOCK, :]
    vt_scr[:, :, 0:BLOCK] = vt_scr[:, :, T:T + BLOCK]


def _const_spec(shape):
    return pl.BlockSpec(shape, lambda b, s: (0,) * len(shape), pipeline_mode=pl.Buffered(1))


def _layer(x, ada, positions, g_pre, g_post, w_in, sinks, ln_v_g, ln_v_b, w_s, b_s,
           w_proj_a, w_proj_b, w_out):
    B, S, D = x.shape
    T = SEQ_TILE
    assert S % T == 0 and T % SUB_TILE == 0 and SUB_TILE % BLOCK == 0 and SUB_TILE % CHUNK == 0
    d_a = N_Q_HEADS * HEAD_DIM
    d_kv = N_KV_HEADS * HEAD_DIM
    d_b = w_proj_b.shape[0]
    d_in = w_in.shape[1]
    assert d_in == 2 * d_a + 2 * d_kv + 3 * d_b + 2 * D

    half = HEAD_DIM // 2
    inv_freq = ROPE_THETA ** (-jnp.arange(half, dtype=F32) / half)
    invf = jnp.broadcast_to(inv_freq[:, None], (half, LANES))

    c_za, c_u = d_a + 2 * d_kv, 2 * d_a + 2 * d_kv
    c_zb = c_u + 2 * d_b
    col = jnp.arange(d_in)
    halved = ((col >= c_za) & (col < c_u)) | (col >= c_zb)
    inv_sqrt2 = 1.0 / math.sqrt(2.0)
    col_scale = jnp.where(halved, 0.5, jnp.where((col >= c_u) & (col < c_zb), inv_sqrt2, 1.0))
    w_in_b = (w_in * col_scale.astype(F32)[None, :]).astype(BF16)
    w_out_b = (w_out * 0.5).astype(BF16)

    kern = functools.partial(_block_kernel, d_a=d_a, d_kv=d_kv, d_b=d_b, d_model=D)
    return pl.pallas_call(
        kern,
        out_shape=jax.ShapeDtypeStruct((B, S, D), x.dtype),
        grid=(B, S // T),
        in_specs=[
            pl.BlockSpec(memory_space=pltpu.SMEM),
            pl.BlockSpec((1, T, D), lambda b, s: (b, s, 0)),
            pl.BlockSpec((B, T), lambda b, s: (0, s)),
            _const_spec((B, 3 * D)),
            _const_spec((1, D)),
            _const_spec((1, D)),
            _const_spec((half, LANES)),
            _const_spec((D, d_in)),
            _const_spec((1, d_b)),
            _const_spec((1, d_b)),
            _const_spec((N_GROUPS, CHUNK, CHUNK)),
            _const_spec((CHUNK, N_GROUPS)),
            _const_spec((d_a, D)),
            _const_spec((d_b, D)),
            _const_spec((D, D)),
        ],
        out_specs=pl.BlockSpec((1, T, D), lambda b, s: (b, s, 0)),
        scratch_shapes=[
            pltpu.VMEM((N_KV_HEADS, BLOCK + T, HEAD_DIM), BF16),
            pltpu.VMEM((N_KV_HEADS, HEAD_DIM, BLOCK + T), BF16),
        ],
        compiler_params=pltpu.CompilerParams(
            dimension_semantics=("arbitrary", "arbitrary"),
            vmem_limit_bytes=VMEM_LIMIT_BYTES),
        name="hybrid_block",
    )(sinks, x, positions, ada,
      g_pre[None, :], g_post[None, :], invf, w_in_b,
      ln_v_g[None, :], ln_v_b[None, :], w_s, b_s.T,
      w_proj_a.astype(BF16), (w_proj_b * inv_sqrt2).astype(BF16), w_out_b)


def _ada(c, w_ada, b_ada):
    B, D = c.shape
    n_out = w_ada.shape[1]
    tn = D
    return pl.pallas_call(
        _ada_kernel,
        out_shape=jax.ShapeDtypeStruct((B, n_out), F32),
        grid=(n_out // tn,),
        in_specs=[
            pl.BlockSpec((B, D), lambda j: (0, 0)),
            pl.BlockSpec((D, tn), lambda j: (0, j)),
            pl.BlockSpec((1, tn), lambda j: (0, j)),
        ],
        out_specs=pl.BlockSpec((B, tn), lambda j: (0, j)),
        compiler_params=pltpu.CompilerParams(dimension_semantics=("arbitrary",)),
        name="adaln_modulation",
    )(c, w_ada, b_ada[None, :])


def kernel(x, c, positions, w_ada, b_ada, g_pre, g_post, w_in, sinks, ln_v_g, ln_v_b, w_s, b_s,
           w_proj_a, w_proj_b, w_out):
    depth = w_in.shape[0]
    for l in range(depth):
        ada = _ada(c, w_ada[l], b_ada[l])
        x = _layer(x, ada, positions, g_pre[l], g_post[l], w_in[l], sinks[l], ln_v_g[l],
                   ln_v_b[l], w_s[l], b_s[l], w_proj_a[l], w_proj_b[l], w_out[l])
    return x
```

```python
import functools
import math

import jax
import jax.numpy as jnp
from jax import lax
from jax.experimental import pallas as pl
from jax.experimental.pallas import tpu as pltpu

HEAD_DIM = 64
N_Q_HEADS = 8
N_KV_HEADS = 2
GQ = N_Q_HEADS // N_KV_HEADS
WINDOW = 128
BLOCK = 128
ROPE_THETA = 10000.0
CHUNK = 128
N_GROUPS = 4
EPS = 1e-6
NEG = -1e30
LOG2E = 1.4426950408889634

LANES = 128
SEQ_TILE = 1024
SUB_TILE = 256
VMEM_LIMIT_BYTES = 44 * 1024 * 1024

BF16 = jnp.bfloat16
F32 = jnp.float32


def _ada_kernel(c_ref, w_ref, b_ref, o_ref):
    c = c_ref[...]
    c_act = c * jax.nn.sigmoid(c)
    o_ref[...] = jnp.dot(c_act, w_ref[...], preferred_element_type=F32) + b_ref[...]


def _two_sigmoid_of_half(h):
    return jnp.tanh(h) + 1.0


def _silu_of_half(h):
    return h * _two_sigmoid_of_half(h)


def _gelu_times_sqrt2(t):
    return t * (1.0 + lax.erf(t))


def _rope(t, cos, sin_signed, first_half):
    outs = []
    for c in range(t.shape[1] // LANES):
        tc = t[:, c * LANES:(c + 1) * LANES]
        rot = jnp.where(first_half,
                        pltpu.roll(tc, LANES - HEAD_DIM // 2, axis=1),
                        pltpu.roll(tc, HEAD_DIM // 2, axis=1))
        outs.append(tc * cos + rot * sin_signed)
    return outs


def _block_kernel(sinks_ref, x_ref, pos_ref, ada_ref, gpre_ref, gpost_ref, invf_ref,
                  win_ref, lng_ref, lnb_ref, ws_ref, bs_ref, wpa_ref, wpb_ref, wout_ref,
                  o_ref, k_scr, vt_scr, hb_scr, *, d_a, d_kv, d_b, d_model):
    T = x_ref.shape[1]
    R = SUB_TILE
    n_sub = T // R
    blk_per_sub = R // BLOCK
    s_idx = pl.program_id(1)

    c_q, c_k, c_v = 0, d_a, d_a + d_kv
    c_za = c_v + d_kv
    c_u = c_za + d_a
    c_vb = c_u + d_b
    c_zb = c_vb + d_b
    c_g = c_zb + d_b

    @pl.when(s_idx == 0)
    def _():
        k_scr[:, 0:BLOCK, :] = jnp.zeros((N_KV_HEADS, BLOCK, HEAD_DIM), BF16)
        vt_scr[:, :, 0:BLOCK] = jnp.zeros((N_KV_HEADS, HEAD_DIM, BLOCK), BF16)

    b_idx = pl.program_id(0)
    ada = ada_ref[pl.ds(b_idx, 1), :]
    shift = ada[:, 0:d_model]
    scale = ada[:, d_model:2 * d_model]
    gate = ada[:, 2 * d_model:3 * d_model]
    pre_gain = gpre_ref[...] * (1.0 + scale)
    post_gain = gate * gpost_ref[...]
    invf = invf_ref[...]
    lane = lax.broadcasted_iota(jnp.int32, (1, LANES), 1)
    first_half = (lane % HEAD_DIM) < (HEAD_DIM // 2)

    kj = lax.broadcasted_iota(jnp.int32, (2 * BLOCK, BLOCK), 0)
    qi = lax.broadcasted_iota(jnp.int32, (2 * BLOCK, BLOCK), 1)
    rel = qi + BLOCK - kj
    in_win = (rel >= 0) & (rel < WINDOW)
    first_lo = jnp.where(s_idx == 0, BLOCK, 0)
    ti = lax.broadcasted_iota(jnp.int32, (CHUNK, CHUNK), 0)
    si = lax.broadcasted_iota(jnp.int32, (CHUNK, CHUNK), 1)
    causal = si <= ti
    group_w = d_b // N_GROUPS
    q_scale = LOG2E / math.sqrt(HEAD_DIM)

    st = [dict() for _ in range(n_sub)]

    def proj(j, lo, hi):
        return jnp.dot(hb_scr[j * R:(j + 1) * R, :], win_ref[:, lo:hi], preferred_element_type=F32)

    def head(j):
        xc = x_ref[0, j * R:(j + 1) * R, :]
        ms = jnp.mean(xc * xc, axis=-1, keepdims=True)
        hb_scr[j * R:(j + 1) * R, :] = ((xc * lax.rsqrt(ms + EPS)) * pre_gain + shift).astype(BF16)
        st[j]["qkv"] = proj(j, c_q, c_za)

    def gmlp_in(j):
        v = _gelu_times_sqrt2(proj(j, c_vb, c_zb))
        mu = jnp.mean(v, axis=-1, keepdims=True)
        vc = v - mu
        var = jnp.mean(vc * vc, axis=-1, keepdims=True)
        st[j]["vn"] = (vc * lax.rsqrt(var + 2.0 * EPS) * lng_ref[...] + lnb_ref[...]).astype(BF16)
        st[j]["u"] = _gelu_times_sqrt2(proj(j, c_u, c_vb))

    def rope_and_scores(j):
        pos = pos_ref[pl.ds(b_idx, 1), j * R:(j + 1) * R].astype(F32)
        cos_rows, sin_rows = [], []
        for b in range(blk_per_sub):
            ang = invf * pos[:, b * BLOCK:(b + 1) * BLOCK]
            cs = jnp.cos(ang)
            sn = jnp.sin(ang)
            cos_rows.append(jnp.concatenate([cs, cs, cs, cs], axis=0).T)
            sin_rows.append(jnp.concatenate([-sn, sn, -sn, sn], axis=0).T)
        cos = jnp.concatenate(cos_rows, axis=0)
        sin_signed = jnp.concatenate(sin_rows, axis=0)
        qkv = st[j]["qkv"]
        q_tiles = _rope(qkv[:, c_q:c_k], cos, sin_signed, first_half)
        k_r = _rope(qkv[:, c_k:c_v], cos, sin_signed, first_half)[0]
        vt_new = qkv[:, c_v:c_za].T
        lo = BLOCK + j * R
        for g in range(N_KV_HEADS):
            k_scr[g, lo:lo + R, :] = k_r[:, g * HEAD_DIM:(g + 1) * HEAD_DIM].astype(BF16)
            vt_scr[g, :, lo:lo + R] = vt_new[g * HEAD_DIM:(g + 1) * HEAD_DIM, :].astype(BF16)
        q_heads = []
        for c in range(len(q_tiles)):
            qs = (q_tiles[c] * q_scale).astype(BF16)
            q_heads.append(qs[:, 0:HEAD_DIM])
            q_heads.append(qs[:, HEAD_DIM:2 * HEAD_DIM])
        ss = []
        for b in range(blk_per_sub):
            n = j * blk_per_sub + b
            band = slice(n * BLOCK, n * BLOCK + 2 * BLOCK)
            for g in range(N_KV_HEADS):
                kb = k_scr[g, band, :]
                q_stack = jnp.concatenate(
                    [q_heads[g * GQ + h][b * BLOCK:(b + 1) * BLOCK] for h in range(GQ)], axis=0)
                ss.append(lax.dot_general(kb, q_stack, (((1,), (1,)), ((), ())),
                                          preferred_element_type=F32))
        st[j]["scores"] = ss

    def pv_unit(j, idx):
        b, g = divmod(idx, N_KV_HEADS)
        n = j * blk_per_sub + b
        band = slice(n * BLOCK, n * BLOCK + 2 * BLOCK)
        s = st[j]["scores"][idx]
        valid = in_win & (kj >= first_lo) if n == 0 else in_win
        s = jnp.where(jnp.concatenate([valid] * GQ, axis=1), s, NEG)
        sink = jnp.concatenate(
            [jnp.full((1, BLOCK), sinks_ref[g * GQ + h] * LOG2E, F32) for h in range(GQ)], axis=1)
        m = jnp.maximum(jnp.max(s, axis=0, keepdims=True), sink)
        p = jnp.exp2(s - m)
        denom = jnp.sum(p, axis=0, keepdims=True) + jnp.exp2(sink - m)
        ot = jnp.dot(vt_scr[g, :, band], p.astype(BF16),
                     preferred_element_type=F32) * (1.0 / denom)
        for pair in range(GQ // 2):
            two = jnp.concatenate(
                [ot[:, (2 * pair) * BLOCK:(2 * pair + 1) * BLOCK],
                 ot[:, (2 * pair + 1) * BLOCK:(2 * pair + 2) * BLOCK]], axis=0)
            st[j]["attn"][b][g * (GQ // 2) + pair] = two.T

    def gating(j):
        rows_out = []
        for n in range(R // CHUNK):
            rows = slice(n * CHUNK, (n + 1) * CHUNK)
            cols_out = []
            for g in range(N_GROUPS):
                w_g = jnp.where(causal, ws_ref[g], 0.0).astype(BF16)
                bias = jnp.broadcast_to(bs_ref[:, g:g + 1], (CHUNK, group_w))
                cols = slice(g * group_w, (g + 1) * group_w)
                sv = jnp.dot(w_g, st[j]["vn"][rows, cols], preferred_element_type=F32) + bias
                cols_out.append(st[j]["u"][rows, cols] * sv)
            rows_out.append(jnp.concatenate(cols_out, axis=1))
        st[j]["yb"] = jnp.concatenate(rows_out, axis=0)

    def branch_proj(j):
        attn = jnp.concatenate([jnp.concatenate(row, axis=1) for row in st[j]["attn"]], axis=0)
        y_a = attn * _silu_of_half(st[j]["za"])
        st[j]["pa"] = jnp.dot(y_a.astype(BF16), wpa_ref[...], preferred_element_type=F32)
        y_b = st[j]["yb"] * _silu_of_half(st[j]["zb"])
        st[j]["pb"] = jnp.dot(y_b.astype(BF16), wpb_ref[...], preferred_element_type=F32)

    def out_proj(j):
        merged2 = (_two_sigmoid_of_half(st[j]["ga"]) * st[j]["pa"]
                   + _two_sigmoid_of_half(st[j]["gb"]) * st[j]["pb"]).astype(BF16)
        y = jnp.dot(merged2, wout_ref[...], preferred_element_type=F32)
        ms_y = jnp.mean(y * y, axis=-1, keepdims=True)
        rows = slice(j * R, (j + 1) * R)
        o_ref[0, rows, :] = x_ref[0, rows, :] + (y * lax.rsqrt(ms_y + EPS)) * post_gain
        st[j].clear()

    n_units = blk_per_sub * N_KV_HEADS
    for j in range(n_sub + 1):
        if j < n_sub:
            head(j)
            gmlp_in(j)
        if j >= 1:
            out_proj(j - 1)
        if j < n_sub:
            rope_and_scores(j)
            st[j]["attn"] = [[None] * (N_Q_HEADS // 2) for _ in range(blk_per_sub)]
            st[j]["za"] = proj(j, c_za, c_u)
            st[j]["zb"] = proj(j, c_zb, c_g)
            st[j]["ga"] = proj(j, c_g, c_g + d_model)
            for idx in range(n_units):
                pv_unit(j, idx)
            gating(j)
            st[j]["gb"] = proj(j, c_g + d_model, c_g + 2 * d_model)
            branch_proj(j)

    k_scr[:, 0:BLOCK, :] = k_scr[:, T:T + BLOCK, :]
    vt_scr[:, :, 0:BLOCK] = vt_scr[:, :, T:T + BLOCK]


def _const_spec(shape):
    return pl.BlockSpec(shape, lambda b, s: (0,) * len(shape), pipeline_mode=pl.Buffered(1))


def _layer(x, ada, positions, g_pre, g_post, w_in, sinks, ln_v_g, ln_v_b, w_s, b_s,
           w_proj_a, w_proj_b, w_out):
    B, S, D = x.shape
    T = SEQ_TILE
    assert S % T == 0 and T % SUB_TILE == 0 and SUB_TILE % BLOCK == 0 and SUB_TILE % CHUNK == 0
    d_a = N_Q_HEADS * HEAD_DIM
    d_kv = N_KV_HEADS * HEAD_DIM
    d_b = w_proj_b.shape[0]
    d_in = w_in.shape[1]
    assert d_in == 2 * d_a + 2 * d_kv + 3 * d_b + 2 * D

    half = HEAD_DIM // 2
    inv_freq = ROPE_THETA ** (-jnp.arange(half, dtype=F32) / half)
    invf = jnp.broadcast_to(inv_freq[:, None], (half, LANES))

    c_za, c_u = d_a + 2 * d_kv, 2 * d_a + 2 * d_kv
    c_zb = c_u + 2 * d_b
    col = jnp.arange(d_in)
    halved = ((col >= c_za) & (col < c_u)) | (col >= c_zb)
    inv_sqrt2 = 1.0 / math.sqrt(2.0)
    col_scale = jnp.where(halved, 0.5, jnp.where((col >= c_u) & (col < c_zb), inv_sqrt2, 1.0))
    w_in_b = (w_in * col_scale.astype(F32)[None, :]).astype(BF16)
    w_out_b = (w_out * 0.5).astype(BF16)

    kern = functools.partial(_block_kernel, d_a=d_a, d_kv=d_kv, d_b=d_b, d_model=D)
    return pl.pallas_call(
        kern,
        out_shape=jax.ShapeDtypeStruct((B, S, D), x.dtype),
        grid=(B, S // T),
        in_specs=[
            pl.BlockSpec(memory_space=pltpu.SMEM),
            pl.BlockSpec((1, T, D), lambda b, s: (b, s, 0)),
            pl.BlockSpec((B, T), lambda b, s: (0, s)),
            _const_spec((B, 3 * D)),
            _const_spec((1, D)),
            _const_spec((1, D)),
            _const_spec((half, LANES)),
            _const_spec((D, d_in)),
            _const_spec((1, d_b)),
            _const_spec((1, d_b)),
            _const_spec((N_GROUPS, CHUNK, CHUNK)),
            _const_spec((CHUNK, N_GROUPS)),
            _const_spec((d_a, D)),
            _const_spec((d_b, D)),
            _const_spec((D, D)),
        ],
        out_specs=pl.BlockSpec((1, T, D), lambda b, s: (b, s, 0)),
        scratch_shapes=[
            pltpu.VMEM((N_KV_HEADS, BLOCK + T, HEAD_DIM), BF16),
            pltpu.VMEM((N_KV_HEADS, HEAD_DIM, BLOCK + T), BF16),
            pltpu.VMEM((T, D), BF16),
        ],
        compiler_params=pltpu.CompilerParams(
            dimension_semantics=("arbitrary", "arbitrary"),
            vmem_limit_bytes=VMEM_LIMIT_BYTES),
        name="hybrid_block",
    )(sinks, x, positions, ada,
      g_pre[None, :], g_post[None, :], invf, w_in_b,
      ln_v_g[None, :], ln_v_b[None, :], w_s, b_s.T,
      w_proj_a.astype(BF16), (w_proj_b * inv_sqrt2).astype(BF16), w_out_b)


def _ada(c, w_ada, b_ada):
    B, D = c.shape
    n_out = w_ada.shape[1]
    tn = D
    return pl.pallas_call(
        _ada_kernel,
        out_shape=jax.ShapeDtypeStruct((B, n_out), F32),
        grid=(n_out // tn,),
        in_specs=[
            pl.BlockSpec((B, D), lambda j: (0, 0)),
            pl.BlockSpec((D, tn), lambda j: (0, j)),
            pl.BlockSpec((1, tn), lambda j: (0, j)),
        ],
        out_specs=pl.BlockSpec((B, tn), lambda j: (0, j)),
        compiler_params=pltpu.CompilerParams(dimension_semantics=("arbitrary",)),
        name="adaln_modulation",
    )(c, w_ada, b_ada[None, :])


def kernel(x, c, positions, w_ada, b_ada, g_pre, g_post, w_in, sinks, ln_v_g, ln_v_b, w_s, b_s,
           w_proj_a, w_proj_b, w_out):
    depth = w_in.shape[0]
    for l in range(depth):
        ada = _ada(c, w_ada[l], b_ada[l])
        x = _layer(x, ada, positions, g_pre[l], g_post[l], w_in[l], sinks[l], ln_v_g[l],
                   ln_v_b[l], w_s[l], b_s[l], w_proj_a[l], w_proj_b[l], w_out[l])
    return x
```

```python
import functools
import math

import jax
import jax.numpy as jnp
from jax import lax
from jax.experimental import pallas as pl
from jax.experimental.pallas import tpu as pltpu

HEAD_DIM = 64
N_Q_HEADS = 8
N_KV_HEADS = 2
GQ = N_Q_HEADS // N_KV_HEADS
WINDOW = 128
BLOCK = 128
ROPE_THETA = 10000.0
CHUNK = 128
N_GROUPS = 4
EPS = 1e-6
NEG = -1e30
LOG2E = 1.4426950408889634

LANES = 128
SEQ_TILE = 512
SUB_TILE = 256
VMEM_LIMIT_BYTES = 36 * 1024 * 1024

BF16 = jnp.bfloat16
F32 = jnp.float32


def _ada_kernel(c_ref, w_ref, b_ref, o_ref):
    c = c_ref[...]
    c_act = c * jax.nn.sigmoid(c)
    o_ref[...] = jnp.dot(c_act, w_ref[...], preferred_element_type=F32) + b_ref[...]


def _two_sigmoid_of_half(h):
    return jnp.tanh(h) + 1.0


def _silu_of_half(h):
    return h * _two_sigmoid_of_half(h)


def _gelu_times_sqrt2(t):
    return t * (1.0 + lax.erf(t))


def _rope(t, cos, sin_signed, first_half):
    outs = []
    for c in range(t.shape[1] // LANES):
        tc = t[:, c * LANES:(c + 1) * LANES]
        rot = jnp.where(first_half,
                        pltpu.roll(tc, LANES - HEAD_DIM // 2, axis=1),
                        pltpu.roll(tc, HEAD_DIM // 2, axis=1))
        outs.append(tc * cos + rot * sin_signed)
    return outs


def _block_kernel(sinks_ref, x_ref, pos_ref, ada_ref, gpre_ref, gpost_ref, invf_ref,
                  win_ref, lng_ref, lnb_ref, ws_ref, bs_ref, wpa_ref, wpb_ref, wout_ref,
                  o_ref, k_scr, vt_scr, hb_scr, *, d_a, d_kv, d_b, d_model):
    T = x_ref.shape[1]
    R = SUB_TILE
    n_sub = T // R
    blk_per_sub = R // BLOCK
    s_idx = pl.program_id(1)

    c_q, c_k, c_v = 0, d_a, d_a + d_kv
    c_za = c_v + d_kv
    c_u = c_za + d_a
    c_vb = c_u + d_b
    c_zb = c_vb + d_b
    c_g = c_zb + d_b

    @pl.when(s_idx == 0)
    def _():
        k_scr[:, 0:BLOCK, :] = jnp.zeros((N_KV_HEADS, BLOCK, HEAD_DIM), BF16)
        vt_scr[:, :, 0:BLOCK] = jnp.zeros((N_KV_HEADS, HEAD_DIM, BLOCK), BF16)

    b_idx = pl.program_id(0)
    ada = ada_ref[pl.ds(b_idx, 1), :]
    shift = ada[:, 0:d_model]
    scale = ada[:, d_model:2 * d_model]
    gate = ada[:, 2 * d_model:3 * d_model]
    pre_gain = gpre_ref[...] * (1.0 + scale)
    post_gain = gate * gpost_ref[...]
    invf = invf_ref[...]
    lane = lax.broadcasted_iota(jnp.int32, (1, LANES), 1)
    first_half = (lane % HEAD_DIM) < (HEAD_DIM // 2)

    kj = lax.broadcasted_iota(jnp.int32, (2 * BLOCK, BLOCK), 0)
    qi = lax.broadcasted_iota(jnp.int32, (2 * BLOCK, BLOCK), 1)
    rel = qi + BLOCK - kj
    in_win = (rel >= 0) & (rel < WINDOW)
    first_lo = jnp.where(s_idx == 0, BLOCK, 0)
    ti = lax.broadcasted_iota(jnp.int32, (CHUNK, CHUNK), 0)
    si = lax.broadcasted_iota(jnp.int32, (CHUNK, CHUNK), 1)
    causal = si <= ti
    group_w = d_b // N_GROUPS
    q_scale = LOG2E / math.sqrt(HEAD_DIM)

    st = [dict() for _ in range(n_sub)]

    def proj(j, lo, hi):
        return jnp.dot(hb_scr[j * R:(j + 1) * R, :], win_ref[:, lo:hi], preferred_element_type=F32)

    def head(j):
        xc = x_ref[0, j * R:(j + 1) * R, :]
        ms = jnp.mean(xc * xc, axis=-1, keepdims=True)
        hb_scr[j * R:(j + 1) * R, :] = ((xc * lax.rsqrt(ms + EPS)) * pre_gain + shift).astype(BF16)
        st[j]["qkv"] = proj(j, c_q, c_za)

    def gmlp_in(j):
        v = _gelu_times_sqrt2(proj(j, c_vb, c_zb))
        mu = jnp.mean(v, axis=-1, keepdims=True)
        vc = v - mu
        var = jnp.mean(vc * vc, axis=-1, keepdims=True)
        st[j]["vn"] = (vc * lax.rsqrt(var + 2.0 * EPS) * lng_ref[...] + lnb_ref[...]).astype(BF16)
        st[j]["u"] = _gelu_times_sqrt2(proj(j, c_u, c_vb))

    def rope_and_scores(j):
        pos = pos_ref[pl.ds(b_idx, 1), j * R:(j + 1) * R].astype(F32)
        cos_rows, sin_rows = [], []
        for b in range(blk_per_sub):
            ang = invf * pos[:, b * BLOCK:(b + 1) * BLOCK]
            cs = jnp.cos(ang)
            sn = jnp.sin(ang)
            cos_rows.append(jnp.concatenate([cs, cs, cs, cs], axis=0).T)
            sin_rows.append(jnp.concatenate([-sn, sn, -sn, sn], axis=0).T)
        cos = jnp.concatenate(cos_rows, axis=0)
        sin_signed = jnp.concatenate(sin_rows, axis=0)
        qkv = st[j]["qkv"]
        q_tiles = _rope(qkv[:, c_q:c_k], cos, sin_signed, first_half)
        k_r = _rope(qkv[:, c_k:c_v], cos, sin_signed, first_half)[0]
        vt_new = qkv[:, c_v:c_za].T
        lo = BLOCK + j * R
        for g in range(N_KV_HEADS):
            k_scr[g, lo:lo + R, :] = k_r[:, g * HEAD_DIM:(g + 1) * HEAD_DIM].astype(BF16)
            vt_scr[g, :, lo:lo + R] = vt_new[g * HEAD_DIM:(g + 1) * HEAD_DIM, :].astype(BF16)
        q_heads = []
        for c in range(len(q_tiles)):
            qs = (q_tiles[c] * q_scale).astype(BF16)
            q_heads.append(qs[:, 0:HEAD_DIM])
            q_heads.append(qs[:, HEAD_DIM:2 * HEAD_DIM])
        ss = []
        for b in range(blk_per_sub):
            n = j * blk_per_sub + b
            band = slice(n * BLOCK, n * BLOCK + 2 * BLOCK)
            for g in range(N_KV_HEADS):
                kb = k_scr[g, band, :]
                q_stack = jnp.concatenate(
                    [q_heads[g * GQ + h][b * BLOCK:(b + 1) * BLOCK] for h in range(GQ)], axis=0)
                ss.append(lax.dot_general(kb, q_stack, (((1,), (1,)), ((), ())),
                                          preferred_element_type=F32))
        st[j]["scores"] = ss

    def pv_unit(j, idx):
        b, g = divmod(idx, N_KV_HEADS)
        n = j * blk_per_sub + b
        band = slice(n * BLOCK, n * BLOCK + 2 * BLOCK)
        s = st[j]["scores"][idx]
        valid = in_win & (kj >= first_lo) if n == 0 else in_win
        s = jnp.where(jnp.concatenate([valid] * GQ, axis=1), s, NEG)
        sink = jnp.concatenate(
            [jnp.full((1, BLOCK), sinks_ref[g * GQ + h] * LOG2E, F32) for h in range(GQ)], axis=1)
        m = jnp.maximum(jnp.max(s, axis=0, keepdims=True), sink)
        p = jnp.exp2(s - m)
        denom = jnp.sum(p, axis=0, keepdims=True) + jnp.exp2(sink - m)
        ot = jnp.dot(vt_scr[g, :, band], p.astype(BF16),
                     preferred_element_type=F32) * (1.0 / denom)
        for pair in range(GQ // 2):
            two = jnp.concatenate(
                [ot[:, (2 * pair) * BLOCK:(2 * pair + 1) * BLOCK],
                 ot[:, (2 * pair + 1) * BLOCK:(2 * pair + 2) * BLOCK]], axis=0)
            st[j]["attn"][b][g * (GQ // 2) + pair] = two.T

    def gating(j):
        rows_out = []
        for n in range(R // CHUNK):
            rows = slice(n * CHUNK, (n + 1) * CHUNK)
            cols_out = []
            for g in range(N_GROUPS):
                w_g = jnp.where(causal, ws_ref[g], 0.0).astype(BF16)
                bias = jnp.broadcast_to(bs_ref[:, g:g + 1], (CHUNK, group_w))
                cols = slice(g * group_w, (g + 1) * group_w)
                sv = jnp.dot(w_g, st[j]["vn"][rows, cols], preferred_element_type=F32) + bias
                cols_out.append(st[j]["u"][rows, cols] * sv)
            rows_out.append(jnp.concatenate(cols_out, axis=1))
        st[j]["yb"] = jnp.concatenate(rows_out, axis=0)

    def branch_proj(j):
        attn = jnp.concatenate([jnp.concatenate(row, axis=1) for row in st[j]["attn"]], axis=0)
        y_a = attn * _silu_of_half(st[j]["za"])
        st[j]["pa"] = jnp.dot(y_a.astype(BF16), wpa_ref[...], preferred_element_type=F32)
        y_b = st[j]["yb"] * _silu_of_half(st[j]["zb"])
        st[j]["pb"] = jnp.dot(y_b.astype(BF16), wpb_ref[...], preferred_element_type=F32)

    def out_proj(j):
        merged2 = (_two_sigmoid_of_half(st[j]["ga"]) * st[j]["pa"]
                   + _two_sigmoid_of_half(st[j]["gb"]) * st[j]["pb"]).astype(BF16)
        y = jnp.dot(merged2, wout_ref[...], preferred_element_type=F32)
        ms_y = jnp.mean(y * y, axis=-1, keepdims=True)
        rows = slice(j * R, (j + 1) * R)
        o_ref[0, rows, :] = x_ref[0, rows, :] + (y * lax.rsqrt(ms_y + EPS)) * post_gain
        st[j].clear()

    n_units = blk_per_sub * N_KV_HEADS
    for j in range(n_sub + 1):
        if j < n_sub:
            head(j)
            gmlp_in(j)
        if j >= 1:
            out_proj(j - 1)
        if j < n_sub:
            rope_and_scores(j)
            st[j]["attn"] = [[None] * (N_Q_HEADS // 2) for _ in range(blk_per_sub)]
            st[j]["za"] = proj(j, c_za, c_u)
            st[j]["zb"] = proj(j, c_zb, c_g)
            st[j]["ga"] = proj(j, c_g, c_g + d_model)
            for idx in range(n_units):
                pv_unit(j, idx)
            gating(j)
            st[j]["gb"] = proj(j, c_g + d_model, c_g + 2 * d_model)
            branch_proj(j)

    k_scr[:, 0:BLOCK, :] = k_scr[:, T:T + BLOCK, :]
    vt_scr[:, :, 0:BLOCK] = vt_scr[:, :, T:T + BLOCK]


def _const_spec(shape):
    return pl.BlockSpec(shape, lambda b, s: (0,) * len(shape), pipeline_mode=pl.Buffered(1))


def _layer(x, ada, positions, g_pre, g_post, w_in, sinks, ln_v_g, ln_v_b, w_s, b_s,
           w_proj_a, w_proj_b, w_out):
    B, S, D = x.shape
    T = SEQ_TILE
    assert S % T == 0 and T % SUB_TILE == 0 and SUB_TILE % BLOCK == 0 and SUB_TILE % CHUNK == 0
    d_a = N_Q_HEADS * HEAD_DIM
    d_kv = N_KV_HEADS * HEAD_DIM
    d_b = w_proj_b.shape[0]
    d_in = w_in.shape[1]
    assert d_in == 2 * d_a + 2 * d_kv + 3 * d_b + 2 * D

    half = HEAD_DIM // 2
    inv_freq = ROPE_THETA ** (-jnp.arange(half, dtype=F32) / half)
    invf = jnp.broadcast_to(inv_freq[:, None], (half, LANES))

    c_za, c_u = d_a + 2 * d_kv, 2 * d_a + 2 * d_kv
    c_zb = c_u + 2 * d_b
    col = jnp.arange(d_in)
    halved = ((col >= c_za) & (col < c_u)) | (col >= c_zb)
    inv_sqrt2 = 1.0 / math.sqrt(2.0)
    col_scale = jnp.where(halved, 0.5, jnp.where((col >= c_u) & (col < c_zb), inv_sqrt2, 1.0))
    w_in_b = (w_in * col_scale.astype(F32)[None, :]).astype(BF16)
    w_out_b = (w_out * 0.5).astype(BF16)

    kern = functools.partial(_block_kernel, d_a=d_a, d_kv=d_kv, d_b=d_b, d_model=D)
    return pl.pallas_call(
        kern,
        out_shape=jax.ShapeDtypeStruct((B, S, D), x.dtype),
        grid=(B, S // T),
        in_specs=[
            pl.BlockSpec(memory_space=pltpu.SMEM),
            pl.BlockSpec((1, T, D), lambda b, s: (b, s, 0)),
            pl.BlockSpec((B, T), lambda b, s: (0, s)),
            _const_spec((B, 3 * D)),
            _const_spec((1, D)),
            _const_spec((1, D)),
            _const_spec((half, LANES)),
            _const_spec((D, d_in)),
            _const_spec((1, d_b)),
            _const_spec((1, d_b)),
            _const_spec((N_GROUPS, CHUNK, CHUNK)),
            _const_spec((CHUNK, N_GROUPS)),
            _const_spec((d_a, D)),
            _const_spec((d_b, D)),
            _const_spec((D, D)),
        ],
        out_specs=pl.BlockSpec((1, T, D), lambda b, s: (b, s, 0)),
        scratch_shapes=[
            pltpu.VMEM((N_KV_HEADS, BLOCK + T, HEAD_DIM), BF16),
            pltpu.VMEM((N_KV_HEADS, HEAD_DIM, BLOCK + T), BF16),
            pltpu.VMEM((T, D), BF16),
        ],
        compiler_params=pltpu.CompilerParams(
            dimension_semantics=("arbitrary", "arbitrary"),
            vmem_limit_bytes=VMEM_LIMIT_BYTES),
        name="hybrid_block",
    )(sinks, x, positions, ada,
      g_pre[None, :], g_post[None, :], invf, w_in_b,
      ln_v_g[None, :], ln_v_b[None, :], w_s, b_s.T,
      w_proj_a.astype(BF16), (w_proj_b * inv_sqrt2).astype(BF16), w_out_b)


def _ada(c, w_ada, b_ada):
    B, D = c.shape
    n_out = w_ada.shape[1]
    tn = D
    return pl.pallas_call(
        _ada_kernel,
        out_shape=jax.ShapeDtypeStruct((B, n_out), F32),
        grid=(n_out // tn,),
        in_specs=[
            pl.BlockSpec((B, D), lambda j: (0, 0)),
            pl.BlockSpec((D, tn), lambda j: (0, j)),
            pl.BlockSpec((1, tn), lambda j: (0, j)),
        ],
        out_specs=pl.BlockSpec((B, tn), lambda j: (0, j)),
        compiler_params=pltpu.CompilerParams(dimension_semantics=("arbitrary",)),
        name="adaln_modulation",
    )(c, w_ada, b_ada[None, :])


def kernel(x, c, positions, w_ada, b_ada, g_pre, g_post, w_in, sinks, ln_v_g, ln_v_b, w_s, b_s,
           w_proj_a, w_proj_b, w_out):
    depth = w_in.shape[0]
    for l in range(depth):
        ada = _ada(c, w_ada[l], b_ada[l])
        x = _layer(x, ada, positions, g_pre[l], g_post[l], w_in[l], sinks[l], ln_v_g[l],
                   ln_v_b[l], w_s[l], b_s[l], w_proj_a[l], w_proj_b[l], w_out[l])
    return x
```

```python
import functools
import math

import jax
import jax.numpy as jnp
from jax import lax
from jax.experimental import pallas as pl
from jax.experimental.pallas import tpu as pltpu

HEAD_DIM = 64
N_Q_HEADS = 8
N_KV_HEADS = 2
GQ = N_Q_HEADS // N_KV_HEADS
WINDOW = 128
BLOCK = 128
ROPE_THETA = 10000.0
CHUNK = 128
N_GROUPS = 4
EPS = 1e-6
NEG = -1e30
LOG2E = 1.4426950408889634

LANES = 128
SEQ_TILE = 1024
SUB_TILE = 256
VMEM_LIMIT_BYTES = 44 * 1024 * 1024

BF16 = jnp.bfloat16
F32 = jnp.float32


def _ada_kernel(c_ref, w_ref, b_ref, o_ref):
    c = c_ref[...]
    c_act = c * jax.nn.sigmoid(c)
    o_ref[...] = jnp.dot(c_act, w_ref[...], preferred_element_type=F32) + b_ref[...]


def _two_sigmoid_of_half(h):
    return jnp.tanh(h) + 1.0


def _silu_of_half(h):
    return h * _two_sigmoid_of_half(h)


def _gelu_times_sqrt2(t):
    return t * (1.0 + lax.erf(t))


def _rope(t, cos, sin_signed, first_half):
    outs = []
    for c in range(t.shape[1] // LANES):
        tc = t[:, c * LANES:(c + 1) * LANES]
        rot = jnp.where(first_half,
                        pltpu.roll(tc, LANES - HEAD_DIM // 2, axis=1),
                        pltpu.roll(tc, HEAD_DIM // 2, axis=1))
        outs.append(tc * cos + rot * sin_signed)
    return outs


def _block_kernel(sinks_ref, x_ref, pos_ref, ada_ref, gpre_ref, gpost_ref, invf_ref,
                  win_ref, lng_ref, lnb_ref, ws_ref, bs_ref, wpa_ref, wpb_ref, wout_ref,
                  o_ref, k_scr, vt_scr, *, d_a, d_kv, d_b, d_model):
    T = x_ref.shape[1]
    R = SUB_TILE
    n_sub = T // R
    blk_per_sub = R // BLOCK
    s_idx = pl.program_id(1)

    c_q, c_k, c_v = 0, d_a, d_a + d_kv
    c_za = c_v + d_kv
    c_u = c_za + d_a
    c_vb = c_u + d_b
    c_zb = c_vb + d_b
    c_g = c_zb + d_b

    @pl.when(s_idx == 0)
    def _():
        k_scr[:, 0:BLOCK, :] = jnp.zeros((N_KV_HEADS, BLOCK, HEAD_DIM), BF16)
        vt_scr[:, :, 0:BLOCK] = jnp.zeros((N_KV_HEADS, HEAD_DIM, BLOCK), BF16)

    b_idx = pl.program_id(0)
    ada = ada_ref[pl.ds(b_idx, 1), :]
    shift = ada[:, 0:d_model]
    scale = ada[:, d_model:2 * d_model]
    gate = ada[:, 2 * d_model:3 * d_model]
    pre_gain = gpre_ref[...] * (1.0 + scale)
    post_gain = gate * gpost_ref[...]
    invf = invf_ref[...]
    lane = lax.broadcasted_iota(jnp.int32, (1, LANES), 1)
    first_half = (lane % HEAD_DIM) < (HEAD_DIM // 2)

    kj = lax.broadcasted_iota(jnp.int32, (2 * BLOCK, BLOCK), 0)
    qi = lax.broadcasted_iota(jnp.int32, (2 * BLOCK, BLOCK), 1)
    rel = qi + BLOCK - kj
    in_win = (rel >= 0) & (rel < WINDOW)
    first_lo = jnp.where(s_idx == 0, BLOCK, 0)
    ti = lax.broadcasted_iota(jnp.int32, (CHUNK, CHUNK), 0)
    si = lax.broadcasted_iota(jnp.int32, (CHUNK, CHUNK), 1)
    causal = si <= ti
    group_w = d_b // N_GROUPS
    q_scale = LOG2E / math.sqrt(HEAD_DIM)

    st = [dict() for _ in range(n_sub)]

    def proj(j, lo, hi):
        return jnp.dot(st[j]["hb"], win_ref[:, lo:hi], preferred_element_type=F32)

    def head(j):
        xc = x_ref[0, j * R:(j + 1) * R, :]
        ms = jnp.mean(xc * xc, axis=-1, keepdims=True)
        st[j]["hb"] = ((xc * lax.rsqrt(ms + EPS)) * pre_gain + shift).astype(BF16)
        st[j]["qkv"] = proj(j, c_q, c_za)

    def gmlp_in(j):
        v = _gelu_times_sqrt2(proj(j, c_vb, c_zb))
        mu = jnp.mean(v, axis=-1, keepdims=True)
        vc = v - mu
        var = jnp.mean(vc * vc, axis=-1, keepdims=True)
        st[j]["vn"] = (vc * lax.rsqrt(var + 2.0 * EPS) * lng_ref[...] + lnb_ref[...]).astype(BF16)
        st[j]["u"] = _gelu_times_sqrt2(proj(j, c_u, c_vb))

    def rope_and_scores(j):
        pos = pos_ref[pl.ds(b_idx, 1), j * R:(j + 1) * R].astype(F32)
        cos_rows, sin_rows = [], []
        for b in range(blk_per_sub):
            ang = invf * pos[:, b * BLOCK:(b + 1) * BLOCK]
            cs = jnp.cos(ang)
            sn = jnp.sin(ang)
            cos_rows.append(jnp.concatenate([cs, cs, cs, cs], axis=0).T)
            sin_rows.append(jnp.concatenate([-sn, sn, -sn, sn], axis=0).T)
        cos = jnp.concatenate(cos_rows, axis=0)
        sin_signed = jnp.concatenate(sin_rows, axis=0)
        qkv = st[j]["qkv"]
        q_tiles = _rope(qkv[:, c_q:c_k], cos, sin_signed, first_half)
        k_r = _rope(qkv[:, c_k:c_v], cos, sin_signed, first_half)[0]
        vt_new = qkv[:, c_v:c_za].T
        lo = BLOCK + j * R
        for g in range(N_KV_HEADS):
            k_scr[g, lo:lo + R, :] = k_r[:, g * HEAD_DIM:(g + 1) * HEAD_DIM].astype(BF16)
            vt_scr[g, :, lo:lo + R] = vt_new[g * HEAD_DIM:(g + 1) * HEAD_DIM, :].astype(BF16)
        q_heads = []
        for c in range(len(q_tiles)):
            qs = (q_tiles[c] * q_scale).astype(BF16)
            q_heads.append(qs[:, 0:HEAD_DIM])
            q_heads.append(qs[:, HEAD_DIM:2 * HEAD_DIM])
        ss = []
        for b in range(blk_per_sub):
            n = j * blk_per_sub + b
            band = slice(n * BLOCK, n * BLOCK + 2 * BLOCK)
            for g in range(N_KV_HEADS):
                kb = k_scr[g, band, :]
                q_stack = jnp.concatenate(
                    [q_heads[g * GQ + h][b * BLOCK:(b + 1) * BLOCK] for h in range(GQ)], axis=0)
                ss.append(lax.dot_general(kb, q_stack, (((1,), (1,)), ((), ())),
                                          preferred_element_type=F32))
        st[j]["scores"] = ss

    def pv_unit(j, idx):
        b, g = divmod(idx, N_KV_HEADS)
        n = j * blk_per_sub + b
        band = slice(n * BLOCK, n * BLOCK + 2 * BLOCK)
        s = st[j]["scores"][idx]
        valid = in_win & (kj >= first_lo) if n == 0 else in_win
        s = jnp.where(jnp.concatenate([valid] * GQ, axis=1), s, NEG)
        sink = jnp.concatenate(
            [jnp.full((1, BLOCK), sinks_ref[g * GQ + h] * LOG2E, F32) for h in range(GQ)], axis=1)
        m = jnp.maximum(jnp.max(s, axis=0, keepdims=True), sink)
        p = jnp.exp2(s - m)
        denom = jnp.sum(p, axis=0, keepdims=True) + jnp.exp2(sink - m)
        ot = jnp.dot(vt_scr[g, :, band], p.astype(BF16),
                     preferred_element_type=F32) * (1.0 / denom)
        for pair in range(GQ // 2):
            two = jnp.concatenate(
                [ot[:, (2 * pair) * BLOCK:(2 * pair + 1) * BLOCK],
                 ot[:, (2 * pair + 1) * BLOCK:(2 * pair + 2) * BLOCK]], axis=0)
            st[j]["attn"][b][g * (GQ // 2) + pair] = two.T

    def gating(j):
        rows_out = []
        for n in range(R // CHUNK):
            rows = slice(n * CHUNK, (n + 1) * CHUNK)
            cols_out = []
            for g in range(N_GROUPS):
                w_g = jnp.where(causal, ws_ref[g], 0.0).astype(BF16)
                bias = jnp.broadcast_to(bs_ref[:, g:g + 1], (CHUNK, group_w))
                cols = slice(g * group_w, (g + 1) * group_w)
                sv = jnp.dot(w_g, st[j]["vn"][rows, cols], preferred_element_type=F32) + bias
                cols_out.append(st[j]["u"][rows, cols] * sv)
            rows_out.append(jnp.concatenate(cols_out, axis=1))
        st[j]["yb"] = jnp.concatenate(rows_out, axis=0)

    def branch_proj(j):
        attn = jnp.concatenate([jnp.concatenate(row, axis=1) for row in st[j]["attn"]], axis=0)
        y_a = attn * _silu_of_half(st[j]["za"])
        st[j]["pa"] = jnp.dot(y_a.astype(BF16), wpa_ref[...], preferred_element_type=F32)
        y_b = st[j]["yb"] * _silu_of_half(st[j]["zb"])
        st[j]["pb"] = jnp.dot(y_b.astype(BF16), wpb_ref[...], preferred_element_type=F32)

    def out_proj(j):
        merged2 = (_two_sigmoid_of_half(st[j]["ga"]) * st[j]["pa"]
                   + _two_sigmoid_of_half(st[j]["gb"]) * st[j]["pb"]).astype(BF16)
        y = jnp.dot(merged2, wout_ref[...], preferred_element_type=F32)
        ms_y = jnp.mean(y * y, axis=-1, keepdims=True)
        rows = slice(j * R, (j + 1) * R)
        o_ref[0, rows, :] = x_ref[0, rows, :] + (y * lax.rsqrt(ms_y + EPS)) * post_gain
        st[j].clear()

    n_units = blk_per_sub * N_KV_HEADS
    for j in range(n_sub + 1):
        if j < n_sub:
            head(j)
            gmlp_in(j)
        if j >= 1:
            out_proj(j - 1)
        if j < n_sub:
            rope_and_scores(j)
            st[j]["attn"] = [[None] * (N_Q_HEADS // 2) for _ in range(blk_per_sub)]
            st[j]["za"] = proj(j, c_za, c_u)
            st[j]["zb"] = proj(j, c_zb, c_g)
            pv_unit(j, 0)
            ga0 = proj(j, c_g, c_g + d_model // 2)
            pv_unit(j, 1)
            pv_unit(j, 2)
            ga1 = proj(j, c_g + d_model // 2, c_g + d_model)
            st[j]["ga"] = jnp.concatenate([ga0, ga1], axis=1)
            pv_unit(j, 3)
            gating(j)
            st[j]["gb"] = proj(j, c_g + d_model, c_g + 2 * d_model)
            branch_proj(j)

    k_scr[:, 0:BLOCK, :] = k_scr[:, T:T + BLOCK, :]
    vt_scr[:, :, 0:BLOCK] = vt_scr[:, :, T:T + BLOCK]


def _const_spec(shape):
    return pl.BlockSpec(shape, lambda b, s: (0,) * len(shape), pipeline_mode=pl.Buffered(1))


def _layer(x, ada, positions, g_pre, g_post, w_in, sinks, ln_v_g, ln_v_b, w_s, b_s,
           w_proj_a, w_proj_b, w_out):
    B, S, D = x.shape
    T = SEQ_TILE
    assert S % T == 0 and T % SUB_TILE == 0 and SUB_TILE % BLOCK == 0 and SUB_TILE % CHUNK == 0
    d_a = N_Q_HEADS * HEAD_DIM
    d_kv = N_KV_HEADS * HEAD_DIM
    d_b = w_proj_b.shape[0]
    d_in = w_in.shape[1]
    assert d_in == 2 * d_a + 2 * d_kv + 3 * d_b + 2 * D

    half = HEAD_DIM // 2
    inv_freq = ROPE_THETA ** (-jnp.arange(half, dtype=F32) / half)
    invf = jnp.broadcast_to(inv_freq[:, None], (half, LANES))

    c_za, c_u = d_a + 2 * d_kv, 2 * d_a + 2 * d_kv
    c_zb = c_u + 2 * d_b
    col = jnp.arange(d_in)
    halved = ((col >= c_za) & (col < c_u)) | (col >= c_zb)
    inv_sqrt2 = 1.0 / math.sqrt(2.0)
    col_scale = jnp.where(halved, 0.5, jnp.where((col >= c_u) & (col < c_zb), inv_sqrt2, 1.0))
    w_in_b = (w_in * col_scale.astype(F32)[None, :]).astype(BF16)
    w_out_b = (w_out * 0.5).astype(BF16)

    kern = functools.partial(_block_kernel, d_a=d_a, d_kv=d_kv, d_b=d_b, d_model=D)
    return pl.pallas_call(
        kern,
        out_shape=jax.ShapeDtypeStruct((B, S, D), x.dtype),
        grid=(B, S // T),
        in_specs=[
            pl.BlockSpec(memory_space=pltpu.SMEM),
            pl.BlockSpec((1, T, D), lambda b, s: (b, s, 0)),
            pl.BlockSpec((B, T), lambda b, s: (0, s)),
            _const_spec((B, 3 * D)),
            _const_spec((1, D)),
            _const_spec((1, D)),
            _const_spec((half, LANES)),
            _const_spec((D, d_in)),
            _const_spec((1, d_b)),
            _const_spec((1, d_b)),
            _const_spec((N_GROUPS, CHUNK, CHUNK)),
            _const_spec((CHUNK, N_GROUPS)),
            _const_spec((d_a, D)),
            _const_spec((d_b, D)),
            _const_spec((D, D)),
        ],
        out_specs=pl.BlockSpec((1, T, D), lambda b, s: (b, s, 0)),
        scratch_shapes=[
            pltpu.VMEM((N_KV_HEADS, BLOCK + T, HEAD_DIM), BF16),
            pltpu.VMEM((N_KV_HEADS, HEAD_DIM, BLOCK + T), BF16),
        ],
        compiler_params=pltpu.CompilerParams(
            dimension_semantics=("arbitrary", "arbitrary"),
            vmem_limit_bytes=VMEM_LIMIT_BYTES),
        name="hybrid_block",
    )(sinks, x, positions, ada,
      g_pre[None, :], g_post[None, :], invf, w_in_b,
      ln_v_g[None, :], ln_v_b[None, :], w_s, b_s.T,
      w_proj_a.astype(BF16), (w_proj_b * inv_sqrt2).astype(BF16), w_out_b)


def _ada(c, w_ada, b_ada):
    B, D = c.shape
    n_out = w_ada.shape[1]
    tn = D
    return pl.pallas_call(
        _ada_kernel,
        out_shape=jax.ShapeDtypeStruct((B, n_out), F32),
        grid=(n_out // tn,),
        in_specs=[
            pl.BlockSpec((B, D), lambda j: (0, 0)),
            pl.BlockSpec((D, tn), lambda j: (0, j)),
            pl.BlockSpec((1, tn), lambda j: (0, j)),
        ],
        out_specs=pl.BlockSpec((B, tn), lambda j: (0, j)),
        compiler_params=pltpu.CompilerParams(dimension_semantics=("arbitrary",)),
        name="adaln_modulation",
    )(c, w_ada, b_ada[None, :])


def kernel(x, c, positions, w_ada, b_ada, g_pre, g_post, w_in, sinks, ln_v_g, ln_v_b, w_s, b_s,
           w_proj_a, w_proj_b, w_out):
    depth = w_in.shape[0]
    for l in range(depth):
        ada = _ada(c, w_ada[l], b_ada[l])
        x = _layer(x, ada, positions, g_pre[l], g_post[l], w_in[l], sinks[l], ln_v_g[l],
                   ln_v_b[l], w_s[l], b_s[l], w_proj_a[l], w_proj_b[l], w_out[l])
    return x
```

```python
import functools
import math

import jax
import jax.numpy as jnp
from jax import lax
from jax.experimental import pallas as pl
from jax.experimental.pallas import tpu as pltpu

HEAD_DIM = 64
N_Q_HEADS = 8
N_KV_HEADS = 2
GQ = N_Q_HEADS // N_KV_HEADS
WINDOW = 128
BLOCK = 128
ROPE_THETA = 10000.0
CHUNK = 128
N_GROUPS = 4
EPS = 1e-6
NEG = -1e30
LOG2E = 1.4426950408889634

LANES = 128
SEQ_TILE = 1024
SUB_TILE = 256
STAGE_ROWS = 64
STAGE_SLOTS = 4
VMEM_LIMIT_BYTES = 50 * 1024 * 1024

BF16 = jnp.bfloat16
F32 = jnp.float32


def _ada_kernel(c_ref, w_ref, b_ref, o_ref):
    c = c_ref[...]
    c_act = c * jax.nn.sigmoid(c)
    o_ref[...] = jnp.dot(c_act, w_ref[...], preferred_element_type=F32) + b_ref[...]


def _two_sigmoid_of_half(h):
    return jnp.tanh(h) + 1.0


def _silu_of_half(h):
    return h * _two_sigmoid_of_half(h)


def _gelu_times_sqrt2(t):
    return t * (1.0 + lax.erf(t))


def _rope(t, cos, sin_signed, first_half):
    outs = []
    for c in range(t.shape[1] // LANES):
        tc = t[:, c * LANES:(c + 1) * LANES]
        rot = jnp.where(first_half,
                        pltpu.roll(tc, LANES - HEAD_DIM // 2, axis=1),
                        pltpu.roll(tc, HEAD_DIM // 2, axis=1))
        outs.append(tc * cos + rot * sin_signed)
    return outs


def _stage_weight(w_hbm, dst, stage, sem, col_groups):
    k, n = w_hbm.shape
    n_chunks = k // STAGE_ROWS

    def copy(c):
        slot = c % STAGE_SLOTS
        return pltpu.make_async_copy(w_hbm.at[pl.ds(c * STAGE_ROWS, STAGE_ROWS), :],
                                     stage.at[slot, :, pl.ds(0, n)], sem.at[slot])

    ahead = STAGE_SLOTS - 1
    for c in range(min(ahead, n_chunks)):
        copy(c).start()
    for c in range(n_chunks):
        if c + ahead < n_chunks:
            copy(c + ahead).start()
        copy(c).wait()
        rows = slice(c * STAGE_ROWS, (c + 1) * STAGE_ROWS)
        for lo, hi, scale in col_groups:
            dst[rows, lo:hi] = (stage[c % STAGE_SLOTS, :, lo:hi] * scale).astype(BF16)


def _block_kernel(sinks_ref, x_ref, pos_ref, ada_ref, gpre_ref, gpost_ref, invf_ref,
                  win_hbm, lng_ref, lnb_ref, ws_ref, bs_ref, wpa_hbm, wpb_hbm, wout_hbm,
                  o_ref, k_scr, vt_scr, win_ref, wpa_ref, wpb_ref, wout_ref, stage, stage_sem,
                  *, d_a, d_kv, d_b, d_model):
    T = x_ref.shape[1]
    R = SUB_TILE
    n_sub = T // R
    blk_per_sub = R // BLOCK
    s_idx = pl.program_id(1)

    c_q, c_k, c_v = 0, d_a, d_a + d_kv
    c_za = c_v + d_kv
    c_u = c_za + d_a
    c_vb = c_u + d_b
    c_zb = c_vb + d_b
    c_g = c_zb + d_b

    inv_sqrt2 = 1.0 / math.sqrt(2.0)

    w_in_groups = [(c_q, c_za, 1.0), (c_za, c_u, 0.5), (c_u, c_zb, inv_sqrt2),
                   (c_zb, c_g + 2 * d_model, 0.5)]

    @pl.when((pl.program_id(0) == 0) & (s_idx == 0))
    def _():
        _stage_weight(win_hbm, win_ref, stage, stage_sem, w_in_groups)
        _stage_weight(wpa_hbm, wpa_ref, stage, stage_sem, [(0, d_model, 1.0)])
        _stage_weight(wpb_hbm, wpb_ref, stage, stage_sem, [(0, d_model, inv_sqrt2)])
        _stage_weight(wout_hbm, wout_ref, stage, stage_sem, [(0, d_model, 0.5)])

    @pl.when(s_idx == 0)
    def _():
        k_scr[:, 0:BLOCK, :] = jnp.zeros((N_KV_HEADS, BLOCK, HEAD_DIM), BF16)
        vt_scr[:, :, 0:BLOCK] = jnp.zeros((N_KV_HEADS, HEAD_DIM, BLOCK), BF16)

    b_idx = pl.program_id(0)
    ada = ada_ref[pl.ds(b_idx, 1), :]
    shift = ada[:, 0:d_model]
    scale = ada[:, d_model:2 * d_model]
    gate = ada[:, 2 * d_model:3 * d_model]
    pre_gain = gpre_ref[...] * (1.0 + scale)
    post_gain = gate * gpost_ref[...]
    invf = invf_ref[...]
    lane = lax.broadcasted_iota(jnp.int32, (1, LANES), 1)
    first_half = (lane % HEAD_DIM) < (HEAD_DIM // 2)

    kj = lax.broadcasted_iota(jnp.int32, (2 * BLOCK, BLOCK), 0)
    qi = lax.broadcasted_iota(jnp.int32, (2 * BLOCK, BLOCK), 1)
    rel = qi + BLOCK - kj
    in_win = (rel >= 0) & (rel < WINDOW)
    first_lo = jnp.where(s_idx == 0, BLOCK, 0)
    ti = lax.broadcasted_iota(jnp.int32, (CHUNK, CHUNK), 0)
    si = lax.broadcasted_iota(jnp.int32, (CHUNK, CHUNK), 1)
    causal = si <= ti
    group_w = d_b // N_GROUPS
    q_scale = LOG2E / math.sqrt(HEAD_DIM)

    st = [dict() for _ in range(n_sub)]

    def proj(j, lo, hi):
        return jnp.dot(st[j]["hb"], win_ref[:, lo:hi], preferred_element_type=F32)

    def head(j):
        xc = x_ref[0, j * R:(j + 1) * R, :]
        ms = jnp.mean(xc * xc, axis=-1, keepdims=True)
        st[j]["hb"] = ((xc * lax.rsqrt(ms + EPS)) * pre_gain + shift).astype(BF16)
        st[j]["qkv"] = proj(j, c_q, c_za)

    def gmlp_in(j):
        v = _gelu_times_sqrt2(proj(j, c_vb, c_zb))
        mu = jnp.mean(v, axis=-1, keepdims=True)
        vc = v - mu
        var = jnp.mean(vc * vc, axis=-1, keepdims=True)
        st[j]["vn"] = (vc * lax.rsqrt(var + 2.0 * EPS) * lng_ref[...] + lnb_ref[...]).astype(BF16)
        st[j]["u"] = _gelu_times_sqrt2(proj(j, c_u, c_vb))

    def rope_and_scores(j):
        pos = pos_ref[pl.ds(b_idx, 1), j * R:(j + 1) * R].astype(F32)
        cos_rows, sin_rows = [], []
        for b in range(blk_per_sub):
            ang = invf * pos[:, b * BLOCK:(b + 1) * BLOCK]
            cs = jnp.cos(ang)
            sn = jnp.sin(ang)
            cos_rows.append(jnp.concatenate([cs, cs, cs, cs], axis=0).T)
            sin_rows.append(jnp.concatenate([-sn, sn, -sn, sn], axis=0).T)
        cos = jnp.concatenate(cos_rows, axis=0)
        sin_signed = jnp.concatenate(sin_rows, axis=0)
        qkv = st[j]["qkv"]
        q_tiles = _rope(qkv[:, c_q:c_k], cos, sin_signed, first_half)
        k_r = _rope(qkv[:, c_k:c_v], cos, sin_signed, first_half)[0]
        vt_new = qkv[:, c_v:c_za].T
        lo = BLOCK + j * R
        for g in range(N_KV_HEADS):
            k_scr[g, lo:lo + R, :] = k_r[:, g * HEAD_DIM:(g + 1) * HEAD_DIM].astype(BF16)
            vt_scr[g, :, lo:lo + R] = vt_new[g * HEAD_DIM:(g + 1) * HEAD_DIM, :].astype(BF16)
        q_heads = []
        for c in range(len(q_tiles)):
            qs = (q_tiles[c] * q_scale).astype(BF16)
            q_heads.append(qs[:, 0:HEAD_DIM])
            q_heads.append(qs[:, HEAD_DIM:2 * HEAD_DIM])
        ss = []
        for b in range(blk_per_sub):
            n = j * blk_per_sub + b
            band = slice(n * BLOCK, n * BLOCK + 2 * BLOCK)
            for g in range(N_KV_HEADS):
                kb = k_scr[g, band, :]
                q_stack = jnp.concatenate(
                    [q_heads[g * GQ + h][b * BLOCK:(b + 1) * BLOCK] for h in range(GQ)], axis=0)
                ss.append(lax.dot_general(kb, q_stack, (((1,), (1,)), ((), ())),
                                          preferred_element_type=F32))
        st[j]["scores"] = ss

    def pv_unit(j, idx):
        b, g = divmod(idx, N_KV_HEADS)
        n = j * blk_per_sub + b
        band = slice(n * BLOCK, n * BLOCK + 2 * BLOCK)
        s = st[j]["scores"][idx]
        valid = in_win & (kj >= first_lo) if n == 0 else in_win
        s = jnp.where(jnp.concatenate([valid] * GQ, axis=1), s, NEG)
        sink = jnp.concatenate(
            [jnp.full((1, BLOCK), sinks_ref[g * GQ + h] * LOG2E, F32) for h in range(GQ)], axis=1)
        m = jnp.maximum(jnp.max(s, axis=0, keepdims=True), sink)
        p = jnp.exp2(s - m)
        denom = jnp.sum(p, axis=0, keepdims=True) + jnp.exp2(sink - m)
        ot = jnp.dot(vt_scr[g, :, band], p.astype(BF16),
                     preferred_element_type=F32) * (1.0 / denom)
        for pair in range(GQ // 2):
            two = jnp.concatenate(
                [ot[:, (2 * pair) * BLOCK:(2 * pair + 1) * BLOCK],
                 ot[:, (2 * pair + 1) * BLOCK:(2 * pair + 2) * BLOCK]], axis=0)
            st[j]["attn"][b][g * (GQ // 2) + pair] = two.T

    def gating(j):
        rows_out = []
        for n in range(R // CHUNK):
            rows = slice(n * CHUNK, (n + 1) * CHUNK)
            cols_out = []
            for g in range(N_GROUPS):
                w_g = jnp.where(causal, ws_ref[g], 0.0).astype(BF16)
                bias = jnp.broadcast_to(bs_ref[:, g:g + 1], (CHUNK, group_w))
                cols = slice(g * group_w, (g + 1) * group_w)
                sv = jnp.dot(w_g, st[j]["vn"][rows, cols], preferred_element_type=F32) + bias
                cols_out.append(st[j]["u"][rows, cols] * sv)
            rows_out.append(jnp.concatenate(cols_out, axis=1))
        st[j]["yb"] = jnp.concatenate(rows_out, axis=0)

    def branch_proj(j):
        attn = jnp.concatenate([jnp.concatenate(row, axis=1) for row in st[j]["attn"]], axis=0)
        y_a = attn * _silu_of_half(st[j]["za"])
        st[j]["pa"] = jnp.dot(y_a.astype(BF16), wpa_ref[...], preferred_element_type=F32)
        y_b = st[j]["yb"] * _silu_of_half(st[j]["zb"])
        st[j]["pb"] = jnp.dot(y_b.astype(BF16), wpb_ref[...], preferred_element_type=F32)

    def out_proj(j):
        merged2 = (_two_sigmoid_of_half(st[j]["ga"]) * st[j]["pa"]
                   + _two_sigmoid_of_half(st[j]["gb"]) * st[j]["pb"]).astype(BF16)
        y = jnp.dot(merged2, wout_ref[...], preferred_element_type=F32)
        ms_y = jnp.mean(y * y, axis=-1, keepdims=True)
        rows = slice(j * R, (j + 1) * R)
        o_ref[0, rows, :] = x_ref[0, rows, :] + (y * lax.rsqrt(ms_y + EPS)) * post_gain
        st[j].clear()

    n_units = blk_per_sub * N_KV_HEADS
    for j in range(n_sub + 1):
        if j < n_sub:
            head(j)
            gmlp_in(j)
        if j >= 1:
            out_proj(j - 1)
        if j < n_sub:
            rope_and_scores(j)
            st[j]["attn"] = [[None] * (N_Q_HEADS // 2) for _ in range(blk_per_sub)]
            st[j]["za"] = proj(j, c_za, c_u)
            st[j]["zb"] = proj(j, c_zb, c_g)
            st[j]["ga"] = proj(j, c_g, c_g + d_model)
            for idx in range(n_units):
                pv_unit(j, idx)
            gating(j)
            st[j]["gb"] = proj(j, c_g + d_model, c_g + 2 * d_model)
            branch_proj(j)

    k_scr[:, 0:BLOCK, :] = k_scr[:, T:T + BLOCK, :]
    vt_scr[:, :, 0:BLOCK] = vt_scr[:, :, T:T + BLOCK]


def _const_spec(shape):
    return pl.BlockSpec(shape, lambda b, s: (0,) * len(shape), pipeline_mode=pl.Buffered(1))


def _layer(x, ada, positions, g_pre, g_post, w_in, sinks, ln_v_g, ln_v_b, w_s, b_s,
           w_proj_a, w_proj_b, w_out):
    B, S, D = x.shape
    T = SEQ_TILE
    assert S % T == 0 and T % SUB_TILE == 0 and SUB_TILE % BLOCK == 0 and SUB_TILE % CHUNK == 0
    d_a = N_Q_HEADS * HEAD_DIM
    d_kv = N_KV_HEADS * HEAD_DIM
    d_b = w_proj_b.shape[0]
    d_in = w_in.shape[1]
    assert d_in == 2 * d_a + 2 * d_kv + 3 * d_b + 2 * D

    half = HEAD_DIM // 2
    inv_freq = ROPE_THETA ** (-jnp.arange(half, dtype=F32) / half)
    invf = jnp.broadcast_to(inv_freq[:, None], (half, LANES))

    assert D % STAGE_ROWS == 0 and d_a % STAGE_ROWS == 0 and d_b % STAGE_ROWS == 0
    hbm = pl.BlockSpec(memory_space=pl.ANY)

    kern = functools.partial(_block_kernel, d_a=d_a, d_kv=d_kv, d_b=d_b, d_model=D)
    return pl.pallas_call(
        kern,
        out_shape=jax.ShapeDtypeStruct((B, S, D), x.dtype),
        grid=(B, S // T),
        in_specs=[
            pl.BlockSpec(memory_space=pltpu.SMEM),
            pl.BlockSpec((1, T, D), lambda b, s: (b, s, 0)),
            pl.BlockSpec((B, T), lambda b, s: (0, s)),
            _const_spec((B, 3 * D)),
            _const_spec((1, D)),
            _const_spec((1, D)),
            _const_spec((half, LANES)),
            hbm,
            _const_spec((1, d_b)),
            _const_spec((1, d_b)),
            _const_spec((N_GROUPS, CHUNK, CHUNK)),
            _const_spec((CHUNK, N_GROUPS)),
            hbm,
            hbm,
            hbm,
        ],
        out_specs=pl.BlockSpec((1, T, D), lambda b, s: (b, s, 0)),
        scratch_shapes=[
            pltpu.VMEM((N_KV_HEADS, BLOCK + T, HEAD_DIM), BF16),
            pltpu.VMEM((N_KV_HEADS, HEAD_DIM, BLOCK + T), BF16),
            pltpu.VMEM((D, d_in), BF16),
            pltpu.VMEM((d_a, D), BF16),
            pltpu.VMEM((d_b, D), BF16),
            pltpu.VMEM((D, D), BF16),
            pltpu.VMEM((STAGE_SLOTS, STAGE_ROWS, d_in), F32),
            pltpu.SemaphoreType.DMA((STAGE_SLOTS,)),
        ],
        compiler_params=pltpu.CompilerParams(
            dimension_semantics=("arbitrary", "arbitrary"),
            vmem_limit_bytes=VMEM_LIMIT_BYTES),
        name="hybrid_block",
    )(sinks, x, positions, ada,
      g_pre[None, :], g_post[None, :], invf, w_in,
      ln_v_g[None, :], ln_v_b[None, :], w_s, b_s.T, w_proj_a, w_proj_b, w_out)


def _ada(c, w_ada, b_ada):
    B, D = c.shape
    n_out = w_ada.shape[1]
    tn = D
    return pl.pallas_call(
        _ada_kernel,
        out_shape=jax.ShapeDtypeStruct((B, n_out), F32),
        grid=(n_out // tn,),
        in_specs=[
            pl.BlockSpec((B, D), lambda j: (0, 0)),
            pl.BlockSpec((D, tn), lambda j: (0, j)),
            pl.BlockSpec((1, tn), lambda j: (0, j)),
        ],
        out_specs=pl.BlockSpec((B, tn), lambda j: (0, j)),
        compiler_params=pltpu.CompilerParams(dimension_semantics=("arbitrary",)),
        name="adaln_modulation",
    )(c, w_ada, b_ada[None, :])


def kernel(x, c, positions, w_ada, b_ada, g_pre, g_post, w_in, sinks, ln_v_g, ln_v_b, w_s, b_s,
           w_proj_a, w_proj_b, w_out):
    depth = w_in.shape[0]
    for l in range(depth):
        ada = _ada(c, w_ada[l], b_ada[l])
        x = _layer(x, ada, positions, g_pre[l], g_post[l], w_in[l], sinks[l], ln_v_g[l],
                   ln_v_b[l], w_s[l], b_s[l], w_proj_a[l], w_proj_b[l], w_out[l])
    return x
```

```python
import functools
import math

import jax
import jax.numpy as jnp
from jax import lax
from jax.experimental import pallas as pl
from jax.experimental.pallas import tpu as pltpu

HEAD_DIM = 64
N_Q_HEADS = 8
N_KV_HEADS = 2
GQ = N_Q_HEADS // N_KV_HEADS
WINDOW = 128
BLOCK = 128
ROPE_THETA = 10000.0
CHUNK = 128
N_GROUPS = 4
EPS = 1e-6
NEG = -1e30
LOG2E = 1.4426950408889634

LANES = 128
SEQ_TILE = 1024
SUB_TILE = 256
VMEM_LIMIT_BYTES = 44 * 1024 * 1024

BF16 = jnp.bfloat16
F32 = jnp.float32


def _ada_kernel(c_ref, w_ref, b_ref, o_ref):
    c = c_ref[...]
    c_act = c * jax.nn.sigmoid(c)
    o_ref[...] = jnp.dot(c_act, w_ref[...], preferred_element_type=F32) + b_ref[...]


def _two_sigmoid_of_half(h):
    return jnp.tanh(h) + 1.0


def _silu_of_half(h):
    return h * _two_sigmoid_of_half(h)


def _gelu_times_sqrt2(t):
    return t * (1.0 + lax.erf(t))


def _rope(t, cos, sin_signed, first_half):
    outs = []
    for c in range(t.shape[1] // LANES):
        tc = t[:, c * LANES:(c + 1) * LANES]
        rot = jnp.where(first_half,
                        pltpu.roll(tc, LANES - HEAD_DIM // 2, axis=1),
                        pltpu.roll(tc, HEAD_DIM // 2, axis=1))
        outs.append(tc * cos + rot * sin_signed)
    return outs


def _block_kernel(sinks_ref, x_ref, pos_ref, ada_ref, gpre_ref, gpost_ref, invf_ref,
                  win_hbm, lng_ref, lnb_ref, ws_ref, bs_ref, wpa_hbm, wpb_hbm, wout_hbm,
                  o_ref, k_scr, vt_scr, win_ref, wpa_ref, wpb_ref, wout_ref, load_sem,
                  *, d_a, d_kv, d_b, d_model):
    T = x_ref.shape[1]
    R = SUB_TILE
    n_sub = T // R
    blk_per_sub = R // BLOCK
    s_idx = pl.program_id(1)

    c_q, c_k, c_v = 0, d_a, d_a + d_kv
    c_za = c_v + d_kv
    c_u = c_za + d_a
    c_vb = c_u + d_b
    c_zb = c_vb + d_b
    c_g = c_zb + d_b

    @pl.when((pl.program_id(0) == 0) & (s_idx == 0))
    def _():
        pairs = ((win_hbm, win_ref), (wpa_hbm, wpa_ref), (wpb_hbm, wpb_ref), (wout_hbm, wout_ref))
        copies = [pltpu.make_async_copy(src, dst, load_sem.at[i])
                  for i, (src, dst) in enumerate(pairs)]
        for cp in copies:
            cp.start()
        for cp in copies:
            cp.wait()

    @pl.when(s_idx == 0)
    def _():
        k_scr[:, 0:BLOCK, :] = jnp.zeros((N_KV_HEADS, BLOCK, HEAD_DIM), BF16)
        vt_scr[:, :, 0:BLOCK] = jnp.zeros((N_KV_HEADS, HEAD_DIM, BLOCK), BF16)

    b_idx = pl.program_id(0)
    ada = ada_ref[pl.ds(b_idx, 1), :]
    shift = ada[:, 0:d_model]
    scale = ada[:, d_model:2 * d_model]
    gate = ada[:, 2 * d_model:3 * d_model]
    pre_gain = gpre_ref[...] * (1.0 + scale)
    post_gain = gate * gpost_ref[...]
    invf = invf_ref[...]
    lane = lax.broadcasted_iota(jnp.int32, (1, LANES), 1)
    first_half = (lane % HEAD_DIM) < (HEAD_DIM // 2)

    kj = lax.broadcasted_iota(jnp.int32, (2 * BLOCK, BLOCK), 0)
    qi = lax.broadcasted_iota(jnp.int32, (2 * BLOCK, BLOCK), 1)
    rel = qi + BLOCK - kj
    in_win = (rel >= 0) & (rel < WINDOW)
    first_lo = jnp.where(s_idx == 0, BLOCK, 0)
    ti = lax.broadcasted_iota(jnp.int32, (CHUNK, CHUNK), 0)
    si = lax.broadcasted_iota(jnp.int32, (CHUNK, CHUNK), 1)
    causal = si <= ti
    group_w = d_b // N_GROUPS
    q_scale = LOG2E / math.sqrt(HEAD_DIM)

    st = [dict() for _ in range(n_sub)]

    def proj(j, lo, hi):
        return jnp.dot(st[j]["hb"], win_ref[:, lo:hi], preferred_element_type=F32)

    def head(j):
        xc = x_ref[0, j * R:(j + 1) * R, :]
        ms = jnp.mean(xc * xc, axis=-1, keepdims=True)
        st[j]["hb"] = ((xc * lax.rsqrt(ms + EPS)) * pre_gain + shift).astype(BF16)
        st[j]["qkv"] = proj(j, c_q, c_za)

    def gmlp_in(j):
        v = _gelu_times_sqrt2(proj(j, c_vb, c_zb))
        mu = jnp.mean(v, axis=-1, keepdims=True)
        vc = v - mu
        var = jnp.mean(vc * vc, axis=-1, keepdims=True)
        st[j]["vn"] = (vc * lax.rsqrt(var + 2.0 * EPS) * lng_ref[...] + lnb_ref[...]).astype(BF16)
        st[j]["u"] = _gelu_times_sqrt2(proj(j, c_u, c_vb))

    def rope_and_scores(j):
        pos = pos_ref[pl.ds(b_idx, 1), j * R:(j + 1) * R].astype(F32)
        cos_rows, sin_rows = [], []
        for b in range(blk_per_sub):
            ang = invf * pos[:, b * BLOCK:(b + 1) * BLOCK]
            cs = jnp.cos(ang)
            sn = jnp.sin(ang)
            cos_rows.append(jnp.concatenate([cs, cs, cs, cs], axis=0).T)
            sin_rows.append(jnp.concatenate([-sn, sn, -sn, sn], axis=0).T)
        cos = jnp.concatenate(cos_rows, axis=0)
        sin_signed = jnp.concatenate(sin_rows, axis=0)
        qkv = st[j]["qkv"]
        q_tiles = _rope(qkv[:, c_q:c_k], cos, sin_signed, first_half)
        k_r = _rope(qkv[:, c_k:c_v], cos, sin_signed, first_half)[0]
        vt_new = qkv[:, c_v:c_za].T
        lo = BLOCK + j * R
        for g in range(N_KV_HEADS):
            k_scr[g, lo:lo + R, :] = k_r[:, g * HEAD_DIM:(g + 1) * HEAD_DIM].astype(BF16)
            vt_scr[g, :, lo:lo + R] = vt_new[g * HEAD_DIM:(g + 1) * HEAD_DIM, :].astype(BF16)
        q_heads = []
        for c in range(len(q_tiles)):
            qs = (q_tiles[c] * q_scale).astype(BF16)
            q_heads.append(qs[:, 0:HEAD_DIM])
            q_heads.append(qs[:, HEAD_DIM:2 * HEAD_DIM])
        ss = []
        for b in range(blk_per_sub):
            n = j * blk_per_sub + b
            band = slice(n * BLOCK, n * BLOCK + 2 * BLOCK)
            for g in range(N_KV_HEADS):
                kb = k_scr[g, band, :]
                q_stack = jnp.concatenate(
                    [q_heads[g * GQ + h][b * BLOCK:(b + 1) * BLOCK] for h in range(GQ)], axis=0)
                ss.append(lax.dot_general(kb, q_stack, (((1,), (1,)), ((), ())),
                                          preferred_element_type=F32))
        st[j]["scores"] = ss

    def pv_unit(j, idx):
        b, g = divmod(idx, N_KV_HEADS)
        n = j * blk_per_sub + b
        band = slice(n * BLOCK, n * BLOCK + 2 * BLOCK)
        s = st[j]["scores"][idx]
        valid = in_win & (kj >= first_lo) if n == 0 else in_win
        s = jnp.where(jnp.concatenate([valid] * GQ, axis=1), s, NEG)
        sink = jnp.concatenate(
            [jnp.full((1, BLOCK), sinks_ref[g * GQ + h] * LOG2E, F32) for h in range(GQ)], axis=1)
        m = jnp.maximum(jnp.max(s, axis=0, keepdims=True), sink)
        p = jnp.exp2(s - m)
        denom = jnp.sum(p, axis=0, keepdims=True) + jnp.exp2(sink - m)
        ot = jnp.dot(vt_scr[g, :, band], p.astype(BF16),
                     preferred_element_type=F32) * (1.0 / denom)
        for pair in range(GQ // 2):
            two = jnp.concatenate(
                [ot[:, (2 * pair) * BLOCK:(2 * pair + 1) * BLOCK],
                 ot[:, (2 * pair + 1) * BLOCK:(2 * pair + 2) * BLOCK]], axis=0)
            st[j]["attn"][b][g * (GQ // 2) + pair] = two.T

    def gating(j):
        rows_out = []
        for n in range(R // CHUNK):
            rows = slice(n * CHUNK, (n + 1) * CHUNK)
            cols_out = []
            for g in range(N_GROUPS):
                w_g = jnp.where(causal, ws_ref[g], 0.0).astype(BF16)
                bias = jnp.broadcast_to(bs_ref[:, g:g + 1], (CHUNK, group_w))
                cols = slice(g * group_w, (g + 1) * group_w)
                sv = jnp.dot(w_g, st[j]["vn"][rows, cols], preferred_element_type=F32) + bias
                cols_out.append(st[j]["u"][rows, cols] * sv)
            rows_out.append(jnp.concatenate(cols_out, axis=1))
        st[j]["yb"] = jnp.concatenate(rows_out, axis=0)

    def branch_proj(j):
        attn = jnp.concatenate([jnp.concatenate(row, axis=1) for row in st[j]["attn"]], axis=0)
        y_a = attn * _silu_of_half(st[j]["za"])
        st[j]["pa"] = jnp.dot(y_a.astype(BF16), wpa_ref[...], preferred_element_type=F32)
        y_b = st[j]["yb"] * _silu_of_half(st[j]["zb"])
        st[j]["pb"] = jnp.dot(y_b.astype(BF16), wpb_ref[...], preferred_element_type=F32)

    def out_proj(j):
        merged2 = (_two_sigmoid_of_half(st[j]["ga"]) * st[j]["pa"]
                   + _two_sigmoid_of_half(st[j]["gb"]) * st[j]["pb"]).astype(BF16)
        y = jnp.dot(merged2, wout_ref[...], preferred_element_type=F32)
        ms_y = jnp.mean(y * y, axis=-1, keepdims=True)
        rows = slice(j * R, (j + 1) * R)
        o_ref[0, rows, :] = x_ref[0, rows, :] + (y * lax.rsqrt(ms_y + EPS)) * post_gain
        st[j].clear()

    n_units = blk_per_sub * N_KV_HEADS
    for j in range(n_sub + 1):
        if j < n_sub:
            head(j)
            gmlp_in(j)
        if j >= 1:
            out_proj(j - 1)
        if j < n_sub:
            rope_and_scores(j)
            st[j]["attn"] = [[None] * (N_Q_HEADS // 2) for _ in range(blk_per_sub)]
            st[j]["za"] = proj(j, c_za, c_u)
            st[j]["zb"] = proj(j, c_zb, c_g)
            st[j]["ga"] = proj(j, c_g, c_g + d_model)
            for idx in range(n_units):
                pv_unit(j, idx)
            gating(j)
            st[j]["gb"] = proj(j, c_g + d_model, c_g + 2 * d_model)
            branch_proj(j)

    k_scr[:, 0:BLOCK, :] = k_scr[:, T:T + BLOCK, :]
    vt_scr[:, :, 0:BLOCK] = vt_scr[:, :, T:T + BLOCK]


def _const_spec(shape):
    return pl.BlockSpec(shape, lambda b, s: (0,) * len(shape), pipeline_mode=pl.Buffered(1))


def _layer(x, ada, positions, g_pre, g_post, w_in, sinks, ln_v_g, ln_v_b, w_s, b_s,
           w_proj_a, w_proj_b, w_out):
    B, S, D = x.shape
    T = SEQ_TILE
    assert S % T == 0 and T % SUB_TILE == 0 and SUB_TILE % BLOCK == 0 and SUB_TILE % CHUNK == 0
    d_a = N_Q_HEADS * HEAD_DIM
    d_kv = N_KV_HEADS * HEAD_DIM
    d_b = w_proj_b.shape[0]
    d_in = w_in.shape[1]
    assert d_in == 2 * d_a + 2 * d_kv + 3 * d_b + 2 * D

    half = HEAD_DIM // 2
    inv_freq = ROPE_THETA ** (-jnp.arange(half, dtype=F32) / half)
    invf = jnp.broadcast_to(inv_freq[:, None], (half, LANES))

    c_za, c_u = d_a + 2 * d_kv, 2 * d_a + 2 * d_kv
    c_zb = c_u + 2 * d_b
    col = jnp.arange(d_in)
    halved = ((col >= c_za) & (col < c_u)) | (col >= c_zb)
    inv_sqrt2 = 1.0 / math.sqrt(2.0)
    col_scale = jnp.where(halved, 0.5, jnp.where((col >= c_u) & (col < c_zb), inv_sqrt2, 1.0))
    w_in_b = (w_in * col_scale.astype(F32)[None, :]).astype(BF16)
    w_out_b = (w_out * 0.5).astype(BF16)
    hbm = pl.BlockSpec(memory_space=pl.ANY)

    kern = functools.partial(_block_kernel, d_a=d_a, d_kv=d_kv, d_b=d_b, d_model=D)
    return pl.pallas_call(
        kern,
        out_shape=jax.ShapeDtypeStruct((B, S, D), x.dtype),
        grid=(B, S // T),
        in_specs=[
            pl.BlockSpec(memory_space=pltpu.SMEM),
            pl.BlockSpec((1, T, D), lambda b, s: (b, s, 0)),
            pl.BlockSpec((B, T), lambda b, s: (0, s)),
            _const_spec((B, 3 * D)),
            _const_spec((1, D)),
            _const_spec((1, D)),
            _const_spec((half, LANES)),
            hbm,
            _const_spec((1, d_b)),
            _const_spec((1, d_b)),
            _const_spec((N_GROUPS, CHUNK, CHUNK)),
            _const_spec((CHUNK, N_GROUPS)),
            hbm,
            hbm,
            hbm,
        ],
        out_specs=pl.BlockSpec((1, T, D), lambda b, s: (b, s, 0)),
        scratch_shapes=[
            pltpu.VMEM((N_KV_HEADS, BLOCK + T, HEAD_DIM), BF16),
            pltpu.VMEM((N_KV_HEADS, HEAD_DIM, BLOCK + T), BF16),
            pltpu.VMEM((D, d_in), BF16),
            pltpu.VMEM((d_a, D), BF16),
            pltpu.VMEM((d_b, D), BF16),
            pltpu.VMEM((D, D), BF16),
            pltpu.SemaphoreType.DMA((4,)),
        ],
        compiler_params=pltpu.CompilerParams(
            dimension_semantics=("arbitrary", "arbitrary"),
            vmem_limit_bytes=VMEM_LIMIT_BYTES),
        name="hybrid_block",
    )(sinks, x, positions, ada,
      g_pre[None, :], g_post[None, :], invf, w_in_b,
      ln_v_g[None, :], ln_v_b[None, :], w_s, b_s.T,
      w_proj_a.astype(BF16), (w_proj_b * inv_sqrt2).astype(BF16), w_out_b)


def _ada(c, w_ada, b_ada):
    B, D = c.shape
    n_out = w_ada.shape[1]
    tn = D
    return pl.pallas_call(
        _ada_kernel,
        out_shape=jax.ShapeDtypeStruct((B, n_out), F32),
        grid=(n_out // tn,),
        in_specs=[
            pl.BlockSpec((B, D), lambda j: (0, 0)),
            pl.BlockSpec((D, tn), lambda j: (0, j)),
            pl.BlockSpec((1, tn), lambda j: (0, j)),
        ],
        out_specs=pl.BlockSpec((B, tn), lambda j: (0, j)),
        compiler_params=pltpu.CompilerParams(dimension_semantics=("arbitrary",)),
        name="adaln_modulation",
    )(c, w_ada, b_ada[None, :])


def kernel(x, c, positions, w_ada, b_ada, g_pre, g_post, w_in, sinks, ln_v_g, ln_v_b, w_s, b_s,
           w_proj_a, w_proj_b, w_out):
    depth = w_in.shape[0]
    for l in range(depth):
        ada = _ada(c, w_ada[l], b_ada[l])
        x = _layer(x, ada, positions, g_pre[l], g_post[l], w_in[l], sinks[l], ln_v_g[l],
                   ln_v_b[l], w_s[l], b_s[l], w_proj_a[l], w_proj_b[l], w_out[l])
    return x
```

```python
import functools
import math

import jax
import jax.numpy as jnp
from jax import lax
from jax.experimental import pallas as pl
from jax.experimental.pallas import tpu as pltpu

HEAD_DIM = 64
N_Q_HEADS = 8
N_KV_HEADS = 2
GQ = N_Q_HEADS // N_KV_HEADS
WINDOW = 128
BLOCK = 128
ROPE_THETA = 10000.0
CHUNK = 128
N_GROUPS = 4
EPS = 1e-6
NEG = -1e30
LOG2E = 1.4426950408889634

LANES = 128
SEQ_TILE = 1024
SUB_TILE = 256
VMEM_LIMIT_BYTES = 44 * 1024 * 1024

BF16 = jnp.bfloat16
F32 = jnp.float32


def _ada_kernel(c_ref, w_ref, b_ref, o_ref):
    c = c_ref[...]
    c_act = c * jax.nn.sigmoid(c)
    o_ref[...] = jnp.dot(c_act, w_ref[...], preferred_element_type=F32) + b_ref[...]


def _two_sigmoid_of_half(h):
    return jnp.tanh(h) + 1.0


def _silu_of_half(h):
    return h * _two_sigmoid_of_half(h)


def _gelu_times_sqrt2(t):
    return t * (1.0 + lax.erf(t))


def _rope(t, cos, sin_signed, first_half):
    outs = []
    for c in range(t.shape[1] // LANES):
        tc = t[:, c * LANES:(c + 1) * LANES]
        rot = jnp.where(first_half,
                        pltpu.roll(tc, LANES - HEAD_DIM // 2, axis=1),
                        pltpu.roll(tc, HEAD_DIM // 2, axis=1))
        outs.append(tc * cos + rot * sin_signed)
    return outs


def _block_kernel(sinks_ref, x_ref, pos_ref, ada_ref, gpre_ref, gpost_ref, invf_ref,
                  win_ref, lng_ref, lnb_ref, ws_ref, bs_ref, wpa_ref, wpb_ref, wout_ref,
                  o_ref, k_scr, vt_scr, *, d_a, d_kv, d_b, d_model):
    T = x_ref.shape[1]
    R = SUB_TILE
    n_sub = T // R
    blk_per_sub = R // BLOCK
    s_idx = pl.program_id(1)

    c_q, c_k, c_v = 0, d_a, d_a + d_kv
    c_za = c_v + d_kv
    c_u = c_za + d_a
    c_vb = c_u + d_b
    c_zb = c_vb + d_b
    c_g = c_zb + d_b

    @pl.when(s_idx == 0)
    def _():
        k_scr[:, 0:BLOCK, :] = jnp.zeros((N_KV_HEADS, BLOCK, HEAD_DIM), BF16)
        vt_scr[:, :, 0:BLOCK] = jnp.zeros((N_KV_HEADS, HEAD_DIM, BLOCK), BF16)

    b_idx = pl.program_id(0)
    ada = ada_ref[pl.ds(b_idx, 1), :]
    shift = ada[:, 0:d_model]
    scale = ada[:, d_model:2 * d_model]
    gate = ada[:, 2 * d_model:3 * d_model]
    pre_gain = gpre_ref[...] * (1.0 + scale)
    post_gain = gate * gpost_ref[...]
    invf = invf_ref[...]
    lane = lax.broadcasted_iota(jnp.int32, (1, LANES), 1)
    first_half = (lane % HEAD_DIM) < (HEAD_DIM // 2)

    kj = lax.broadcasted_iota(jnp.int32, (2 * BLOCK, BLOCK), 0)
    qi = lax.broadcasted_iota(jnp.int32, (2 * BLOCK, BLOCK), 1)
    rel = qi + BLOCK - kj
    in_win = (rel >= 0) & (rel < WINDOW)
    first_lo = jnp.where(s_idx == 0, BLOCK, 0)
    ti = lax.broadcasted_iota(jnp.int32, (CHUNK, CHUNK), 0)
    si = lax.broadcasted_iota(jnp.int32, (CHUNK, CHUNK), 1)
    causal = si <= ti
    group_w = d_b // N_GROUPS
    q_scale = LOG2E / math.sqrt(HEAD_DIM)

    st = [dict() for _ in range(n_sub)]

    def proj(j, lo, hi):
        return jnp.dot(st[j]["hb"], win_ref[:, lo:hi], preferred_element_type=F32)

    def head(j):
        xc = x_ref[0, j * R:(j + 1) * R, :]
        ms = jnp.mean(xc * xc, axis=-1, keepdims=True)
        st[j]["hb"] = ((xc * lax.rsqrt(ms + EPS)) * pre_gain + shift).astype(BF16)
        st[j]["qkv"] = proj(j, c_q, c_za)

    def gmlp_in(j):
        v = _gelu_times_sqrt2(proj(j, c_vb, c_zb))
        mu = jnp.mean(v, axis=-1, keepdims=True)
        vc = v - mu
        var = jnp.mean(vc * vc, axis=-1, keepdims=True)
        st[j]["vn"] = (vc * lax.rsqrt(var + 2.0 * EPS) * lng_ref[...] + lnb_ref[...]).astype(BF16)
        st[j]["u"] = _gelu_times_sqrt2(proj(j, c_u, c_vb))

    def rope_and_scores(j):
        pos = pos_ref[pl.ds(b_idx, 1), j * R:(j + 1) * R].astype(F32)
        cos_rows, sin_rows = [], []
        for b in range(blk_per_sub):
            ang = invf * pos[:, b * BLOCK:(b + 1) * BLOCK]
            cs = jnp.cos(ang)
            sn = jnp.sin(ang)
            cos_rows.append(jnp.concatenate([cs, cs, cs, cs], axis=0).T)
            sin_rows.append(jnp.concatenate([-sn, sn, -sn, sn], axis=0).T)
        cos = jnp.concatenate(cos_rows, axis=0)
        sin_signed = jnp.concatenate(sin_rows, axis=0)
        qkv = st[j]["qkv"]
        q_tiles = _rope(qkv[:, c_q:c_k], cos, sin_signed, first_half)
        k_r = _rope(qkv[:, c_k:c_v], cos, sin_signed, first_half)[0]
        vt_new = qkv[:, c_v:c_za].T
        lo = BLOCK + j * R
        for g in range(N_KV_HEADS):
            k_scr[g, lo:lo + R, :] = k_r[:, g * HEAD_DIM:(g + 1) * HEAD_DIM].astype(BF16)
            vt_scr[g, :, lo:lo + R] = vt_new[g * HEAD_DIM:(g + 1) * HEAD_DIM, :].astype(BF16)
        q_heads = []
        for c in range(len(q_tiles)):
            qs = (q_tiles[c] * q_scale).astype(BF16)
            q_heads.append(qs[:, 0:HEAD_DIM])
            q_heads.append(qs[:, HEAD_DIM:2 * HEAD_DIM])
        ss = []
        for b in range(blk_per_sub):
            n = j * blk_per_sub + b
            band = slice(n * BLOCK, n * BLOCK + 2 * BLOCK)
            for g in range(N_KV_HEADS):
                kb = k_scr[g, band, :]
                q_stack = jnp.concatenate(
                    [q_heads[g * GQ + h][b * BLOCK:(b + 1) * BLOCK] for h in range(GQ)], axis=0)
                ss.append(lax.dot_general(kb, q_stack, (((1,), (1,)), ((), ())),
                                          preferred_element_type=F32))
        st[j]["scores"] = ss

    def pv_unit(j, idx):
        b, g = divmod(idx, N_KV_HEADS)
        n = j * blk_per_sub + b
        band = slice(n * BLOCK, n * BLOCK + 2 * BLOCK)
        s = st[j]["scores"][idx]
        valid = in_win & (kj >= first_lo) if n == 0 else in_win
        s = jnp.where(jnp.concatenate([valid] * GQ, axis=1), s, NEG)
        sink = jnp.concatenate(
            [jnp.full((1, BLOCK), sinks_ref[g * GQ + h] * LOG2E, F32) for h in range(GQ)], axis=1)
        m = jnp.maximum(jnp.max(s, axis=0, keepdims=True), sink)
        p = jnp.exp2(s - m)
        denom = jnp.sum(p, axis=0, keepdims=True) + jnp.exp2(sink - m)
        ot = jnp.dot(vt_scr[g, :, band], p.astype(BF16),
                     preferred_element_type=F32) * (1.0 / denom)
        for pair in range(GQ // 2):
            two = jnp.concatenate(
                [ot[:, (2 * pair) * BLOCK:(2 * pair + 1) * BLOCK],
                 ot[:, (2 * pair + 1) * BLOCK:(2 * pair + 2) * BLOCK]], axis=0)
            st[j]["attn"][b][g * (GQ // 2) + pair] = two.T

    def gating(j):
        rows_out = []
        for n in range(R // CHUNK):
            rows = slice(n * CHUNK, (n + 1) * CHUNK)
            cols_out = []
            for g in range(N_GROUPS):
                w_g = jnp.where(causal, ws_ref[g], 0.0).astype(BF16)
                bias = jnp.broadcast_to(bs_ref[:, g:g + 1], (CHUNK, group_w))
                cols = slice(g * group_w, (g + 1) * group_w)
                sv = jnp.dot(w_g, st[j]["vn"][rows, cols], preferred_element_type=F32) + bias
                cols_out.append(st[j]["u"][rows, cols] * sv)
            rows_out.append(jnp.concatenate(cols_out, axis=1))
        st[j]["yb"] = jnp.concatenate(rows_out, axis=0)

    def branch_proj(j):
        attn = jnp.concatenate([jnp.concatenate(row, axis=1) for row in st[j]["attn"]], axis=0)
        y_a = attn.astype(BF16) * _silu_of_half(st[j]["za"]).astype(BF16)
        st[j]["pa"] = jnp.dot(y_a, wpa_ref[...], preferred_element_type=F32)
        y_b = st[j]["yb"].astype(BF16) * _silu_of_half(st[j]["zb"]).astype(BF16)
        st[j]["pb"] = jnp.dot(y_b, wpb_ref[...], preferred_element_type=F32)

    def out_proj(j):
        merged2 = (_two_sigmoid_of_half(st[j]["ga"]).astype(BF16) * st[j]["pa"].astype(BF16)
                   + _two_sigmoid_of_half(st[j]["gb"]).astype(BF16) * st[j]["pb"].astype(BF16))
        y = jnp.dot(merged2, wout_ref[...], preferred_element_type=F32)
        ms_y = jnp.mean(y * y, axis=-1, keepdims=True)
        rows = slice(j * R, (j + 1) * R)
        o_ref[0, rows, :] = x_ref[0, rows, :] + (y * lax.rsqrt(ms_y + EPS)) * post_gain
        st[j].clear()

    n_units = blk_per_sub * N_KV_HEADS
    for j in range(n_sub + 1):
        if j < n_sub:
            head(j)
            gmlp_in(j)
        if j >= 1:
            out_proj(j - 1)
        if j < n_sub:
            rope_and_scores(j)
            st[j]["attn"] = [[None] * (N_Q_HEADS // 2) for _ in range(blk_per_sub)]
            st[j]["za"] = proj(j, c_za, c_u)
            st[j]["zb"] = proj(j, c_zb, c_g)
            st[j]["ga"] = proj(j, c_g, c_g + d_model)
            for idx in range(n_units):
                pv_unit(j, idx)
            gating(j)
            st[j]["gb"] = proj(j, c_g + d_model, c_g + 2 * d_model)
            branch_proj(j)

    k_scr[:, 0:BLOCK, :] = k_scr[:, T:T + BLOCK, :]
    vt_scr[:, :, 0:BLOCK] = vt_scr[:, :, T:T + BLOCK]


def _const_spec(shape):
    return pl.BlockSpec(shape, lambda b, s: (0,) * len(shape), pipeline_mode=pl.Buffered(1))


def _layer(x, ada, positions, g_pre, g_post, w_in, sinks, ln_v_g, ln_v_b, w_s, b_s,
           w_proj_a, w_proj_b, w_out):
    B, S, D = x.shape
    T = SEQ_TILE
    assert S % T == 0 and T % SUB_TILE == 0 and SUB_TILE % BLOCK == 0 and SUB_TILE % CHUNK == 0
    d_a = N_Q_HEADS * HEAD_DIM
    d_kv = N_KV_HEADS * HEAD_DIM
    d_b = w_proj_b.shape[0]
    d_in = w_in.shape[1]
    assert d_in == 2 * d_a + 2 * d_kv + 3 * d_b + 2 * D

    half = HEAD_DIM // 2
    inv_freq = ROPE_THETA ** (-jnp.arange(half, dtype=F32) / half)
    invf = jnp.broadcast_to(inv_freq[:, None], (half, LANES))

    c_za, c_u = d_a + 2 * d_kv, 2 * d_a + 2 * d_kv
    c_zb = c_u + 2 * d_b
    col = jnp.arange(d_in)
    halved = ((col >= c_za) & (col < c_u)) | (col >= c_zb)
    inv_sqrt2 = 1.0 / math.sqrt(2.0)
    col_scale = jnp.where(halved, 0.5, jnp.where((col >= c_u) & (col < c_zb), inv_sqrt2, 1.0))
    w_in_b = (w_in * col_scale.astype(F32)[None, :]).astype(BF16)
    w_out_b = (w_out * 0.5).astype(BF16)

    kern = functools.partial(_block_kernel, d_a=d_a, d_kv=d_kv, d_b=d_b, d_model=D)
    return pl.pallas_call(
        kern,
        out_shape=jax.ShapeDtypeStruct((B, S, D), x.dtype),
        grid=(B, S // T),
        in_specs=[
            pl.BlockSpec(memory_space=pltpu.SMEM),
            pl.BlockSpec((1, T, D), lambda b, s: (b, s, 0)),
            pl.BlockSpec((B, T), lambda b, s: (0, s)),
            _const_spec((B, 3 * D)),
            _const_spec((1, D)),
            _const_spec((1, D)),
            _const_spec((half, LANES)),
            _const_spec((D, d_in)),
            _const_spec((1, d_b)),
            _const_spec((1, d_b)),
            _const_spec((N_GROUPS, CHUNK, CHUNK)),
            _const_spec((CHUNK, N_GROUPS)),
            _const_spec((d_a, D)),
            _const_spec((d_b, D)),
            _const_spec((D, D)),
        ],
        out_specs=pl.BlockSpec((1, T, D), lambda b, s: (b, s, 0)),
        scratch_shapes=[
            pltpu.VMEM((N_KV_HEADS, BLOCK + T, HEAD_DIM), BF16),
            pltpu.VMEM((N_KV_HEADS, HEAD_DIM, BLOCK + T), BF16),
        ],
        compiler_params=pltpu.CompilerParams(
            dimension_semantics=("arbitrary", "arbitrary"),
            vmem_limit_bytes=VMEM_LIMIT_BYTES),
        name="hybrid_block",
    )(sinks, x, positions, ada,
      g_pre[None, :], g_post[None, :], invf, w_in_b,
      ln_v_g[None, :], ln_v_b[None, :], w_s, b_s.T,
      w_proj_a.astype(BF16), (w_proj_b * inv_sqrt2).astype(BF16), w_out_b)


def _ada(c, w_ada, b_ada):
    B, D = c.shape
    n_out = w_ada.shape[1]
    tn = D
    return pl.pallas_call(
        _ada_kernel,
        out_shape=jax.ShapeDtypeStruct((B, n_out), F32),
        grid=(n_out // tn,),
        in_specs=[
            pl.BlockSpec((B, D), lambda j: (0, 0)),
            pl.BlockSpec((D, tn), lambda j: (0, j)),
            pl.BlockSpec((1, tn), lambda j: (0, j)),
        ],
        out_specs=pl.BlockSpec((B, tn), lambda j: (0, j)),
        compiler_params=pltpu.CompilerParams(dimension_semantics=("arbitrary",)),
        name="adaln_modulation",
    )(c, w_ada, b_ada[None, :])


def kernel(x, c, positions, w_ada, b_ada, g_pre, g_post, w_in, sinks, ln_v_g, ln_v_b, w_s, b_s,
           w_proj_a, w_proj_b, w_out):
    depth = w_in.shape[0]
    for l in range(depth):
        ada = _ada(c, w_ada[l], b_ada[l])
        x = _layer(x, ada, positions, g_pre[l], g_post[l], w_in[l], sinks[l], ln_v_g[l],
                   ln_v_b[l], w_s[l], b_s[l], w_proj_a[l], w_proj_b[l], w_out[l])
    return x
```

```python
import functools
import math

import jax
import jax.numpy as jnp
from jax import lax
from jax.experimental import pallas as pl
from jax.experimental.pallas import tpu as pltpu

HEAD_DIM = 64
N_Q_HEADS = 8
N_KV_HEADS = 2
GQ = N_Q_HEADS // N_KV_HEADS
WINDOW = 128
BLOCK = 128
ROPE_THETA = 10000.0
CHUNK = 128
N_GROUPS = 4
EPS = 1e-6
NEG = -1e30
LOG2E = 1.4426950408889634

LANES = 128
SEQ_TILE = 1024
SUB_TILE = 256
VMEM_LIMIT_BYTES = 44 * 1024 * 1024

BF16 = jnp.bfloat16
F32 = jnp.float32


def _ada_kernel(c_ref, w_ref, b_ref, o_ref):
    c = c_ref[...]
    c_act = c * jax.nn.sigmoid(c)
    o_ref[...] = jnp.dot(c_act, w_ref[...], preferred_element_type=F32) + b_ref[...]


def _two_sigmoid_of_half(h):
    return jnp.tanh(h) + 1.0


def _silu_of_half(h):
    return h * _two_sigmoid_of_half(h)


def _gelu_times_sqrt2(t):
    return t * (1.0 + lax.erf(t))


def _rope(t, cos, sin_signed, first_half):
    outs = []
    for c in range(t.shape[1] // LANES):
        tc = t[:, c * LANES:(c + 1) * LANES]
        rot = jnp.where(first_half,
                        pltpu.roll(tc, LANES - HEAD_DIM // 2, axis=1),
                        pltpu.roll(tc, HEAD_DIM // 2, axis=1))
        outs.append(tc * cos + rot * sin_signed)
    return outs


def _block_kernel(sinks_ref, x_ref, pos_ref, ada_ref, gpre_ref, gpost_ref, invf_ref,
                  win_ref, lng_ref, lnb_ref, ws_ref, bs_ref, wpa_ref, wpb_ref, wout_ref,
                  o_ref, k_scr, vt_scr, *, d_a, d_kv, d_b, d_model):
    T = x_ref.shape[1]
    R = SUB_TILE
    n_sub = T // R
    blk_per_sub = R // BLOCK
    s_idx = pl.program_id(1)

    c_q, c_k, c_v = 0, d_a, d_a + d_kv
    c_za = c_v + d_kv
    c_u = c_za + d_a
    c_vb = c_u + d_b
    c_zb = c_vb + d_b
    c_g = c_zb + d_b

    @pl.when(s_idx == 0)
    def _():
        k_scr[:, 0:BLOCK, :] = jnp.zeros((N_KV_HEADS, BLOCK, HEAD_DIM), BF16)
        vt_scr[:, :, 0:BLOCK] = jnp.zeros((N_KV_HEADS, HEAD_DIM, BLOCK), BF16)

    b_idx = pl.program_id(0)
    ada = ada_ref[pl.ds(b_idx, 1), :]
    shift = ada[:, 0:d_model]
    scale = ada[:, d_model:2 * d_model]
    gate = ada[:, 2 * d_model:3 * d_model]
    pre_gain = gpre_ref[...] * (1.0 + scale)
    post_gain = gate * gpost_ref[...]
    invf = invf_ref[...]
    lane = lax.broadcasted_iota(jnp.int32, (1, LANES), 1)
    first_half = (lane % HEAD_DIM) < (HEAD_DIM // 2)

    kj = lax.broadcasted_iota(jnp.int32, (2 * BLOCK, BLOCK), 0)
    qi = lax.broadcasted_iota(jnp.int32, (2 * BLOCK, BLOCK), 1)
    rel = qi + BLOCK - kj
    in_win = (rel >= 0) & (rel < WINDOW)
    first_lo = jnp.where(s_idx == 0, BLOCK, 0)
    ti = lax.broadcasted_iota(jnp.int32, (CHUNK, CHUNK), 0)
    si = lax.broadcasted_iota(jnp.int32, (CHUNK, CHUNK), 1)
    causal = si <= ti
    group_w = d_b // N_GROUPS
    q_scale = LOG2E / math.sqrt(HEAD_DIM)

    st = [dict() for _ in range(n_sub)]

    def proj(j, lo, hi):
        return jnp.dot(st[j]["hb"], win_ref[:, lo:hi], preferred_element_type=F32)

    def head(j):
        xc = x_ref[0, j * R:(j + 1) * R, :]
        ms = jnp.mean(xc * xc, axis=-1, keepdims=True)
        st[j]["hb"] = ((xc * lax.rsqrt(ms + EPS)) * pre_gain + shift).astype(BF16)
        st[j]["qkv"] = proj(j, c_q, c_za)

    def gmlp_in(j):
        v = _gelu_times_sqrt2(proj(j, c_vb, c_zb))
        mu = jnp.mean(v, axis=-1, keepdims=True)
        vc = v - mu
        var = jnp.mean(vc * vc, axis=-1, keepdims=True)
        st[j]["vn"] = (vc * lax.rsqrt(var + 2.0 * EPS) * lng_ref[...] + lnb_ref[...]).astype(BF16)
        st[j]["u"] = _gelu_times_sqrt2(proj(j, c_u, c_vb))

    def rope_and_scores(j):
        pos = pos_ref[pl.ds(b_idx, 1), j * R:(j + 1) * R].astype(F32)
        cos_rows, sin_rows = [], []
        for b in range(blk_per_sub):
            ang = invf * pos[:, b * BLOCK:(b + 1) * BLOCK]
            cs = jnp.cos(ang)
            sn = jnp.sin(ang)
            cos_rows.append(jnp.concatenate([cs, cs, cs, cs], axis=0).T)
            sin_rows.append(jnp.concatenate([-sn, sn, -sn, sn], axis=0).T)
        cos = jnp.concatenate(cos_rows, axis=0)
        sin_signed = jnp.concatenate(sin_rows, axis=0)
        qkv = st[j]["qkv"]
        q_tiles = _rope(qkv[:, c_q:c_k], cos, sin_signed, first_half)
        k_r = _rope(qkv[:, c_k:c_v], cos, sin_signed, first_half)[0]
        vt_new = qkv[:, c_v:c_za].T
        lo = BLOCK + j * R
        for g in range(N_KV_HEADS):
            k_scr[g, lo:lo + R, :] = k_r[:, g * HEAD_DIM:(g + 1) * HEAD_DIM].astype(BF16)
            vt_scr[g, :, lo:lo + R] = vt_new[g * HEAD_DIM:(g + 1) * HEAD_DIM, :].astype(BF16)
        q_heads = []
        for c in range(len(q_tiles)):
            qs = (q_tiles[c] * q_scale).astype(BF16)
            q_heads.append(qs[:, 0:HEAD_DIM])
            q_heads.append(qs[:, HEAD_DIM:2 * HEAD_DIM])
        ss = []
        for b in range(blk_per_sub):
            n = j * blk_per_sub + b
            band = slice(n * BLOCK, n * BLOCK + 2 * BLOCK)
            for g in range(N_KV_HEADS):
                kb = k_scr[g, band, :]
                q_stack = jnp.concatenate(
                    [q_heads[g * GQ + h][b * BLOCK:(b + 1) * BLOCK] for h in range(GQ)], axis=0)
                ss.append(lax.dot_general(kb, q_stack, (((1,), (1,)), ((), ())),
                                          preferred_element_type=F32))
        st[j]["scores"] = ss

    def pv_unit(j, idx):
        b, g = divmod(idx, N_KV_HEADS)
        n = j * blk_per_sub + b
        band = slice(n * BLOCK, n * BLOCK + 2 * BLOCK)
        s = st[j]["scores"][idx]
        valid = in_win & (kj >= first_lo) if n == 0 else in_win
        s = jnp.where(jnp.concatenate([valid] * GQ, axis=1), s, NEG)
        sink = jnp.concatenate(
            [jnp.full((1, BLOCK), sinks_ref[g * GQ + h] * LOG2E, F32) for h in range(GQ)], axis=1)
        m = jnp.maximum(jnp.max(s, axis=0, keepdims=True), sink)
        p = jnp.exp2(s - m)
        denom = jnp.sum(p, axis=0, keepdims=True) + jnp.exp2(sink - m)
        ot = jnp.dot(vt_scr[g, :, band], p.astype(BF16),
                     preferred_element_type=F32) * (1.0 / denom)
        for pair in range(GQ // 2):
            two = jnp.concatenate(
                [ot[:, (2 * pair) * BLOCK:(2 * pair + 1) * BLOCK],
                 ot[:, (2 * pair + 1) * BLOCK:(2 * pair + 2) * BLOCK]], axis=0)
            st[j]["attn"][b][g * (GQ // 2) + pair] = two.T

    def gating(j):
        rows_out = []
        for n in range(R // CHUNK):
            rows = slice(n * CHUNK, (n + 1) * CHUNK)
            cols_out = []
            for g in range(N_GROUPS):
                w_g = jnp.where(causal, ws_ref[g], 0.0).astype(BF16)
                bias = jnp.broadcast_to(bs_ref[:, g:g + 1], (CHUNK, group_w))
                cols = slice(g * group_w, (g + 1) * group_w)
                sv = jnp.dot(w_g, st[j]["vn"][rows, cols], preferred_element_type=F32) + bias
                cols_out.append(st[j]["u"][rows, cols] * sv)
            rows_out.append(jnp.concatenate(cols_out, axis=1))
        st[j]["yb"] = jnp.concatenate(rows_out, axis=0)

    def branch_proj(j):
        attn = jnp.concatenate([jnp.concatenate(row, axis=1) for row in st[j]["attn"]], axis=0)
        y_a = attn.astype(BF16) * _silu_of_half(st[j]["za"]).astype(BF16)
        st[j]["pa"] = jnp.dot(y_a, wpa_ref[...], preferred_element_type=F32)
        y_b = st[j]["yb"].astype(BF16) * _silu_of_half(st[j]["zb"]).astype(BF16)
        st[j]["pb"] = jnp.dot(y_b, wpb_ref[...], preferred_element_type=F32)

    def out_proj(j):
        merged2 = (_two_sigmoid_of_half(st[j]["ga"]) * st[j]["pa"]
                   + _two_sigmoid_of_half(st[j]["gb"]) * st[j]["pb"]).astype(BF16)
        y = jnp.dot(merged2, wout_ref[...], preferred_element_type=F32)
        ms_y = jnp.mean(y * y, axis=-1, keepdims=True)
        rows = slice(j * R, (j + 1) * R)
        o_ref[0, rows, :] = x_ref[0, rows, :] + (y * lax.rsqrt(ms_y + EPS)) * post_gain
        st[j].clear()

    n_units = blk_per_sub * N_KV_HEADS
    for j in range(n_sub + 1):
        if j < n_sub:
            head(j)
            gmlp_in(j)
        if j >= 1:
            out_proj(j - 1)
        if j < n_sub:
            rope_and_scores(j)
            st[j]["attn"] = [[None] * (N_Q_HEADS // 2) for _ in range(blk_per_sub)]
            st[j]["za"] = proj(j, c_za, c_u)
            st[j]["zb"] = proj(j, c_zb, c_g)
            st[j]["ga"] = proj(j, c_g, c_g + d_model)
            for idx in range(n_units):
                pv_unit(j, idx)
            gating(j)
            st[j]["gb"] = proj(j, c_g + d_model, c_g + 2 * d_model)
            branch_proj(j)

    k_scr[:, 0:BLOCK, :] = k_scr[:, T:T + BLOCK, :]
    vt_scr[:, :, 0:BLOCK] = vt_scr[:, :, T:T + BLOCK]


def _const_spec(shape):
    return pl.BlockSpec(shape, lambda b, s: (0,) * len(shape), pipeline_mode=pl.Buffered(1))


def _layer(x, ada, positions, g_pre, g_post, w_in, sinks, ln_v_g, ln_v_b, w_s, b_s,
           w_proj_a, w_proj_b, w_out):
    B, S, D = x.shape
    T = SEQ_TILE
    assert S % T == 0 and T % SUB_TILE == 0 and SUB_TILE % BLOCK == 0 and SUB_TILE % CHUNK == 0
    d_a = N_Q_HEADS * HEAD_DIM
    d_kv = N_KV_HEADS * HEAD_DIM
    d_b = w_proj_b.shape[0]
    d_in = w_in.shape[1]
    assert d_in == 2 * d_a + 2 * d_kv + 3 * d_b + 2 * D

    half = HEAD_DIM // 2
    inv_freq = ROPE_THETA ** (-jnp.arange(half, dtype=F32) / half)
    invf = jnp.broadcast_to(inv_freq[:, None], (half, LANES))

    c_za, c_u = d_a + 2 * d_kv, 2 * d_a + 2 * d_kv
    c_zb = c_u + 2 * d_b
    col = jnp.arange(d_in)
    halved = ((col >= c_za) & (col < c_u)) | (col >= c_zb)
    inv_sqrt2 = 1.0 / math.sqrt(2.0)
    col_scale = jnp.where(halved, 0.5, jnp.where((col >= c_u) & (col < c_zb), inv_sqrt2, 1.0))
    w_in_b = (w_in * col_scale.astype(F32)[None, :]).astype(BF16)
    w_out_b = (w_out * 0.5).astype(BF16)

    kern = functools.partial(_block_kernel, d_a=d_a, d_kv=d_kv, d_b=d_b, d_model=D)
    return pl.pallas_call(
        kern,
        out_shape=jax.ShapeDtypeStruct((B, S, D), x.dtype),
        grid=(B, S // T),
        in_specs=[
            pl.BlockSpec(memory_space=pltpu.SMEM),
            pl.BlockSpec((1, T, D), lambda b, s: (b, s, 0)),
            pl.BlockSpec((B, T), lambda b, s: (0, s)),
            _const_spec((B, 3 * D)),
            _const_spec((1, D)),
            _const_spec((1, D)),
            _const_spec((half, LANES)),
            _const_spec((D, d_in)),
            _const_spec((1, d_b)),
            _const_spec((1, d_b)),
            _const_spec((N_GROUPS, CHUNK, CHUNK)),
            _const_spec((CHUNK, N_GROUPS)),
            _const_spec((d_a, D)),
            _const_spec((d_b, D)),
            _const_spec((D, D)),
        ],
        out_specs=pl.BlockSpec((1, T, D), lambda b, s: (b, s, 0)),
        scratch_shapes=[
            pltpu.VMEM((N_KV_HEADS, BLOCK + T, HEAD_DIM), BF16),
            pltpu.VMEM((N_KV_HEADS, HEAD_DIM, BLOCK + T), BF16),
        ],
        compiler_params=pltpu.CompilerParams(
            dimension_semantics=("arbitrary", "arbitrary"),
            vmem_limit_bytes=VMEM_LIMIT_BYTES),
        name="hybrid_block",
    )(sinks, x, positions, ada,
      g_pre[None, :], g_post[None, :], invf, w_in_b,
      ln_v_g[None, :], ln_v_b[None, :], w_s, b_s.T,
      w_proj_a.astype(BF16), (w_proj_b * inv_sqrt2).astype(BF16), w_out_b)


def _ada(c, w_ada, b_ada):
    B, D = c.shape
    n_out = w_ada.shape[1]
    tn = D
    return pl.pallas_call(
        _ada_kernel,
        out_shape=jax.ShapeDtypeStruct((B, n_out), F32),
        grid=(n_out // tn,),
        in_specs=[
            pl.BlockSpec((B, D), lambda j: (0, 0)),
            pl.BlockSpec((D, tn), lambda j: (0, j)),
            pl.BlockSpec((1, tn), lambda j: (0, j)),
        ],
        out_specs=pl.BlockSpec((B, tn), lambda j: (0, j)),
        compiler_params=pltpu.CompilerParams(dimension_semantics=("arbitrary",)),
        name="adaln_modulation",
    )(c, w_ada, b_ada[None, :])


def kernel(x, c, positions, w_ada, b_ada, g_pre, g_post, w_in, sinks, ln_v_g, ln_v_b, w_s, b_s,
           w_proj_a, w_proj_b, w_out):
    depth = w_in.shape[0]
    for l in range(depth):
        ada = _ada(c, w_ada[l], b_ada[l])
        x = _layer(x, ada, positions, g_pre[l], g_post[l], w_in[l], sinks[l], ln_v_g[l],
                   ln_v_b[l], w_s[l], b_s[l], w_proj_a[l], w_proj_b[l], w_out[l])
    return x
```

```python
import functools
import math

import jax
import jax.numpy as jnp
from jax import lax
from jax.experimental import pallas as pl
from jax.experimental.pallas import tpu as pltpu

HEAD_DIM = 64
N_Q_HEADS = 8
N_KV_HEADS = 2
GQ = N_Q_HEADS // N_KV_HEADS
WINDOW = 128
BLOCK = 128
ROPE_THETA = 10000.0
CHUNK = 128
N_GROUPS = 4
EPS = 1e-6
NEG = -1e30
LOG2E = 1.4426950408889634

LANES = 128
SEQ_TILE = 1024
SUB_TILE = 256
VMEM_LIMIT_BYTES = 44 * 1024 * 1024

BF16 = jnp.bfloat16
F32 = jnp.float32


def _ada_kernel(c_ref, w_ref, b_ref, o_ref):
    c = c_ref[...]
    c_act = c * jax.nn.sigmoid(c)
    o_ref[...] = jnp.dot(c_act, w_ref[...], preferred_element_type=F32) + b_ref[...]


def _two_sigmoid_of_half(h):
    return jnp.tanh(h) + 1.0


def _silu_of_half(h):
    return h * _two_sigmoid_of_half(h)


def _gelu_times_sqrt2(t):
    return t * (1.0 + lax.erf(t))


def _rope(t, cos, sin_signed, first_half):
    outs = []
    for c in range(t.shape[1] // LANES):
        tc = t[:, c * LANES:(c + 1) * LANES]
        rot = jnp.where(first_half,
                        pltpu.roll(tc, LANES - HEAD_DIM // 2, axis=1),
                        pltpu.roll(tc, HEAD_DIM // 2, axis=1))
        outs.append(tc * cos + rot * sin_signed)
    return outs


def _block_kernel(sinks_ref, x_ref, pos_ref, ada_ref, gpre_ref, gpost_ref, invf_ref,
                  win_ref, lng_ref, lnb_ref, ws_ref, bs_ref, wpa_ref, wpb_ref, wout_ref,
                  o_ref, k_scr, vt_scr, *, d_a, d_kv, d_b, d_model):
    T = x_ref.shape[1]
    R = SUB_TILE
    n_sub = T // R
    blk_per_sub = R // BLOCK
    s_idx = pl.program_id(1)

    c_q, c_k, c_v = 0, d_a, d_a + d_kv
    c_za = c_v + d_kv
    c_u = c_za + d_a
    c_vb = c_u + d_b
    c_zb = c_vb + d_b
    c_g = c_zb + d_b

    @pl.when(s_idx == 0)
    def _():
        k_scr[:, 0:BLOCK, :] = jnp.zeros((N_KV_HEADS, BLOCK, HEAD_DIM), BF16)
        vt_scr[:, :, 0:BLOCK] = jnp.zeros((N_KV_HEADS, HEAD_DIM, BLOCK), BF16)

    b_idx = pl.program_id(0)
    ada = ada_ref[pl.ds(b_idx, 1), :]
    shift = ada[:, 0:d_model]
    scale = ada[:, d_model:2 * d_model]
    gate = ada[:, 2 * d_model:3 * d_model]
    pre_gain = gpre_ref[...] * (1.0 + scale)
    post_gain = gate * gpost_ref[...]
    invf = invf_ref[...]
    lane = lax.broadcasted_iota(jnp.int32, (1, LANES), 1)
    first_half = (lane % HEAD_DIM) < (HEAD_DIM // 2)

    kj = lax.broadcasted_iota(jnp.int32, (2 * BLOCK, BLOCK), 0)
    qi = lax.broadcasted_iota(jnp.int32, (2 * BLOCK, BLOCK), 1)
    rel = qi + BLOCK - kj
    in_win = (rel >= 0) & (rel < WINDOW)
    first_lo = jnp.where(s_idx == 0, BLOCK, 0)
    ti = lax.broadcasted_iota(jnp.int32, (CHUNK, CHUNK), 0)
    si = lax.broadcasted_iota(jnp.int32, (CHUNK, CHUNK), 1)
    causal = si <= ti
    group_w = d_b // N_GROUPS
    q_scale = LOG2E / math.sqrt(HEAD_DIM)

    st = [dict() for _ in range(n_sub)]

    def proj(j, lo, hi):
        return jnp.dot(st[j]["hb"], win_ref[:, lo:hi], preferred_element_type=F32)

    def head(j):
        xc = x_ref[0, j * R:(j + 1) * R, :]
        ms = jnp.mean(xc * xc, axis=-1, keepdims=True)
        st[j]["hb"] = ((xc * lax.rsqrt(ms + EPS)) * pre_gain + shift).astype(BF16)
        st[j]["qkv"] = proj(j, c_q, c_za)

    def gmlp_in(j):
        v = _gelu_times_sqrt2(proj(j, c_vb, c_zb))
        mu = jnp.mean(v, axis=-1, keepdims=True)
        vc = v - mu
        var = jnp.mean(vc * vc, axis=-1, keepdims=True)
        st[j]["vn"] = (vc * lax.rsqrt(var + 2.0 * EPS) * lng_ref[...] + lnb_ref[...]).astype(BF16)
        st[j]["u"] = _gelu_times_sqrt2(proj(j, c_u, c_vb))

    def rope_and_scores(j):
        pos = pos_ref[pl.ds(b_idx, 1), j * R:(j + 1) * R].astype(F32)
        cos_rows, sin_rows = [], []
        for b in range(blk_per_sub):
            ang = invf * pos[:, b * BLOCK:(b + 1) * BLOCK]
            cs = jnp.cos(ang)
            sn = jnp.sin(ang)
            cos_rows.append(jnp.concatenate([cs, cs, cs, cs], axis=0).T)
            sin_rows.append(jnp.concatenate([-sn, sn, -sn, sn], axis=0).T)
        cos = jnp.concatenate(cos_rows, axis=0)
        sin_signed = jnp.concatenate(sin_rows, axis=0)
        qkv = st[j]["qkv"]
        q_tiles = _rope(qkv[:, c_q:c_k], cos, sin_signed, first_half)
        k_r = _rope(qkv[:, c_k:c_v], cos, sin_signed, first_half)[0]
        vt_new = qkv[:, c_v:c_za].T
        lo = BLOCK + j * R
        for g in range(N_KV_HEADS):
            k_scr[g, lo:lo + R, :] = k_r[:, g * HEAD_DIM:(g + 1) * HEAD_DIM].astype(BF16)
            vt_scr[g, :, lo:lo + R] = vt_new[g * HEAD_DIM:(g + 1) * HEAD_DIM, :].astype(BF16)
        q_heads = []
        for c in range(len(q_tiles)):
            qs = (q_tiles[c] * q_scale).astype(BF16)
            q_heads.append(qs[:, 0:HEAD_DIM])
            q_heads.append(qs[:, HEAD_DIM:2 * HEAD_DIM])
        ss = []
        for b in range(blk_per_sub):
            n = j * blk_per_sub + b
            band = slice(n * BLOCK, n * BLOCK + 2 * BLOCK)
            for g in range(N_KV_HEADS):
                kb = k_scr[g, band, :]
                q_stack = jnp.concatenate(
                    [q_heads[g * GQ + h][b * BLOCK:(b + 1) * BLOCK] for h in range(GQ)], axis=0)
                ss.append(lax.dot_general(kb, q_stack, (((1,), (1,)), ((), ())),
                                          preferred_element_type=F32))
        st[j]["scores"] = ss

    def pv_unit(j, idx):
        b, g = divmod(idx, N_KV_HEADS)
        n = j * blk_per_sub + b
        band = slice(n * BLOCK, n * BLOCK + 2 * BLOCK)
        s = st[j]["scores"][idx]
        valid = in_win & (kj >= first_lo) if n == 0 else in_win
        s = jnp.where(jnp.concatenate([valid] * GQ, axis=1), s, NEG)
        sink = jnp.concatenate(
            [jnp.full((1, BLOCK), sinks_ref[g * GQ + h] * LOG2E, F32) for h in range(GQ)], axis=1)
        m = jnp.maximum(jnp.max(s, axis=0, keepdims=True), sink)
        p = jnp.exp2(s - m)
        denom = jnp.sum(p, axis=0, keepdims=True) + jnp.exp2(sink - m)
        ot = jnp.dot(vt_scr[g, :, band], p.astype(BF16),
                     preferred_element_type=F32) * (1.0 / denom)
        for pair in range(GQ // 2):
            two = jnp.concatenate(
                [ot[:, (2 * pair) * BLOCK:(2 * pair + 1) * BLOCK],
                 ot[:, (2 * pair + 1) * BLOCK:(2 * pair + 2) * BLOCK]], axis=0)
            st[j]["attn"][b][g * (GQ // 2) + pair] = two.T

    def gating(j):
        rows_out = []
        for n in range(R // CHUNK):
            rows = slice(n * CHUNK, (n + 1) * CHUNK)
            cols_out = []
            for g in range(N_GROUPS):
                w_g = jnp.where(causal, ws_ref[g], 0.0).astype(BF16)
                bias = jnp.broadcast_to(bs_ref[:, g:g + 1], (CHUNK, group_w))
                cols = slice(g * group_w, (g + 1) * group_w)
                sv = jnp.dot(w_g, st[j]["vn"][rows, cols], preferred_element_type=F32) + bias
                cols_out.append(st[j]["u"][rows, cols] * sv)
            rows_out.append(jnp.concatenate(cols_out, axis=1))
        st[j]["yb"] = jnp.concatenate(rows_out, axis=0)

    def branch_proj(j):
        attn = jnp.concatenate([jnp.concatenate(row, axis=1) for row in st[j]["attn"]], axis=0)
        y_a = attn.astype(BF16) * _silu_of_half(st[j]["za"]).astype(BF16)
        st[j]["pa"] = jnp.dot(y_a, wpa_ref[...], preferred_element_type=F32)
        y_b = st[j]["yb"].astype(BF16) * _silu_of_half(st[j]["zb"]).astype(BF16)
        st[j]["pb"] = jnp.dot(y_b, wpb_ref[...], preferred_element_type=F32)

    def out_proj(j):
        merged2 = (_two_sigmoid_of_half(st[j]["ga"]).astype(BF16) * st[j]["pa"].astype(BF16)
                   + _two_sigmoid_of_half(st[j]["gb"]).astype(BF16) * st[j]["pb"].astype(BF16))
        y = jnp.dot(merged2, wout_ref[...], preferred_element_type=F32)
        ms_y = jnp.mean(y * y, axis=-1, keepdims=True)
        rows = slice(j * R, (j + 1) * R)
        o_ref[0, rows, :] = x_ref[0, rows, :] + (y * lax.rsqrt(ms_y + EPS)) * post_gain
        st[j].clear()

    n_units = blk_per_sub * N_KV_HEADS
    for j in range(n_sub + 1):
        if j < n_sub:
            head(j)
            gmlp_in(j)
        if j >= 1:
            out_proj(j - 1)
        if j < n_sub:
            rope_and_scores(j)
            st[j]["attn"] = [[None] * (N_Q_HEADS // 2) for _ in range(blk_per_sub)]
            st[j]["za"] = proj(j, c_za, c_u)
            st[j]["zb"] = proj(j, c_zb, c_g)
            st[j]["ga"] = proj(j, c_g, c_g + d_model)
            for idx in range(n_units):
                pv_unit(j, idx)
            gating(j)
            st[j]["gb"] = proj(j, c_g + d_model, c_g + 2 * d_model)
            branch_proj(j)

    k_scr[:, 0:BLOCK, :] = k_scr[:, T:T + BLOCK, :]
    vt_scr[:, :, 0:BLOCK] = vt_scr[:, :, T:T + BLOCK]


def _const_spec(shape):
    return pl.BlockSpec(shape, lambda b, s: (0,) * len(shape), pipeline_mode=pl.Buffered(1))


def _layer(x, ada, positions, g_pre, g_post, w_in, sinks, ln_v_g, ln_v_b, w_s, b_s,
           w_proj_a, w_proj_b, w_out):
    B, S, D = x.shape
    T = SEQ_TILE
    assert S % T == 0 and T % SUB_TILE == 0 and SUB_TILE % BLOCK == 0 and SUB_TILE % CHUNK == 0
    d_a = N_Q_HEADS * HEAD_DIM
    d_kv = N_KV_HEADS * HEAD_DIM
    d_b = w_proj_b.shape[0]
    d_in = w_in.shape[1]
    assert d_in == 2 * d_a + 2 * d_kv + 3 * d_b + 2 * D

    half = HEAD_DIM // 2
    inv_freq = ROPE_THETA ** (-jnp.arange(half, dtype=F32) / half)
    invf = jnp.broadcast_to(inv_freq[:, None], (half, LANES))

    c_za, c_u = d_a + 2 * d_kv, 2 * d_a + 2 * d_kv
    c_zb = c_u + 2 * d_b
    col = jnp.arange(d_in)
    halved = ((col >= c_za) & (col < c_u)) | (col >= c_zb)
    inv_sqrt2 = 1.0 / math.sqrt(2.0)
    col_scale = jnp.where(halved, 0.5, jnp.where((col >= c_u) & (col < c_zb), inv_sqrt2, 1.0))
    w_in_b = (w_in * col_scale.astype(F32)[None, :]).astype(BF16)
    w_out_b = (w_out * 0.5).astype(BF16)

    kern = functools.partial(_block_kernel, d_a=d_a, d_kv=d_kv, d_b=d_b, d_model=D)
    return pl.pallas_call(
        kern,
        out_shape=jax.ShapeDtypeStruct((B, S, D), x.dtype),
        grid=(B, S // T),
        in_specs=[
            pl.BlockSpec(memory_space=pltpu.SMEM),
            pl.BlockSpec((1, T, D), lambda b, s: (b, s, 0)),
            pl.BlockSpec((B, T), lambda b, s: (0, s)),
            _const_spec((B, 3 * D)),
            _const_spec((1, D)),
            _const_spec((1, D)),
            _const_spec((half, LANES)),
            _const_spec((D, d_in)),
            _const_spec((1, d_b)),
            _const_spec((1, d_b)),
            _const_spec((N_GROUPS, CHUNK, CHUNK)),
            _const_spec((CHUNK, N_GROUPS)),
            _const_spec((d_a, D)),
            _const_spec((d_b, D)),
            _const_spec((D, D)),
        ],
        out_specs=pl.BlockSpec((1, T, D), lambda b, s: (b, s, 0)),
        scratch_shapes=[
            pltpu.VMEM((N_KV_HEADS, BLOCK + T, HEAD_DIM), BF16),
            pltpu.VMEM((N_KV_HEADS, HEAD_DIM, BLOCK + T), BF16),
        ],
        compiler_params=pltpu.CompilerParams(
            dimension_semantics=("arbitrary", "arbitrary"),
            vmem_limit_bytes=VMEM_LIMIT_BYTES),
        name="hybrid_block",
    )(sinks, x, positions, ada,
      g_pre[None, :], g_post[None, :], invf, w_in_b,
      ln_v_g[None, :], ln_v_b[None, :], w_s, b_s.T,
      w_proj_a.astype(BF16), (w_proj_b * inv_sqrt2).astype(BF16), w_out_b)


def _ada(c, w_ada, b_ada):
    B, D = c.shape
    n_out = w_ada.shape[1]
    tn = n_out // 2
    return pl.pallas_call(
        _ada_kernel,
        out_shape=jax.ShapeDtypeStruct((B, n_out), F32),
        grid=(n_out // tn,),
        in_specs=[
            pl.BlockSpec((B, D), lambda j: (0, 0)),
            pl.BlockSpec((D, tn), lambda j: (0, j)),
            pl.BlockSpec((1, tn), lambda j: (0, j)),
        ],
        out_specs=pl.BlockSpec((B, tn), lambda j: (0, j)),
        compiler_params=pltpu.CompilerParams(dimension_semantics=("arbitrary",)),
        name="adaln_modulation",
    )(c, w_ada, b_ada[None, :])


def kernel(x, c, positions, w_ada, b_ada, g_pre, g_post, w_in, sinks, ln_v_g, ln_v_b, w_s, b_s,
           w_proj_a, w_proj_b, w_out):
    depth = w_in.shape[0]
    for l in range(depth):
        ada = _ada(c, w_ada[l], b_ada[l])
        x = _layer(x, ada, positions, g_pre[l], g_post[l], w_in[l], sinks[l], ln_v_g[l],
                   ln_v_b[l], w_s[l], b_s[l], w_proj_a[l], w_proj_b[l], w_out[l])
    return x
```

```python
import functools
import math

import jax
import jax.numpy as jnp
from jax import lax
from jax.experimental import pallas as pl
from jax.experimental.pallas import tpu as pltpu

HEAD_DIM = 64
N_Q_HEADS = 8
N_KV_HEADS = 2
GQ = N_Q_HEADS // N_KV_HEADS
WINDOW = 128
BLOCK = 128
ROPE_THETA = 10000.0
CHUNK = 128
N_GROUPS = 4
EPS = 1e-6
NEG = -1e30
LOG2E = 1.4426950408889634

LANES = 128
SEQ_TILE = 1024
SUB_TILE = 256
VMEM_LIMIT_BYTES = 44 * 1024 * 1024

BF16 = jnp.bfloat16
F32 = jnp.float32


def _ada_kernel(c_ref, w_ref, b_ref, o_ref):
    c = c_ref[...]
    c_act = c * jax.nn.sigmoid(c)
    o_ref[...] = jnp.dot(c_act, w_ref[...], preferred_element_type=F32) + b_ref[...]


def _two_sigmoid_of_half(h):
    return jnp.tanh(h) + 1.0


def _silu_of_half(h):
    return h * _two_sigmoid_of_half(h)


def _gelu_times_sqrt2(t):
    return t * (1.0 + lax.erf(t))


def _rope(t, cos, sin_signed, first_half):
    outs = []
    for c in range(t.shape[1] // LANES):
        tc = t[:, c * LANES:(c + 1) * LANES]
        rot = jnp.where(first_half,
                        pltpu.roll(tc, LANES - HEAD_DIM // 2, axis=1),
                        pltpu.roll(tc, HEAD_DIM // 2, axis=1))
        outs.append(tc * cos + rot * sin_signed)
    return outs


def _block_kernel(sinks_ref, x_ref, pos_ref, ada_ref, gpre_ref, gpost_ref, invf_ref,
                  win_ref, lng_ref, lnb_ref, ws_ref, bs_ref, wpa_ref, wpb_ref, wout_ref,
                  o_ref, k_scr, vt_scr, *, d_a, d_kv, d_b, d_model):
    T = x_ref.shape[1]
    R = SUB_TILE
    n_sub = T // R
    blk_per_sub = R // BLOCK
    s_idx = pl.program_id(1)

    c_q, c_k, c_v = 0, d_a, d_a + d_kv
    c_za = c_v + d_kv
    c_u = c_za + d_a
    c_vb = c_u + d_b
    c_zb = c_vb + d_b
    c_g = c_zb + d_b

    @pl.when(s_idx == 0)
    def _():
        k_scr[:, 0:BLOCK, :] = jnp.zeros((N_KV_HEADS, BLOCK, HEAD_DIM), BF16)
        vt_scr[:, :, 0:BLOCK] = jnp.zeros((N_KV_HEADS, HEAD_DIM, BLOCK), BF16)

    b_idx = pl.program_id(0)
    ada = ada_ref[pl.ds(b_idx, 1), :]
    shift = ada[:, 0:d_model]
    scale = ada[:, d_model:2 * d_model]
    gate = ada[:, 2 * d_model:3 * d_model]
    pre_gain = gpre_ref[...] * (1.0 + scale)
    post_gain = gate * gpost_ref[...]
    invf = invf_ref[...]
    lane = lax.broadcasted_iota(jnp.int32, (1, LANES), 1)
    first_half = (lane % HEAD_DIM) < (HEAD_DIM // 2)

    kj = lax.broadcasted_iota(jnp.int32, (2 * BLOCK, BLOCK), 0)
    qi = lax.broadcasted_iota(jnp.int32, (2 * BLOCK, BLOCK), 1)
    rel = qi + BLOCK - kj
    in_win = (rel >= 0) & (rel < WINDOW)
    first_lo = jnp.where(s_idx == 0, BLOCK, 0)
    ti = lax.broadcasted_iota(jnp.int32, (CHUNK, CHUNK), 0)
    si = lax.broadcasted_iota(jnp.int32, (CHUNK, CHUNK), 1)
    causal = si <= ti
    group_w = d_b // N_GROUPS
    assert group_w == CHUNK
    gate_bias = [jnp.broadcast_to(bs_ref[g:g + 1, :], (CHUNK, CHUNK)).T for g in range(N_GROUPS)]
    q_scale =LOG2E / math.sqrt(HEAD_DIM)

    st = [dict() for _ in range(n_sub)]

    def proj(j, lo, hi):
        return jnp.dot(st[j]["hb"], win_ref[:, lo:hi], preferred_element_type=F32)

    def head(j):
        xc = x_ref[0, j * R:(j + 1) * R, :]
        ms = jnp.mean(xc * xc, axis=-1, keepdims=True)
        st[j]["hb"] = ((xc * lax.rsqrt(ms + EPS)) * pre_gain + shift).astype(BF16)
        st[j]["qkv"] = proj(j, c_q, c_za)

    def gmlp_in(j):
        v = _gelu_times_sqrt2(proj(j, c_vb, c_zb))
        mu = jnp.mean(v, axis=-1, keepdims=True)
        vc = v - mu
        var = jnp.mean(vc * vc, axis=-1, keepdims=True)
        st[j]["vn"] = (vc * lax.rsqrt(var + 2.0 * EPS) * lng_ref[...] + lnb_ref[...]).astype(BF16)
        st[j]["u"] = _gelu_times_sqrt2(proj(j, c_u, c_vb))

    def rope_and_scores(j):
        pos = pos_ref[pl.ds(b_idx, 1), j * R:(j + 1) * R].astype(F32)
        cos_rows, sin_rows = [], []
        for b in range(blk_per_sub):
            ang = invf * pos[:, b * BLOCK:(b + 1) * BLOCK]
            cs = jnp.cos(ang)
            sn = jnp.sin(ang)
            cos_rows.append(jnp.concatenate([cs, cs, cs, cs], axis=0).T)
            sin_rows.append(jnp.concatenate([-sn, sn, -sn, sn], axis=0).T)
        cos = jnp.concatenate(cos_rows, axis=0)
        sin_signed = jnp.concatenate(sin_rows, axis=0)
        qkv = st[j]["qkv"]
        q_tiles = _rope(qkv[:, c_q:c_k], cos, sin_signed, first_half)
        k_r = _rope(qkv[:, c_k:c_v], cos, sin_signed, first_half)[0]
        vt_new = qkv[:, c_v:c_za].T
        lo = BLOCK + j * R
        for g in range(N_KV_HEADS):
            k_scr[g, lo:lo + R, :] = k_r[:, g * HEAD_DIM:(g + 1) * HEAD_DIM].astype(BF16)
            vt_scr[g, :, lo:lo + R] = vt_new[g * HEAD_DIM:(g + 1) * HEAD_DIM, :].astype(BF16)
        q_heads = []
        for c in range(len(q_tiles)):
            qs = (q_tiles[c] * q_scale).astype(BF16)
            q_heads.append(qs[:, 0:HEAD_DIM])
            q_heads.append(qs[:, HEAD_DIM:2 * HEAD_DIM])
        ss = []
        for b in range(blk_per_sub):
            n = j * blk_per_sub + b
            band = slice(n * BLOCK, n * BLOCK + 2 * BLOCK)
            for g in range(N_KV_HEADS):
                kb = k_scr[g, band, :]
                q_stack = jnp.concatenate(
                    [q_heads[g * GQ + h][b * BLOCK:(b + 1) * BLOCK] for h in range(GQ)], axis=0)
                ss.append(lax.dot_general(kb, q_stack, (((1,), (1,)), ((), ())),
                                          preferred_element_type=F32))
        st[j]["scores"] = ss

    def pv_unit(j, idx):
        b, g = divmod(idx, N_KV_HEADS)
        n = j * blk_per_sub + b
        band = slice(n * BLOCK, n * BLOCK + 2 * BLOCK)
        s = st[j]["scores"][idx]
        valid = in_win & (kj >= first_lo) if n == 0 else in_win
        s = jnp.where(jnp.concatenate([valid] * GQ, axis=1), s, NEG)
        sink = jnp.concatenate(
            [jnp.full((1, BLOCK), sinks_ref[g * GQ + h] * LOG2E, F32) for h in range(GQ)], axis=1)
        m = jnp.maximum(jnp.max(s, axis=0, keepdims=True), sink)
        p = jnp.exp2(s - m)
        denom = jnp.sum(p, axis=0, keepdims=True) + jnp.exp2(sink - m)
        ot = jnp.dot(vt_scr[g, :, band], p.astype(BF16),
                     preferred_element_type=F32) * (1.0 / denom)
        for pair in range(GQ // 2):
            two = jnp.concatenate(
                [ot[:, (2 * pair) * BLOCK:(2 * pair + 1) * BLOCK],
                 ot[:, (2 * pair + 1) * BLOCK:(2 * pair + 2) * BLOCK]], axis=0)
            st[j]["attn"][b][g * (GQ // 2) + pair] = two.T

    def gating(j):
        rows_out = []
        for n in range(R // CHUNK):
            rows = slice(n * CHUNK, (n + 1) * CHUNK)
            cols_out = []
            for g in range(N_GROUPS):
                w_g = jnp.where(causal, ws_ref[g], 0.0).astype(BF16)
                bias = gate_bias[g]
                cols = slice(g * group_w, (g + 1) * group_w)
                sv = jnp.dot(w_g, st[j]["vn"][rows, cols], preferred_element_type=F32) + bias
                cols_out.append(st[j]["u"][rows, cols] * sv)
            rows_out.append(jnp.concatenate(cols_out, axis=1))
        st[j]["yb"] = jnp.concatenate(rows_out, axis=0)

    def branch_proj(j):
        attn = jnp.concatenate([jnp.concatenate(row, axis=1) for row in st[j]["attn"]], axis=0)
        y_a = attn.astype(BF16) * _silu_of_half(st[j]["za"]).astype(BF16)
        st[j]["pa"] = jnp.dot(y_a, wpa_ref[...], preferred_element_type=F32)
        y_b = st[j]["yb"].astype(BF16) * _silu_of_half(st[j]["zb"]).astype(BF16)
        st[j]["pb"] = jnp.dot(y_b, wpb_ref[...], preferred_element_type=F32)

    def out_proj(j):
        merged2 = (_two_sigmoid_of_half(st[j]["ga"]).astype(BF16) * st[j]["pa"].astype(BF16)
                   + _two_sigmoid_of_half(st[j]["gb"]).astype(BF16) * st[j]["pb"].astype(BF16))
        y = jnp.dot(merged2, wout_ref[...], preferred_element_type=F32)
        ms_y = jnp.mean(y * y, axis=-1, keepdims=True)
        rows = slice(j * R, (j + 1) * R)
        o_ref[0, rows, :] = x_ref[0, rows, :] + (y * lax.rsqrt(ms_y + EPS)) * post_gain
        st[j].clear()

    n_units = blk_per_sub * N_KV_HEADS
    for j in range(n_sub + 1):
        if j < n_sub:
            head(j)
            gmlp_in(j)
        if j >= 1:
            out_proj(j - 1)
        if j < n_sub:
            rope_and_scores(j)
            st[j]["attn"] = [[None] * (N_Q_HEADS // 2) for _ in range(blk_per_sub)]
            st[j]["za"] = proj(j, c_za, c_u)
            st[j]["zb"] = proj(j, c_zb, c_g)
            st[j]["ga"] = proj(j, c_g, c_g + d_model)
            for idx in range(n_units):
                pv_unit(j, idx)
            gating(j)
            st[j]["gb"] = proj(j, c_g + d_model, c_g + 2 * d_model)
            branch_proj(j)

    k_scr[:, 0:BLOCK, :] = k_scr[:, T:T + BLOCK, :]
    vt_scr[:, :, 0:BLOCK] = vt_scr[:, :, T:T + BLOCK]


def _const_spec(shape):
    return pl.BlockSpec(shape, lambda b, s: (0,) * len(shape), pipeline_mode=pl.Buffered(1))


def _layer(x, ada, positions, g_pre, g_post, w_in, sinks, ln_v_g, ln_v_b, w_s, b_s,
           w_proj_a, w_proj_b, w_out):
    B, S, D = x.shape
    T = SEQ_TILE
    assert S % T == 0 and T % SUB_TILE == 0 and SUB_TILE % BLOCK == 0 and SUB_TILE % CHUNK == 0
    d_a = N_Q_HEADS * HEAD_DIM
    d_kv = N_KV_HEADS * HEAD_DIM
    d_b = w_proj_b.shape[0]
    d_in = w_in.shape[1]
    assert d_in == 2 * d_a + 2 * d_kv + 3 * d_b + 2 * D

    half = HEAD_DIM // 2
    inv_freq = ROPE_THETA ** (-jnp.arange(half, dtype=F32) / half)
    invf = jnp.broadcast_to(inv_freq[:, None], (half, LANES))

    c_za, c_u = d_a + 2 * d_kv, 2 * d_a + 2 * d_kv
    c_zb = c_u + 2 * d_b
    col = jnp.arange(d_in)
    halved = ((col >= c_za) & (col < c_u)) | (col >= c_zb)
    inv_sqrt2 = 1.0 / math.sqrt(2.0)
    col_scale = jnp.where(halved, 0.5, jnp.where((col >= c_u) & (col < c_zb), inv_sqrt2, 1.0))
    w_in_b = (w_in * col_scale.astype(F32)[None, :]).astype(BF16)
    w_out_b = (w_out * 0.5).astype(BF16)

    kern = functools.partial(_block_kernel, d_a=d_a, d_kv=d_kv, d_b=d_b, d_model=D)
    return pl.pallas_call(
        kern,
        out_shape=jax.ShapeDtypeStruct((B, S, D), x.dtype),
        grid=(B, S // T),
        in_specs=[
            pl.BlockSpec(memory_space=pltpu.SMEM),
            pl.BlockSpec((1, T, D), lambda b, s: (b, s, 0)),
            pl.BlockSpec((B, T), lambda b, s: (0, s)),
            _const_spec((B, 3 * D)),
            _const_spec((1, D)),
            _const_spec((1, D)),
            _const_spec((half, LANES)),
            _const_spec((D, d_in)),
            _const_spec((1, d_b)),
            _const_spec((1, d_b)),
            _const_spec((N_GROUPS, CHUNK, CHUNK)),
            _const_spec((N_GROUPS, CHUNK)),
            _const_spec((d_a, D)),
            _const_spec((d_b, D)),
            _const_spec((D, D)),
        ],
        out_specs=pl.BlockSpec((1, T, D), lambda b, s: (b, s, 0)),
        scratch_shapes=[
            pltpu.VMEM((N_KV_HEADS, BLOCK + T, HEAD_DIM), BF16),
            pltpu.VMEM((N_KV_HEADS, HEAD_DIM, BLOCK + T), BF16),
        ],
        compiler_params=pltpu.CompilerParams(
            dimension_semantics=("arbitrary", "arbitrary"),
            vmem_limit_bytes=VMEM_LIMIT_BYTES),
        name="hybrid_block",
    )(sinks, x, positions, ada,
      g_pre[None, :], g_post[None, :], invf, w_in_b,
      ln_v_g[None, :], ln_v_b[None, :], w_s, b_s,
      w_proj_a.astype(BF16), (w_proj_b * inv_sqrt2).astype(BF16), w_out_b)


def _ada(c, w_ada, b_ada):
    B, D = c.shape
    n_out = w_ada.shape[1]
    tn = n_out // 2
    return pl.pallas_call(
        _ada_kernel,
        out_shape=jax.ShapeDtypeStruct((B, n_out), F32),
        grid=(n_out // tn,),
        in_specs=[
            pl.BlockSpec((B, D), lambda j: (0, 0)),
            pl.BlockSpec((D, tn), lambda j: (0, j)),
            pl.BlockSpec((1, tn), lambda j: (0, j)),
        ],
        out_specs=pl.BlockSpec((B, tn), lambda j: (0, j)),
        compiler_params=pltpu.CompilerParams(dimension_semantics=("arbitrary",)),
        name="adaln_modulation",
    )(c, w_ada, b_ada[None, :])


def kernel(x, c, positions, w_ada, b_ada, g_pre, g_post, w_in, sinks, ln_v_g, ln_v_b, w_s, b_s,
           w_proj_a, w_proj_b, w_out):
    depth = w_in.shape[0]
    for l in range(depth):
        ada = _ada(c, w_ada[l], b_ada[l])
        x = _layer(x, ada, positions, g_pre[l], g_post[l], w_in[l], sinks[l], ln_v_g[l],
                   ln_v_b[l], w_s[l], b_s[l], w_proj_a[l], w_proj_b[l], w_out[l])
    return x
```

```python
import functools
import math

import jax
import jax.numpy as jnp
from jax import lax
from jax.experimental import pallas as pl
from jax.experimental.pallas import tpu as pltpu

HEAD_DIM = 64
N_Q_HEADS = 8
N_KV_HEADS = 2
GQ = N_Q_HEADS // N_KV_HEADS
WINDOW = 128
BLOCK = 128
ROPE_THETA = 10000.0
CHUNK = 128
N_GROUPS = 4
EPS = 1e-6
NEG = -1e30
LOG2E = 1.4426950408889634

LANES = 128
SEQ_TILE = 1024
SUB_TILE = 256
VMEM_LIMIT_BYTES = 44 * 1024 * 1024

BF16 = jnp.bfloat16
F32 = jnp.float32


def _ada_kernel(c_ref, w_ref, b_ref, o_ref):
    c = c_ref[...]
    c_act = c * jax.nn.sigmoid(c)
    o_ref[...] = jnp.dot(c_act, w_ref[...], preferred_element_type=F32) + b_ref[...]


def _two_sigmoid_of_half(h):
    return jnp.tanh(h) + 1.0


def _silu_of_half(h):
    return h * _two_sigmoid_of_half(h)


def _gelu_times_sqrt2(t):
    return t * (1.0 + lax.erf(t))


def _rope(t, cos, sin_signed, first_half):
    outs = []
    for c in range(t.shape[1] // LANES):
        tc = t[:, c * LANES:(c + 1) * LANES]
        rot = jnp.where(first_half,
                        pltpu.roll(tc, LANES - HEAD_DIM // 2, axis=1),
                        pltpu.roll(tc, HEAD_DIM // 2, axis=1))
        outs.append(tc * cos + rot * sin_signed)
    return outs


def _block_kernel(sinks_ref, x_ref, pos_ref, ada_ref, gpre_ref, gpost_ref, invf_ref,
                  win_ref, lng_ref, lnb_ref, ws_ref, bs_ref, wpa_ref, wpb_ref, wout_ref,
                  o_ref, k_scr, vt_scr, *, d_a, d_kv, d_b, d_model):
    T = x_ref.shape[1]
    R = SUB_TILE
    n_sub = T // R
    blk_per_sub = R // BLOCK
    s_idx = pl.program_id(1)

    c_q, c_k, c_v = 0, d_a, d_a + d_kv
    c_za = c_v + d_kv
    c_u = c_za + d_a
    c_vb = c_u + d_b
    c_zb = c_vb + d_b
    c_g = c_zb + d_b

    @pl.when(s_idx == 0)
    def _():
        k_scr[:, 0:BLOCK, :] = jnp.zeros((N_KV_HEADS, BLOCK, HEAD_DIM), BF16)
        vt_scr[:, :, 0:BLOCK] = jnp.zeros((N_KV_HEADS, HEAD_DIM, BLOCK), BF16)

    b_idx = pl.program_id(0)
    ada = ada_ref[pl.ds(b_idx, 1), :]
    shift = ada[:, 0:d_model]
    scale = ada[:, d_model:2 * d_model]
    gate = ada[:, 2 * d_model:3 * d_model]
    pre_gain = gpre_ref[...] * (1.0 + scale)
    post_gain = gate * gpost_ref[...]
    invf = invf_ref[...]
    lane = lax.broadcasted_iota(jnp.int32, (1, LANES), 1)
    first_half = (lane % HEAD_DIM) < (HEAD_DIM // 2)

    kj = lax.broadcasted_iota(jnp.int32, (2 * BLOCK, BLOCK), 0)
    qi = lax.broadcasted_iota(jnp.int32, (2 * BLOCK, BLOCK), 1)
    rel = qi + BLOCK - kj
    in_win = (rel >= 0) & (rel < WINDOW)
    first_lo = jnp.where(s_idx == 0, BLOCK, 0)
    ti = lax.broadcasted_iota(jnp.int32, (CHUNK, CHUNK), 0)
    si = lax.broadcasted_iota(jnp.int32, (CHUNK, CHUNK), 1)
    causal = si <= ti
    group_w = d_b // N_GROUPS
    assert group_w == CHUNK
    gate_bias = [jnp.broadcast_to(bs_ref[g:g + 1, :], (CHUNK, CHUNK)).T for g in range(N_GROUPS)]
    q_scale = LOG2E / math.sqrt(HEAD_DIM)

    st = [dict() for _ in range(n_sub)]

    def proj(j, lo, hi):
        return jnp.dot(st[j]["hb"], win_ref[:, lo:hi], preferred_element_type=F32)

    def head(j):
        xc = x_ref[0, j * R:(j + 1) * R, :]
        ms = jnp.mean(xc * xc, axis=-1, keepdims=True)
        st[j]["hb"] = ((xc * lax.rsqrt(ms + EPS)) * pre_gain + shift).astype(BF16)
        st[j]["qkv"] = proj(j, c_q, c_za)

    def gmlp_in(j):
        v = _gelu_times_sqrt2(proj(j, c_vb, c_zb))
        mu = jnp.mean(v, axis=-1, keepdims=True)
        vc = v - mu
        var = jnp.mean(vc * vc, axis=-1, keepdims=True)
        st[j]["vn"] = (vc * lax.rsqrt(var + 2.0 * EPS) * lng_ref[...] + lnb_ref[...]).astype(BF16)
        st[j]["u"] = _gelu_times_sqrt2(proj(j, c_u, c_vb))

    def rope_and_scores(j):
        pos = pos_ref[pl.ds(b_idx, 1), j * R:(j + 1) * R].astype(F32)
        cos_rows, sin_rows = [], []
        for b in range(blk_per_sub):
            ang = invf * pos[:, b * BLOCK:(b + 1) * BLOCK]
            cs = jnp.cos(ang)
            sn = jnp.sin(ang)
            cos_rows.append(jnp.concatenate([cs, cs, cs, cs], axis=0).T)
            sin_rows.append(jnp.concatenate([-sn, sn, -sn, sn], axis=0).T)
        cos = jnp.concatenate(cos_rows, axis=0)
        sin_signed = jnp.concatenate(sin_rows, axis=0)
        qkv = st[j]["qkv"]
        q_tiles = _rope(qkv[:, c_q:c_k], cos, sin_signed, first_half)
        k_r = _rope(qkv[:, c_k:c_v], cos, sin_signed, first_half)[0]
        vt_new = qkv[:, c_v:c_za].T
        lo = BLOCK + j * R
        for g in range(N_KV_HEADS):
            k_scr[g, lo:lo + R, :] = k_r[:, g * HEAD_DIM:(g + 1) * HEAD_DIM].astype(BF16)
            vt_scr[g, :, lo:lo + R] = vt_new[g * HEAD_DIM:(g + 1) * HEAD_DIM, :].astype(BF16)
        q_heads = []
        for c in range(len(q_tiles)):
            qs = (q_tiles[c] * q_scale).astype(BF16)
            q_heads.append(qs[:, 0:HEAD_DIM])
            q_heads.append(qs[:, HEAD_DIM:2 * HEAD_DIM])
        ss = []
        for b in range(blk_per_sub):
            n = j * blk_per_sub + b
            band = slice(n * BLOCK, n * BLOCK + 2 * BLOCK)
            for g in range(N_KV_HEADS):
                kb = k_scr[g, band, :]
                q_stack = jnp.concatenate(
                    [q_heads[g * GQ + h][b * BLOCK:(b + 1) * BLOCK] for h in range(GQ)], axis=0)
                ss.append(lax.dot_general(kb, q_stack, (((1,), (1,)), ((), ())),
                                          preferred_element_type=F32))
        st[j]["scores"] = ss

    def pv_unit(j, idx):
        b, g = divmod(idx, N_KV_HEADS)
        n = j * blk_per_sub + b
        band = slice(n * BLOCK, n * BLOCK + 2 * BLOCK)
        s = st[j]["scores"][idx]
        valid = in_win & (kj >= first_lo) if n == 0 else in_win
        s = jnp.where(jnp.concatenate([valid] * GQ, axis=1), s, NEG)
        sink = jnp.concatenate(
            [jnp.full((1, BLOCK), sinks_ref[g * GQ + h] * LOG2E, F32) for h in range(GQ)], axis=1)
        m = jnp.maximum(jnp.max(s, axis=0, keepdims=True), sink)
        p = jnp.exp2(s - m)
        denom = jnp.sum(p, axis=0, keepdims=True) + jnp.exp2(sink - m)
        ot = jnp.dot(vt_scr[g, :, band], p.astype(BF16),
                     preferred_element_type=F32) * (1.0 / denom)
        for pair in range(GQ // 2):
            two = jnp.concatenate(
                [ot[:, (2 * pair) * BLOCK:(2 * pair + 1) * BLOCK],
                 ot[:, (2 * pair + 1) * BLOCK:(2 * pair + 2) * BLOCK]], axis=0)
            st[j]["attn"][b][g * (GQ // 2) + pair] = two.T

    def gating(j):
        rows_out = []
        for n in range(R // CHUNK):
            rows = slice(n * CHUNK, (n + 1) * CHUNK)
            cols_out = []
            for g in range(N_GROUPS):
                w_g = jnp.where(causal, ws_ref[g], 0.0).astype(BF16)
                bias = gate_bias[g]
                cols = slice(g * group_w, (g + 1) * group_w)
                sv = jnp.dot(w_g, st[j]["vn"][rows, cols], preferred_element_type=F32) + bias
                cols_out.append(st[j]["u"][rows, cols] * sv)
            rows_out.append(jnp.concatenate(cols_out, axis=1))
        st[j]["yb"] = jnp.concatenate(rows_out, axis=0)

    def branch_proj(j):
        attn = jnp.concatenate([jnp.concatenate(row, axis=1) for row in st[j]["attn"]], axis=0)
        y_a = attn.astype(BF16) * _silu_of_half(st[j]["za"]).astype(BF16)
        st[j]["pa"] = jnp.dot(y_a, wpa_ref[...], preferred_element_type=F32)
        y_b = st[j]["yb"].astype(BF16) * _silu_of_half(st[j]["zb"]).astype(BF16)
        st[j]["pb"] = jnp.dot(y_b, wpb_ref[...], preferred_element_type=F32)

    def out_proj(j):
        merged2 = (_two_sigmoid_of_half(st[j]["ga"]).astype(BF16) * st[j]["pa"].astype(BF16)
                   + _two_sigmoid_of_half(st[j]["gb"]).astype(BF16) * st[j]["pb"].astype(BF16))
        y = jnp.dot(merged2, wout_ref[...], preferred_element_type=F32)
        ms_y = jnp.mean(y * y, axis=-1, keepdims=True)
        rows = slice(j * R, (j + 1) * R)
        o_ref[0, rows, :] = x_ref[0, rows, :] + (y * lax.rsqrt(ms_y + EPS)) * post_gain
        st[j].clear()

    n_units = blk_per_sub * N_KV_HEADS
    for j in range(n_sub + 1):
        if j < n_sub:
            head(j)
            gmlp_in(j)
        if j >= 1:
            out_proj(j - 1)
        if j < n_sub:
            rope_and_scores(j)
            st[j]["attn"] = [[None] * (N_Q_HEADS // 2) for _ in range(blk_per_sub)]
            st[j]["za"] = proj(j, c_za, c_u)
            st[j]["zb"] = proj(j, c_zb, c_g)
            st[j]["ga"] = proj(j, c_g, c_g + d_model)
            for idx in range(n_units):
                pv_unit(j, idx)
            gating(j)
            st[j]["gb"] = proj(j, c_g + d_model, c_g + 2 * d_model)
            branch_proj(j)

    k_scr[:, 0:BLOCK, :] = k_scr[:, T:T + BLOCK, :]
    vt_scr[:, :, 0:BLOCK] = vt_scr[:, :, T:T + BLOCK]


def _const_spec(shape):
    return pl.BlockSpec(shape, lambda b, s: (0,) * len(shape), pipeline_mode=pl.Buffered(1))


def _layer(x, ada, positions, g_pre, g_post, w_in, sinks, ln_v_g, ln_v_b, w_s, b_s,
           w_proj_a, w_proj_b, w_out):
    B, S, D = x.shape
    T = SEQ_TILE
    assert S % T == 0 and T % SUB_TILE == 0 and SUB_TILE % BLOCK == 0 and SUB_TILE % CHUNK == 0
    d_a = N_Q_HEADS * HEAD_DIM
    d_kv = N_KV_HEADS * HEAD_DIM
    d_b = w_proj_b.shape[0]
    d_in = w_in.shape[1]
    assert d_in == 2 * d_a + 2 * d_kv + 3 * d_b + 2 * D

    half = HEAD_DIM // 2
    inv_freq = ROPE_THETA ** (-jnp.arange(half, dtype=F32) / half)
    invf = jnp.broadcast_to(inv_freq[:, None], (half, LANES))

    c_za, c_u = d_a + 2 * d_kv, 2 * d_a + 2 * d_kv
    c_zb = c_u + 2 * d_b
    col = jnp.arange(d_in)
    halved = ((col >= c_za) & (col < c_u)) | (col >= c_zb)
    inv_sqrt2 = 1.0 / math.sqrt(2.0)
    col_scale = jnp.where(halved, 0.5, jnp.where((col >= c_u) & (col < c_zb), inv_sqrt2, 1.0))
    w_in_b = (w_in * col_scale.astype(F32)[None, :]).astype(BF16)
    w_out_b = (w_out * 0.5).astype(BF16)

    kern = functools.partial(_block_kernel, d_a=d_a, d_kv=d_kv, d_b=d_b, d_model=D)
    return pl.pallas_call(
        kern,
        out_shape=jax.ShapeDtypeStruct((B, S, D), x.dtype),
        grid=(B, S // T),
        in_specs=[
            pl.BlockSpec(memory_space=pltpu.SMEM),
            pl.BlockSpec((1, T, D), lambda b, s: (b, s, 0)),
            pl.BlockSpec((B, T), lambda b, s: (0, s)),
            _const_spec((B, 3 * D)),
            _const_spec((1, D)),
            _const_spec((1, D)),
            _const_spec((half, LANES)),
            _const_spec((D, d_in)),
            _const_spec((1, d_b)),
            _const_spec((1, d_b)),
            _const_spec((N_GROUPS, CHUNK, CHUNK)),
            _const_spec((N_GROUPS, CHUNK)),
            _const_spec((d_a, D)),
            _const_spec((d_b, D)),
            _const_spec((D, D)),
        ],
        out_specs=pl.BlockSpec((1, T, D), lambda b, s: (b, s, 0)),
        scratch_shapes=[
            pltpu.VMEM((N_KV_HEADS, BLOCK + T, HEAD_DIM), BF16),
            pltpu.VMEM((N_KV_HEADS, HEAD_DIM, BLOCK + T), BF16),
        ],
        compiler_params=pltpu.CompilerParams(
            dimension_semantics=("arbitrary", "arbitrary"),
            vmem_limit_bytes=VMEM_LIMIT_BYTES),
        name="hybrid_block",
    )(sinks, x, positions, ada,
      g_pre[None, :], g_post[None, :], invf, w_in_b,
      ln_v_g[None, :], ln_v_b[None, :], w_s, b_s,
      w_proj_a.astype(BF16), (w_proj_b * inv_sqrt2).astype(BF16), w_out_b)


def _ada(c, w_ada, b_ada):
    B, D = c.shape
    n_out = w_ada.shape[1]
    tn = n_out // 2
    return pl.pallas_call(
        _ada_kernel,
        out_shape=jax.ShapeDtypeStruct((B, n_out), F32),
        grid=(n_out // tn,),
        in_specs=[
            pl.BlockSpec((B, D), lambda j: (0, 0)),
            pl.BlockSpec((D, tn), lambda j: (0, j)),
            pl.BlockSpec((1, tn), lambda j: (0, j)),
        ],
        out_specs=pl.BlockSpec((B, tn), lambda j: (0, j)),
        compiler_params=pltpu.CompilerParams(dimension_semantics=("arbitrary",)),
        name="adaln_modulation",
    )(c, w_ada, b_ada[None, :])


def kernel(x, c, positions, w_ada, b_ada, g_pre, g_post, w_in, sinks, ln_v_g, ln_v_b, w_s, b_s,
           w_proj_a, w_proj_b, w_out):
    depth = w_in.shape[0]
    for l in range(depth):
        ada = _ada(c, w_ada[l], b_ada[l])
        x = _layer(x, ada, positions, g_pre[l], g_post[l], w_in[l], sinks[l], ln_v_g[l],
                   ln_v_b[l], w_s[l], b_s[l], w_proj_a[l], w_proj_b[l], w_out[l])
    return x
```

```python
import functools
import math

import jax
import jax.numpy as jnp
from jax import lax
from jax.experimental import pallas as pl
from jax.experimental.pallas import tpu as pltpu

HEAD_DIM = 64
N_Q_HEADS = 8
N_KV_HEADS = 2
GQ = N_Q_HEADS // N_KV_HEADS
WINDOW = 128
BLOCK = 128
ROPE_THETA = 10000.0
CHUNK = 128
N_GROUPS = 4
EPS = 1e-6
NEG = -1e30
LOG2E = 1.4426950408889634

LANES = 128
SEQ_TILE = 1024
SUB_TILE = 256
VMEM_LIMIT_BYTES = 44 * 1024 * 1024

BF16 = jnp.bfloat16
F32 = jnp.float32


def _ada_kernel(c_ref, w_ref, b_ref, o_ref):
    c = c_ref[...]
    c_act = c * jax.nn.sigmoid(c)
    o_ref[...] = jnp.dot(c_act, w_ref[...], preferred_element_type=F32) + b_ref[...]


def _two_sigmoid_of_half(h):
    return jnp.tanh(h) + 1.0


def _silu_of_half(h):
    return h * _two_sigmoid_of_half(h)


def _gelu_times_sqrt2(t):
    return t * (1.0 + lax.erf(t))


def _rope(t, cos, sin_signed, first_half):
    outs = []
    for c in range(t.shape[1] // LANES):
        tc = t[:, c * LANES:(c + 1) * LANES]
        rot = jnp.where(first_half,
                        pltpu.roll(tc, LANES - HEAD_DIM // 2, axis=1),
                        pltpu.roll(tc, HEAD_DIM // 2, axis=1))
        outs.append(tc * cos + rot * sin_signed)
    return outs


def _block_kernel(sinks_ref, x_ref, pos_ref, ada_ref, gpre_ref, gpost_ref, invf_ref,
                  win_ref, lng_ref, lnb_ref, ws_ref, bs_ref, wpa_ref, wpb_ref, wout_ref,
                  o_ref, k_scr, vt_scr, *, d_a, d_kv, d_b, d_model):
    T = x_ref.shape[1]
    R = SUB_TILE
    n_sub = T // R
    blk_per_sub = R // BLOCK
    s_idx = pl.program_id(1)

    c_q, c_k, c_v = 0, d_a, d_a + d_kv
    c_za = c_v + d_kv
    c_u = c_za + d_a
    c_vb = c_u + d_b
    c_zb = c_vb + d_b
    c_g = c_zb + d_b

    @pl.when(s_idx == 0)
    def _():
        k_scr[:, 0:BLOCK, :] = jnp.zeros((N_KV_HEADS, BLOCK, HEAD_DIM), BF16)
        vt_scr[:, :, 0:BLOCK] = jnp.zeros((N_KV_HEADS, HEAD_DIM, BLOCK), BF16)

    b_idx = pl.program_id(0)
    ada = ada_ref[pl.ds(b_idx, 1), :]
    shift = ada[:, 0:d_model]
    scale = ada[:, d_model:2 * d_model]
    gate = ada[:, 2 * d_model:3 * d_model]
    pre_gain = gpre_ref[...] * (1.0 + scale)
    post_gain = gate * gpost_ref[...]
    invf = invf_ref[...]
    lane = lax.broadcasted_iota(jnp.int32, (1, LANES), 1)
    first_half = (lane % HEAD_DIM) < (HEAD_DIM // 2)

    kj = lax.broadcasted_iota(jnp.int32, (2 * BLOCK, BLOCK), 0)
    qi = lax.broadcasted_iota(jnp.int32, (2 * BLOCK, BLOCK), 1)
    rel = qi + BLOCK - kj
    in_win = (rel >= 0) & (rel < WINDOW)
    first_lo = jnp.where(s_idx == 0, BLOCK, 0)
    ti = lax.broadcasted_iota(jnp.int32, (CHUNK, CHUNK), 0)
    si = lax.broadcasted_iota(jnp.int32, (CHUNK, CHUNK), 1)
    causal = si <= ti
    group_w = d_b // N_GROUPS
    assert group_w == CHUNK
    gate_bias = [jnp.broadcast_to(bs_ref[g:g + 1, :], (CHUNK, CHUNK)).T for g in range(N_GROUPS)]
    q_scale = LOG2E / math.sqrt(HEAD_DIM)

    st = [dict() for _ in range(n_sub)]

    def proj(j, lo, hi):
        return jnp.dot(st[j]["hb"], win_ref[:, lo:hi], preferred_element_type=F32)

    def head(j):
        xc = x_ref[0, j * R:(j + 1) * R, :]
        ms = jnp.mean(xc * xc, axis=-1, keepdims=True)
        st[j]["hb"] = ((xc * lax.rsqrt(ms + EPS)) * pre_gain + shift).astype(BF16)
        st[j]["qkv"] = proj(j, c_q, c_za)

    def gmlp_in(j):
        v = _gelu_times_sqrt2(proj(j, c_vb, c_zb))
        mu = jnp.mean(v, axis=-1, keepdims=True)
        vc = v - mu
        var = jnp.mean(vc * vc, axis=-1, keepdims=True)
        st[j]["vn"] = (vc * lax.rsqrt(var + 2.0 * EPS) * lng_ref[...] + lnb_ref[...]).astype(BF16)
        st[j]["u"] = _gelu_times_sqrt2(proj(j, c_u, c_vb))

    def rope_and_scores(j):
        pos = pos_ref[pl.ds(b_idx, 1), j * R:(j + 1) * R].astype(F32)
        cos_rows, sin_rows = [], []
        for b in range(blk_per_sub):
            ang = invf * pos[:, b * BLOCK:(b + 1) * BLOCK]
            cs = jnp.cos(ang)
            sn = jnp.sin(ang)
            cos_rows.append(jnp.concatenate([cs, cs, cs, cs], axis=0).T)
            sin_rows.append(jnp.concatenate([-sn, sn, -sn, sn], axis=0).T)
        cos = jnp.concatenate(cos_rows, axis=0)
        sin_signed = jnp.concatenate(sin_rows, axis=0)
        qkv = st[j]["qkv"]
        q_tiles = _rope(qkv[:, c_q:c_k], cos, sin_signed, first_half)
        k_r = _rope(qkv[:, c_k:c_v], cos, sin_signed, first_half)[0]
        vt_new = qkv[:, c_v:c_za].T
        lo = BLOCK + j * R
        for g in range(N_KV_HEADS):
            k_scr[g, lo:lo + R, :] = k_r[:, g * HEAD_DIM:(g + 1) * HEAD_DIM].astype(BF16)
            vt_scr[g, :, lo:lo + R] = vt_new[g * HEAD_DIM:(g + 1) * HEAD_DIM, :].astype(BF16)
        q_heads = []
        for c in range(len(q_tiles)):
            qs = (q_tiles[c] * q_scale).astype(BF16)
            q_heads.append(qs[:, 0:HEAD_DIM])
            q_heads.append(qs[:, HEAD_DIM:2 * HEAD_DIM])
        ss = []
        for b in range(blk_per_sub):
            n = j * blk_per_sub + b
            band = slice(n * BLOCK, n * BLOCK + 2 * BLOCK)
            for g in range(N_KV_HEADS):
                kb = k_scr[g, band, :]
                q_stack = jnp.concatenate(
                    [q_heads[g * GQ + h][b * BLOCK:(b + 1) * BLOCK] for h in range(GQ)], axis=0)
                ss.append(lax.dot_general(kb, q_stack, (((1,), (1,)), ((), ())),
                                          preferred_element_type=F32))
        st[j]["scores"] = ss

    def pv_unit(j, idx):
        b, g = divmod(idx, N_KV_HEADS)
        n = j * blk_per_sub + b
        band = slice(n * BLOCK, n * BLOCK + 2 * BLOCK)
        s = st[j]["scores"][idx]
        valid = in_win & (kj >= first_lo) if n == 0 else in_win
        s = jnp.where(jnp.concatenate([valid] * GQ, axis=1), s, NEG)
        sink = jnp.concatenate(
            [jnp.full((1, BLOCK), sinks_ref[g * GQ + h] * LOG2E, F32) for h in range(GQ)], axis=1)
        m = jnp.maximum(jnp.max(s, axis=0, keepdims=True), sink)
        p = jnp.exp2(s - m)
        denom = jnp.sum(p, axis=0, keepdims=True) + jnp.exp2(sink - m)
        ot = jnp.dot(vt_scr[g, :, band], p.astype(BF16),
                     preferred_element_type=F32) * (1.0 / denom)
        for pair in range(GQ // 2):
            two = jnp.concatenate(
                [ot[:, (2 * pair) * BLOCK:(2 * pair + 1) * BLOCK],
                 ot[:, (2 * pair + 1) * BLOCK:(2 * pair + 2) * BLOCK]], axis=0)
            st[j]["attn"][b][g * (GQ // 2) + pair] = two.T

    def gating(j):
        rows_out = []
        for n in range(R // CHUNK):
            rows = slice(n * CHUNK, (n + 1) * CHUNK)
            cols_out = []
            for g in range(N_GROUPS):
                w_g = jnp.where(causal, ws_ref[g], 0.0).astype(BF16)
                bias = gate_bias[g]
                cols = slice(g * group_w, (g + 1) * group_w)
                sv = jnp.dot(w_g, st[j]["vn"][rows, cols], preferred_element_type=F32) + bias
                cols_out.append(st[j]["u"][rows, cols] * sv)
            rows_out.append(jnp.concatenate(cols_out, axis=1))
        st[j]["yb"] = jnp.concatenate(rows_out, axis=0)

    def branch_proj(j):
        attn = jnp.concatenate([jnp.concatenate(row, axis=1) for row in st[j]["attn"]], axis=0)
        y_a = attn.astype(BF16) * _silu_of_half(st[j]["za"]).astype(BF16)
        st[j]["pa"] = jnp.dot(y_a, wpa_ref[...], preferred_element_type=F32)
        y_b = st[j]["yb"].astype(BF16) * _silu_of_half(st[j]["zb"]).astype(BF16)
        st[j]["pb"] = jnp.dot(y_b, wpb_ref[...], preferred_element_type=F32)

    def out_proj(j):
        merged2 = (_two_sigmoid_of_half(st[j]["ga"]).astype(BF16) * st[j]["pa"].astype(BF16)
                   + _two_sigmoid_of_half(st[j]["gb"]).astype(BF16) * st[j]["pb"].astype(BF16))
        y = jnp.dot(merged2, wout_ref[...], preferred_element_type=F32)
        ms_y = jnp.mean(y * y, axis=-1, keepdims=True)
        rows = slice(j * R, (j + 1) * R)
        o_ref[0, rows, :] = x_ref[0, rows, :] + (y * lax.rsqrt(ms_y + EPS)) * post_gain
        st[j].clear()

    n_units = blk_per_sub * N_KV_HEADS
    for j in range(n_sub + 1):
        if j < n_sub:
            head(j)
            gmlp_in(j)
        if j >= 1:
            out_proj(j - 1)
        if j < n_sub:
            rope_and_scores(j)
            st[j]["attn"] = [[None] * (N_Q_HEADS // 2) for _ in range(blk_per_sub)]
            st[j]["za"] = proj(j, c_za, c_u)
            st[j]["zb"] = proj(j, c_zb, c_g)
            st[j]["ga"] = proj(j, c_g, c_g + d_model)
            for idx in sorted(range(n_units), key=lambda i: (i % N_KV_HEADS, i)):
                pv_unit(j, idx)
            gating(j)
            st[j]["gb"] = proj(j, c_g + d_model, c_g + 2 * d_model)
            branch_proj(j)

    k_scr[:, 0:BLOCK, :] = k_scr[:, T:T + BLOCK, :]
    vt_scr[:, :, 0:BLOCK] = vt_scr[:, :, T:T + BLOCK]


def _const_spec(shape):
    return pl.BlockSpec(shape, lambda b, s: (0,) * len(shape), pipeline_mode=pl.Buffered(1))


def _layer(x, ada, positions, g_pre, g_post, w_in, sinks, ln_v_g, ln_v_b, w_s, b_s,
           w_proj_a, w_proj_b, w_out):
    B, S, D = x.shape
    T = SEQ_TILE
    assert S % T == 0 and T % SUB_TILE == 0 and SUB_TILE % BLOCK == 0 and SUB_TILE % CHUNK == 0
    d_a = N_Q_HEADS * HEAD_DIM
    d_kv = N_KV_HEADS * HEAD_DIM
    d_b = w_proj_b.shape[0]
    d_in = w_in.shape[1]
    assert d_in == 2 * d_a + 2 * d_kv + 3 * d_b + 2 * D

    half = HEAD_DIM // 2
    inv_freq = ROPE_THETA ** (-jnp.arange(half, dtype=F32) / half)
    invf = jnp.broadcast_to(inv_freq[:, None], (half, LANES))

    c_za, c_u = d_a + 2 * d_kv, 2 * d_a + 2 * d_kv
    c_zb = c_u + 2 * d_b
    col = jnp.arange(d_in)
    halved = ((col >= c_za) & (col < c_u)) | (col >= c_zb)
    inv_sqrt2 = 1.0 / math.sqrt(2.0)
    col_scale = jnp.where(halved, 0.5, jnp.where((col >= c_u) & (col < c_zb), inv_sqrt2, 1.0))
    w_in_b = (w_in * col_scale.astype(F32)[None, :]).astype(BF16)
    w_out_b = (w_out * 0.5).astype(BF16)

    kern = functools.partial(_block_kernel, d_a=d_a, d_kv=d_kv, d_b=d_b, d_model=D)
    return pl.pallas_call(
        kern,
        out_shape=jax.ShapeDtypeStruct((B, S, D), x.dtype),
        grid=(B, S // T),
        in_specs=[
            pl.BlockSpec(memory_space=pltpu.SMEM),
            pl.BlockSpec((1, T, D), lambda b, s: (b, s, 0)),
            pl.BlockSpec((B, T), lambda b, s: (0, s)),
            _const_spec((B, 3 * D)),
            _const_spec((1, D)),
            _const_spec((1, D)),
            _const_spec((half, LANES)),
            _const_spec((D, d_in)),
            _const_spec((1, d_b)),
            _const_spec((1, d_b)),
            _const_spec((N_GROUPS, CHUNK, CHUNK)),
            _const_spec((N_GROUPS, CHUNK)),
            _const_spec((d_a, D)),
            _const_spec((d_b, D)),
            _const_spec((D, D)),
        ],
        out_specs=pl.BlockSpec((1, T, D), lambda b, s: (b, s, 0)),
        scratch_shapes=[
            pltpu.VMEM((N_KV_HEADS, BLOCK + T, HEAD_DIM), BF16),
            pltpu.VMEM((N_KV_HEADS, HEAD_DIM, BLOCK + T), BF16),
        ],
        compiler_params=pltpu.CompilerParams(
            dimension_semantics=("arbitrary", "arbitrary"),
            vmem_limit_bytes=VMEM_LIMIT_BYTES),
        name="hybrid_block",
    )(sinks, x, positions, ada,
      g_pre[None, :], g_post[None, :], invf, w_in_b,
      ln_v_g[None, :], ln_v_b[None, :], w_s, b_s,
      w_proj_a.astype(BF16), (w_proj_b * inv_sqrt2).astype(BF16), w_out_b)


def _ada(c, w_ada, b_ada):
    B, D = c.shape
    n_out = w_ada.shape[1]
    tn = n_out // 2
    return pl.pallas_call(
        _ada_kernel,
        out_shape=jax.ShapeDtypeStruct((B, n_out), F32),
        grid=(n_out // tn,),
        in_specs=[
            pl.BlockSpec((B, D), lambda j: (0, 0)),
            pl.BlockSpec((D, tn), lambda j: (0, j)),
            pl.BlockSpec((1, tn), lambda j: (0, j)),
        ],
        out_specs=pl.BlockSpec((B, tn), lambda j: (0, j)),
        compiler_params=pltpu.CompilerParams(dimension_semantics=("arbitrary",)),
        name="adaln_modulation",
    )(c, w_ada, b_ada[None, :])


def kernel(x, c, positions, w_ada, b_ada, g_pre, g_post, w_in, sinks, ln_v_g, ln_v_b, w_s, b_s,
           w_proj_a, w_proj_b, w_out):
    depth = w_in.shape[0]
    for l in range(depth):
        ada = _ada(c, w_ada[l], b_ada[l])
        x = _layer(x, ada, positions, g_pre[l], g_post[l], w_in[l], sinks[l], ln_v_g[l],
                   ln_v_b[l], w_s[l], b_s[l], w_proj_a[l], w_proj_b[l], w_out[l])
    return x
```

```python
import functools
import math

import jax
import jax.numpy as jnp
from jax import lax
from jax.experimental import pallas as pl
from jax.experimental.pallas import tpu as pltpu

HEAD_DIM = 64
N_Q_HEADS = 8
N_KV_HEADS = 2
GQ = N_Q_HEADS // N_KV_HEADS
WINDOW = 128
BLOCK = 128
ROPE_THETA = 10000.0
CHUNK = 128
N_GROUPS = 4
EPS = 1e-6
NEG = -1e30
LOG2E = 1.4426950408889634

LANES = 128
SEQ_TILE = 1024
SUB_TILE = 256
VMEM_LIMIT_BYTES = 44 * 1024 * 1024

BF16 = jnp.bfloat16
F32 = jnp.float32


def _ada_kernel(c_ref, w_ref, b_ref, o_ref):
    c = c_ref[...]
    c_act = c * jax.nn.sigmoid(c)
    o_ref[...] = jnp.dot(c_act, w_ref[...], preferred_element_type=F32) + b_ref[...]


def _two_sigmoid_of_half(h):
    return jnp.tanh(h) + 1.0


def _silu_of_half(h):
    return h * _two_sigmoid_of_half(h)


def _gelu_times_sqrt2(t):
    return t * (1.0 + lax.erf(t))


def _rope(t, cos, sin_signed, first_half):
    outs = []
    for c in range(t.shape[1] // LANES):
        tc = t[:, c * LANES:(c + 1) * LANES]
        rot = jnp.where(first_half,
                        pltpu.roll(tc, LANES - HEAD_DIM // 2, axis=1),
                        pltpu.roll(tc, HEAD_DIM // 2, axis=1))
        outs.append(tc * cos + rot * sin_signed)
    return outs


def _out_kernel(m_ref, x_ref, ada_ref, gpost_ref, wout_ref, o_ref, *, d_model):
    gate = ada_ref[pl.ds(pl.program_id(0), 1), 2 * d_model:3 * d_model]
    post_gain = gate * gpost_ref[...]
    R = SUB_TILE
    for c in range(x_ref.shape[1] // R):
        rows = slice(c * R, (c + 1) * R)
        y = jnp.dot(m_ref[0, rows, :], wout_ref[...], preferred_element_type=F32)
        ms_y = jnp.mean(y * y, axis=-1, keepdims=True)
        o_ref[0, rows, :] = x_ref[0, rows, :] + (y * lax.rsqrt(ms_y + EPS)) * post_gain


def _block_kernel(sinks_ref, x_ref, pos_ref, ada_ref, gpre_ref, invf_ref,
                  win_ref, lng_ref, lnb_ref, ws_ref, bs_ref, wpa_ref, wpb_ref,
                  o_ref, k_scr, vt_scr, *, d_a, d_kv, d_b, d_model):
    T = x_ref.shape[1]
    R = SUB_TILE
    n_sub = T // R
    blk_per_sub = R // BLOCK
    s_idx = pl.program_id(1)

    c_q, c_k, c_v = 0, d_a, d_a + d_kv
    c_za = c_v + d_kv
    c_u = c_za + d_a
    c_vb = c_u + d_b
    c_zb = c_vb + d_b
    c_g = c_zb + d_b

    @pl.when(s_idx == 0)
    def _():
        k_scr[:, 0:BLOCK, :] = jnp.zeros((N_KV_HEADS, BLOCK, HEAD_DIM), BF16)
        vt_scr[:, :, 0:BLOCK] = jnp.zeros((N_KV_HEADS, HEAD_DIM, BLOCK), BF16)

    b_idx = pl.program_id(0)
    ada = ada_ref[pl.ds(b_idx, 1), :]
    shift = ada[:, 0:d_model]
    scale = ada[:, d_model:2 * d_model]
    pre_gain = gpre_ref[...] * (1.0 + scale)
    invf = invf_ref[...]
    lane = lax.broadcasted_iota(jnp.int32, (1, LANES), 1)
    first_half = (lane % HEAD_DIM) < (HEAD_DIM // 2)

    kj = lax.broadcasted_iota(jnp.int32, (2 * BLOCK, BLOCK), 0)
    qi = lax.broadcasted_iota(jnp.int32, (2 * BLOCK, BLOCK), 1)
    rel = qi + BLOCK - kj
    in_win = (rel >= 0) & (rel < WINDOW)
    first_lo = jnp.where(s_idx == 0, BLOCK, 0)
    ti = lax.broadcasted_iota(jnp.int32, (CHUNK, CHUNK), 0)
    si = lax.broadcasted_iota(jnp.int32, (CHUNK, CHUNK), 1)
    causal = si <= ti
    group_w = d_b // N_GROUPS
    assert group_w == CHUNK
    gate_bias = [jnp.broadcast_to(bs_ref[g:g + 1, :], (CHUNK, CHUNK)).T for g in range(N_GROUPS)]
    q_scale = LOG2E / math.sqrt(HEAD_DIM)

    st = [dict() for _ in range(n_sub)]

    def proj(j, lo, hi):
        return jnp.dot(st[j]["hb"], win_ref[:, lo:hi], preferred_element_type=F32)

    def head(j):
        xc = x_ref[0, j * R:(j + 1) * R, :]
        ms = jnp.mean(xc * xc, axis=-1, keepdims=True)
        st[j]["hb"] = ((xc * lax.rsqrt(ms + EPS)) * pre_gain + shift).astype(BF16)
        st[j]["qkv"] = proj(j, c_q, c_za)

    def gmlp_in(j):
        v = _gelu_times_sqrt2(proj(j, c_vb, c_zb))
        mu = jnp.mean(v, axis=-1, keepdims=True)
        vc = v - mu
        var = jnp.mean(vc * vc, axis=-1, keepdims=True)
        st[j]["vn"] = (vc * lax.rsqrt(var + 2.0 * EPS) * lng_ref[...] + lnb_ref[...]).astype(BF16)
        st[j]["u"] = _gelu_times_sqrt2(proj(j, c_u, c_vb))

    def rope_and_scores(j):
        pos = pos_ref[pl.ds(b_idx, 1), j * R:(j + 1) * R].astype(F32)
        cos_rows, sin_rows = [], []
        for b in range(blk_per_sub):
            ang = invf * pos[:, b * BLOCK:(b + 1) * BLOCK]
            cs = jnp.cos(ang)
            sn = jnp.sin(ang)
            cos_rows.append(jnp.concatenate([cs, cs, cs, cs], axis=0).T)
            sin_rows.append(jnp.concatenate([-sn, sn, -sn, sn], axis=0).T)
        cos = jnp.concatenate(cos_rows, axis=0)
        sin_signed = jnp.concatenate(sin_rows, axis=0)
        qkv = st[j]["qkv"]
        q_tiles = _rope(qkv[:, c_q:c_k], cos, sin_signed, first_half)
        k_r = _rope(qkv[:, c_k:c_v], cos, sin_signed, first_half)[0]
        vt_new = qkv[:, c_v:c_za].T
        lo = BLOCK + j * R
        for g in range(N_KV_HEADS):
            k_scr[g, lo:lo + R, :] = k_r[:, g * HEAD_DIM:(g + 1) * HEAD_DIM].astype(BF16)
            vt_scr[g, :, lo:lo + R] = vt_new[g * HEAD_DIM:(g + 1) * HEAD_DIM, :].astype(BF16)
        q_heads = []
        for c in range(len(q_tiles)):
            qs = (q_tiles[c] * q_scale).astype(BF16)
            q_heads.append(qs[:, 0:HEAD_DIM])
            q_heads.append(qs[:, HEAD_DIM:2 * HEAD_DIM])
        ss = []
        for b in range(blk_per_sub):
            n = j * blk_per_sub + b
            band = slice(n * BLOCK, n * BLOCK + 2 * BLOCK)
            for g in range(N_KV_HEADS):
                kb = k_scr[g, band, :]
                q_stack = jnp.concatenate(
                    [q_heads[g * GQ + h][b * BLOCK:(b + 1) * BLOCK] for h in range(GQ)], axis=0)
                ss.append(lax.dot_general(kb, q_stack, (((1,), (1,)), ((), ())),
                                          preferred_element_type=F32))
        st[j]["scores"] = ss

    def pv_unit(j, idx):
        b, g = divmod(idx, N_KV_HEADS)
        n = j * blk_per_sub + b
        band = slice(n * BLOCK, n * BLOCK + 2 * BLOCK)
        s = st[j]["scores"][idx]
        valid = in_win & (kj >= first_lo) if n == 0 else in_win
        s = jnp.where(jnp.concatenate([valid] * GQ, axis=1), s, NEG)
        sink = jnp.concatenate(
            [jnp.full((1, BLOCK), sinks_ref[g * GQ + h] * LOG2E, F32) for h in range(GQ)], axis=1)
        m = jnp.maximum(jnp.max(s, axis=0, keepdims=True), sink)
        p = jnp.exp2(s - m)
        denom = jnp.sum(p, axis=0, keepdims=True) + jnp.exp2(sink - m)
        ot = jnp.dot(vt_scr[g, :, band], p.astype(BF16),
                     preferred_element_type=F32) * (1.0 / denom)
        for pair in range(GQ // 2):
            two = jnp.concatenate(
                [ot[:, (2 * pair) * BLOCK:(2 * pair + 1) * BLOCK],
                 ot[:, (2 * pair + 1) * BLOCK:(2 * pair + 2) * BLOCK]], axis=0)
            st[j]["attn"][b][g * (GQ // 2) + pair] = two.T

    def gating(j):
        rows_out = []
        for n in range(R // CHUNK):
            rows = slice(n * CHUNK, (n + 1) * CHUNK)
            cols_out = []
            for g in range(N_GROUPS):
                w_g = jnp.where(causal, ws_ref[g], 0.0).astype(BF16)
                bias = gate_bias[g]
                cols = slice(g * group_w, (g + 1) * group_w)
                sv = jnp.dot(w_g, st[j]["vn"][rows, cols], preferred_element_type=F32) + bias
                cols_out.append(st[j]["u"][rows, cols] * sv)
            rows_out.append(jnp.concatenate(cols_out, axis=1))
        st[j]["yb"] = jnp.concatenate(rows_out, axis=0)

    def branch_proj(j):
        attn = jnp.concatenate([jnp.concatenate(row, axis=1) for row in st[j]["attn"]], axis=0)
        y_a = attn.astype(BF16) * _silu_of_half(st[j]["za"]).astype(BF16)
        st[j]["pa"] = jnp.dot(y_a, wpa_ref[...], preferred_element_type=F32)
        y_b = st[j]["yb"].astype(BF16) * _silu_of_half(st[j]["zb"]).astype(BF16)
        st[j]["pb"] = jnp.dot(y_b, wpb_ref[...], preferred_element_type=F32)

    def out_proj(j):
        merged2 = (_two_sigmoid_of_half(st[j]["ga"]).astype(BF16) * st[j]["pa"].astype(BF16)
                   + _two_sigmoid_of_half(st[j]["gb"]).astype(BF16) * st[j]["pb"].astype(BF16))
        o_ref[0, j * R:(j + 1) * R, :] = merged2
        st[j].clear()

    n_units = blk_per_sub * N_KV_HEADS
    for j in range(n_sub + 1):
        if j < n_sub:
            head(j)
            gmlp_in(j)
        if j >= 1:
            out_proj(j - 1)
        if j < n_sub:
            rope_and_scores(j)
            st[j]["attn"] = [[None] * (N_Q_HEADS // 2) for _ in range(blk_per_sub)]
            st[j]["za"] = proj(j, c_za, c_u)
            st[j]["zb"] = proj(j, c_zb, c_g)
            st[j]["ga"] = proj(j, c_g, c_g + d_model)
            for idx in range(n_units):
                pv_unit(j, idx)
            gating(j)
            st[j]["gb"] = proj(j, c_g + d_model, c_g + 2 * d_model)
            branch_proj(j)

    k_scr[:, 0:BLOCK, :] = k_scr[:, T:T + BLOCK, :]
    vt_scr[:, :, 0:BLOCK] = vt_scr[:, :, T:T + BLOCK]


def _const_spec(shape):
    return pl.BlockSpec(shape, lambda b, s: (0,) * len(shape), pipeline_mode=pl.Buffered(1))


def _layer(x, ada, positions, g_pre, g_post, w_in, sinks, ln_v_g, ln_v_b, w_s, b_s,
           w_proj_a, w_proj_b, w_out):
    B, S, D = x.shape
    T = SEQ_TILE
    assert S % T == 0 and T % SUB_TILE == 0 and SUB_TILE % BLOCK == 0 and SUB_TILE % CHUNK == 0
    d_a = N_Q_HEADS * HEAD_DIM
    d_kv = N_KV_HEADS * HEAD_DIM
    d_b = w_proj_b.shape[0]
    d_in = w_in.shape[1]
    assert d_in == 2 * d_a + 2 * d_kv + 3 * d_b + 2 * D

    half = HEAD_DIM // 2
    inv_freq = ROPE_THETA ** (-jnp.arange(half, dtype=F32) / half)
    invf = jnp.broadcast_to(inv_freq[:, None], (half, LANES))

    c_za, c_u = d_a + 2 * d_kv, 2 * d_a + 2 * d_kv
    c_zb = c_u + 2 * d_b
    col = jnp.arange(d_in)
    halved = ((col >= c_za) & (col < c_u)) | (col >= c_zb)
    inv_sqrt2 = 1.0 / math.sqrt(2.0)
    col_scale = jnp.where(halved, 0.5, jnp.where((col >= c_u) & (col < c_zb), inv_sqrt2, 1.0))
    w_in_b = (w_in * col_scale.astype(F32)[None, :]).astype(BF16)
    w_out_b = (w_out * 0.5).astype(BF16)

    kern = functools.partial(_block_kernel, d_a=d_a, d_kv=d_kv, d_b=d_b, d_model=D)
    merged2 = pl.pallas_call(
        kern,
        out_shape=jax.ShapeDtypeStruct((B, S, D), BF16),
        grid=(B, S // T),
        in_specs=[
            pl.BlockSpec(memory_space=pltpu.SMEM),
            pl.BlockSpec((1, T, D), lambda b, s: (b, s, 0)),
            pl.BlockSpec((B, T), lambda b, s: (0, s)),
            _const_spec((B, 3 * D)),
            _const_spec((1, D)),
            _const_spec((half, LANES)),
            _const_spec((D, d_in)),
            _const_spec((1, d_b)),
            _const_spec((1, d_b)),
            _const_spec((N_GROUPS, CHUNK, CHUNK)),
            _const_spec((N_GROUPS, CHUNK)),
            _const_spec((d_a, D)),
            _const_spec((d_b, D)),
        ],
        out_specs=pl.BlockSpec((1, T, D), lambda b, s: (b, s, 0)),
        scratch_shapes=[
            pltpu.VMEM((N_KV_HEADS, BLOCK + T, HEAD_DIM), BF16),
            pltpu.VMEM((N_KV_HEADS, HEAD_DIM, BLOCK + T), BF16),
        ],
        compiler_params=pltpu.CompilerParams(
            dimension_semantics=("arbitrary", "arbitrary"),
            vmem_limit_bytes=VMEM_LIMIT_BYTES),
        name="hybrid_block",
    )(sinks, x, positions, ada,
      g_pre[None, :], invf, w_in_b,
      ln_v_g[None, :], ln_v_b[None, :], w_s, b_s,
      w_proj_a.astype(BF16), (w_proj_b * inv_sqrt2).astype(BF16))

    return pl.pallas_call(
        functools.partial(_out_kernel, d_model=D),
        out_shape=jax.ShapeDtypeStruct((B, S, D), x.dtype),
        grid=(B, S // T),
        in_specs=[
            pl.BlockSpec((1, T, D), lambda b, s: (b, s, 0)),
            pl.BlockSpec((1, T, D), lambda b, s: (b, s, 0)),
            _const_spec((B, 3 * D)),
            _const_spec((1, D)),
            _const_spec((D, D)),
        ],
        out_specs=pl.BlockSpec((1, T, D), lambda b, s: (b, s, 0)),
        compiler_params=pltpu.CompilerParams(
            dimension_semantics=("arbitrary", "arbitrary"),
            vmem_limit_bytes=VMEM_LIMIT_BYTES),
        name="out_block",
    )(merged2, x, ada, g_post[None, :], w_out_b)


def _ada(c, w_ada, b_ada):
    B, D = c.shape
    n_out = w_ada.shape[1]
    tn = n_out // 2
    return pl.pallas_call(
        _ada_kernel,
        out_shape=jax.ShapeDtypeStruct((B, n_out), F32),
        grid=(n_out // tn,),
        in_specs=[
            pl.BlockSpec((B, D), lambda j: (0, 0)),
            pl.BlockSpec((D, tn), lambda j: (0, j)),
            pl.BlockSpec((1, tn), lambda j: (0, j)),
        ],
        out_specs=pl.BlockSpec((B, tn), lambda j: (0, j)),
        compiler_params=pltpu.CompilerParams(dimension_semantics=("arbitrary",)),
        name="adaln_modulation",
    )(c, w_ada, b_ada[None, :])


def kernel(x, c, positions, w_ada, b_ada, g_pre, g_post, w_in, sinks, ln_v_g, ln_v_b, w_s, b_s,
           w_proj_a, w_proj_b, w_out):
    depth = w_in.shape[0]
    for l in range(depth):
        ada = _ada(c, w_ada[l], b_ada[l])
        x = _layer(x, ada, positions, g_pre[l], g_post[l], w_in[l], sinks[l], ln_v_g[l],
                   ln_v_b[l], w_s[l], b_s[l], w_proj_a[l], w_proj_b[l], w_out[l])
    return x
```

```python
import functools
import math

import jax
import jax.numpy as jnp
from jax import lax
from jax.experimental import pallas as pl
from jax.experimental.pallas import tpu as pltpu

HEAD_DIM = 64
N_Q_HEADS = 8
N_KV_HEADS = 2
GQ = N_Q_HEADS // N_KV_HEADS
WINDOW = 128
BLOCK = 128
ROPE_THETA = 10000.0
CHUNK = 128
N_GROUPS = 4
EPS = 1e-6
NEG = -1e30
LOG2E = 1.4426950408889634

LANES = 128
SEQ_TILE = 1024
SUB_TILE = 256
VMEM_LIMIT_BYTES = 44 * 1024 * 1024

BF16 = jnp.bfloat16
F32 = jnp.float32


def _ada_kernel(c_ref, w_ref, b_ref, o_ref):
    c = c_ref[...]
    c_act = c * jax.nn.sigmoid(c)
    o_ref[...] = jnp.dot(c_act, w_ref[...], preferred_element_type=F32) + b_ref[...]


def _two_sigmoid_of_half(h):
    return jnp.tanh(h) + 1.0


def _silu_of_half(h):
    return h * _two_sigmoid_of_half(h)


def _gelu_times_sqrt2(t):
    return t * (1.0 + lax.erf(t))


def _rope(t, cos, sin_signed, first_half):
    outs = []
    for c in range(t.shape[1] // LANES):
        tc = t[:, c * LANES:(c + 1) * LANES]
        rot = jnp.where(first_half,
                        pltpu.roll(tc, LANES - HEAD_DIM // 2, axis=1),
                        pltpu.roll(tc, HEAD_DIM // 2, axis=1))
        outs.append(tc * cos + rot * sin_signed)
    return outs


def _block_kernel(sinks_ref, x_ref, pos_ref, ada_ref, gpre_ref, gpost_ref, invf_ref,
                  win_ref, lng_ref, lnb_ref, ws_ref, bs_ref, wpa_ref, wpb_ref, wout_ref,
                  o_ref, k_scr, vt_scr, *, d_a, d_kv, d_b, d_model):
    T = x_ref.shape[1]
    R = SUB_TILE
    n_sub = T // R
    blk_per_sub = R // BLOCK
    s_idx = pl.program_id(1)

    c_q, c_k, c_v = 0, d_a, d_a + d_kv
    c_za = c_v + d_kv
    c_u = c_za + d_a
    c_vb = c_u + d_b
    c_zb = c_vb + d_b
    c_g = c_zb + d_b

    @pl.when(s_idx == 0)
    def _():
        k_scr[:, 0:BLOCK, :] = jnp.zeros((N_KV_HEADS, BLOCK, HEAD_DIM), BF16)
        vt_scr[:, :, 0:BLOCK] = jnp.zeros((N_KV_HEADS, HEAD_DIM, BLOCK), BF16)

    b_idx = pl.program_id(0)
    ada = ada_ref[pl.ds(b_idx, 1), :]
    shift = ada[:, 0:d_model]
    scale = ada[:, d_model:2 * d_model]
    gate = ada[:, 2 * d_model:3 * d_model]
    pre_gain = gpre_ref[...] * (1.0 + scale)
    post_gain = gate * gpost_ref[...]
    invf = invf_ref[...]
    lane = lax.broadcasted_iota(jnp.int32, (1, LANES), 1)
    first_half = (lane % HEAD_DIM) < (HEAD_DIM // 2)

    kj = lax.broadcasted_iota(jnp.int32, (2 * BLOCK, BLOCK), 0)
    qi = lax.broadcasted_iota(jnp.int32, (2 * BLOCK, BLOCK), 1)
    rel = qi + BLOCK - kj
    in_win = (rel >= 0) & (rel < WINDOW)
    first_lo = jnp.where(s_idx == 0, BLOCK, 0)
    ti = lax.broadcasted_iota(jnp.int32, (CHUNK, CHUNK), 0)
    si = lax.broadcasted_iota(jnp.int32, (CHUNK, CHUNK), 1)
    causal = si <= ti
    group_w = d_b // N_GROUPS
    assert group_w == CHUNK
    gate_bias = [jnp.broadcast_to(bs_ref[g:g + 1, :], (CHUNK, CHUNK)).T for g in range(N_GROUPS)]
    q_scale = LOG2E / math.sqrt(HEAD_DIM)

    st = [dict() for _ in range(n_sub)]

    def proj(j, lo, hi):
        return jnp.dot(st[j]["hb"], win_ref[:, lo:hi], preferred_element_type=F32)

    def head(j):
        xc = x_ref[0, j * R:(j + 1) * R, :]
        ms = jnp.mean(xc * xc, axis=-1, keepdims=True)
        st[j]["hb"] = ((xc * lax.rsqrt(ms + EPS)) * pre_gain + shift).astype(BF16)
        st[j]["qkv"] = proj(j, c_q, c_za)

    def gmlp_in(j):
        v = _gelu_times_sqrt2(proj(j, c_vb, c_zb))
        mu = jnp.mean(v, axis=-1, keepdims=True)
        vc = v - mu
        var = jnp.mean(vc * vc, axis=-1, keepdims=True)
        st[j]["vn"] = (vc * lax.rsqrt(var + 2.0 * EPS) * lng_ref[...] + lnb_ref[...]).astype(BF16)
        st[j]["u"] = _gelu_times_sqrt2(proj(j, c_u, c_vb))

    def rope_and_scores(j):
        pos = pos_ref[pl.ds(b_idx, 1), j * R:(j + 1) * R].astype(F32)
        cos_rows, sin_rows = [], []
        for b in range(blk_per_sub):
            ang = invf * pos[:, b * BLOCK:(b + 1) * BLOCK]
            cs = jnp.cos(ang)
            sn = jnp.sin(ang)
            cos_rows.append(jnp.concatenate([cs, cs, cs, cs], axis=0).T)
            sin_rows.append(jnp.concatenate([-sn, sn, -sn, sn], axis=0).T)
        cos = jnp.concatenate(cos_rows, axis=0)
        sin_signed = jnp.concatenate(sin_rows, axis=0)
        qkv = st[j]["qkv"]
        q_tiles = _rope(qkv[:, c_q:c_k], cos, sin_signed, first_half)
        k_r = _rope(qkv[:, c_k:c_v], cos, sin_signed, first_half)[0]
        vt_new = qkv[:, c_v:c_za].T
        lo = BLOCK + j * R
        for g in range(N_KV_HEADS):
            k_scr[g, lo:lo + R, :] = k_r[:, g * HEAD_DIM:(g + 1) * HEAD_DIM].astype(BF16)
            vt_scr[g, :, lo:lo + R] = vt_new[g * HEAD_DIM:(g + 1) * HEAD_DIM, :].astype(BF16)
        q_heads = []
        for c in range(len(q_tiles)):
            qs = (q_tiles[c] * q_scale).astype(BF16)
            q_heads.append(qs[:, 0:HEAD_DIM])
            q_heads.append(qs[:, HEAD_DIM:2 * HEAD_DIM])
        ss = []
        for b in range(blk_per_sub):
            n = j * blk_per_sub + b
            band = slice(n * BLOCK, n * BLOCK + 2 * BLOCK)
            for g in range(N_KV_HEADS):
                kb = k_scr[g, band, :]
                q_stack = jnp.concatenate(
                    [q_heads[g * GQ + h][b * BLOCK:(b + 1) * BLOCK] for h in range(GQ)], axis=0)
                ss.append(lax.dot_general(kb, q_stack, (((1,), (1,)), ((), ())),
                                          preferred_element_type=F32))
        st[j]["scores"] = ss

    def pv_unit(j, idx):
        b, g = divmod(idx, N_KV_HEADS)
        n = j * blk_per_sub + b
        band = slice(n * BLOCK, n * BLOCK + 2 * BLOCK)
        s = st[j]["scores"][idx]
        valid = in_win & (kj >= first_lo) if n == 0 else in_win
        s = jnp.where(jnp.concatenate([valid] * GQ, axis=1), s, NEG)
        sink = jnp.concatenate(
            [jnp.full((1, BLOCK), sinks_ref[g * GQ + h] * LOG2E, F32) for h in range(GQ)], axis=1)
        m = jnp.maximum(jnp.max(s, axis=0, keepdims=True), sink)
        p = jnp.exp2(s - m)
        denom = jnp.sum(p, axis=0, keepdims=True) + jnp.exp2(sink - m)
        ot = jnp.dot(vt_scr[g, :, band], p.astype(BF16),
                     preferred_element_type=F32) * (1.0 / denom)
        for pair in range(GQ // 2):
            two = jnp.concatenate(
                [ot[:, (2 * pair) * BLOCK:(2 * pair + 1) * BLOCK],
                 ot[:, (2 * pair + 1) * BLOCK:(2 * pair + 2) * BLOCK]], axis=0)
            st[j]["attn"][b][g * (GQ // 2) + pair] = two.T

    def gating(j):
        rows_out = []
        for n in range(R // CHUNK):
            rows = slice(n * CHUNK, (n + 1) * CHUNK)
            cols_out = []
            for g in range(N_GROUPS):
                w_g = jnp.where(causal, ws_ref[g], 0.0).astype(BF16)
                bias = gate_bias[g]
                cols = slice(g * group_w, (g + 1) * group_w)
                sv = jnp.dot(w_g, st[j]["vn"][rows, cols], preferred_element_type=F32) + bias
                cols_out.append(st[j]["u"][rows, cols] * sv)
            rows_out.append(jnp.concatenate(cols_out, axis=1))
        st[j]["yb"] = jnp.concatenate(rows_out, axis=0)

    def branch_proj(j):
        attn = jnp.concatenate([jnp.concatenate(row, axis=1) for row in st[j]["attn"]], axis=0)
        y_a = attn.astype(BF16) * _silu_of_half(st[j]["za"]).astype(BF16)
        st[j]["pa"] = jnp.dot(y_a, wpa_ref[...], preferred_element_type=F32)
        y_b = st[j]["yb"].astype(BF16) * _silu_of_half(st[j]["zb"]).astype(BF16)
        st[j]["pb"] = jnp.dot(y_b, wpb_ref[...], preferred_element_type=F32)

    def out_proj(j):
        merged2 = (_two_sigmoid_of_half(st[j]["ga"]).astype(BF16) * st[j]["pa"].astype(BF16)
                   + _two_sigmoid_of_half(st[j]["gb"]).astype(BF16) * st[j]["pb"].astype(BF16))
        y = jnp.dot(merged2, wout_ref[...], preferred_element_type=F32)
        ms_y = jnp.mean(y * y, axis=-1, keepdims=True)
        rows = slice(j * R, (j + 1) * R)
        o_ref[0, rows, :] = x_ref[0, rows, :] + (y * lax.rsqrt(ms_y + EPS)) * post_gain
        st[j].clear()

    n_units = blk_per_sub * N_KV_HEADS
    for j in range(n_sub + 1):
        if j < n_sub:
            head(j)
            gmlp_in(j)
        if j >= 1:
            out_proj(j - 1)
        if j < n_sub:
            rope_and_scores(j)
            st[j]["attn"] = [[None] * (N_Q_HEADS // 2) for _ in range(blk_per_sub)]
            st[j]["za"] = proj(j, c_za, c_u)
            st[j]["zb"] = proj(j, c_zb, c_g)
            st[j]["ga"] = proj(j, c_g, c_g + d_model)
            for idx in range(n_units):
                pv_unit(j, idx)
            gating(j)
            st[j]["gb"] = proj(j, c_g + d_model, c_g + 2 * d_model)
            branch_proj(j)

    k_scr[:, 0:BLOCK, :] = k_scr[:, T:T + BLOCK, :]
    vt_scr[:, :, 0:BLOCK] = vt_scr[:, :, T:T + BLOCK]


def _const_spec(shape):
    return pl.BlockSpec(shape, lambda b, s: (0,) * len(shape), pipeline_mode=pl.Buffered(1))


def _layer(x, ada, positions, g_pre, g_post, w_in, sinks, ln_v_g, ln_v_b, w_s, b_s,
           w_proj_a, w_proj_b, w_out):
    B, S, D = x.shape
    T = SEQ_TILE
    assert S % T == 0 and T % SUB_TILE == 0 and SUB_TILE % BLOCK == 0 and SUB_TILE % CHUNK == 0
    d_a = N_Q_HEADS * HEAD_DIM
    d_kv = N_KV_HEADS * HEAD_DIM
    d_b = w_proj_b.shape[0]
    d_in = w_in.shape[1]
    assert d_in == 2 * d_a + 2 * d_kv + 3 * d_b + 2 * D

    half = HEAD_DIM // 2
    inv_freq = ROPE_THETA ** (-jnp.arange(half, dtype=F32) / half)
    invf = jnp.broadcast_to(inv_freq[:, None], (half, LANES))

    c_za, c_u = d_a + 2 * d_kv, 2 * d_a + 2 * d_kv
    c_zb = c_u + 2 * d_b
    col = jnp.arange(d_in)
    halved = ((col >= c_za) & (col < c_u)) | (col >= c_zb)
    inv_sqrt2 = 1.0 / math.sqrt(2.0)
    col_scale = jnp.where(halved, 0.5, jnp.where((col >= c_u) & (col < c_zb), inv_sqrt2, 1.0))
    w_in_b = (w_in * col_scale.astype(F32)[None, :]).astype(BF16)
    w_out_b = (w_out * 0.5).astype(BF16)

    kern = functools.partial(_block_kernel, d_a=d_a, d_kv=d_kv, d_b=d_b, d_model=D)
    return pl.pallas_call(
        kern,
        out_shape=jax.ShapeDtypeStruct((B, S, D), x.dtype),
        grid=(B, S // T),
        in_specs=[
            pl.BlockSpec(memory_space=pltpu.SMEM),
            pl.BlockSpec((1, T, D), lambda b, s: (b, s, 0)),
            pl.BlockSpec((B, T), lambda b, s: (0, s)),
            _const_spec((B, 3 * D)),
            _const_spec((1, D)),
            _const_spec((1, D)),
            _const_spec((half, LANES)),
            _const_spec((D, d_in)),
            _const_spec((1, d_b)),
            _const_spec((1, d_b)),
            _const_spec((N_GROUPS, CHUNK, CHUNK)),
            _const_spec((N_GROUPS, CHUNK)),
            _const_spec((d_a, D)),
            _const_spec((d_b, D)),
            _const_spec((D, D)),
        ],
        out_specs=pl.BlockSpec((1, T, D), lambda b, s: (b, s, 0)),
        scratch_shapes=[
            pltpu.VMEM((N_KV_HEADS, BLOCK + T, HEAD_DIM), BF16),
            pltpu.VMEM((N_KV_HEADS, HEAD_DIM, BLOCK + T), BF16),
        ],
        compiler_params=pltpu.CompilerParams(
            dimension_semantics=("arbitrary", "arbitrary"),
            allow_input_fusion=[i in (7, 12, 13, 14) for i in range(15)],
            vmem_limit_bytes=VMEM_LIMIT_BYTES),
        name="hybrid_block",
    )(sinks, x, positions, ada,
      g_pre[None, :], g_post[None, :], invf, w_in_b,
      ln_v_g[None, :], ln_v_b[None, :], w_s, b_s,
      w_proj_a.astype(BF16), (w_proj_b * inv_sqrt2).astype(BF16), w_out_b)


def _ada(c, w_ada, b_ada):
    B, D = c.shape
    n_out = w_ada.shape[1]
    tn = n_out // 2
    return pl.pallas_call(
        _ada_kernel,
        out_shape=jax.ShapeDtypeStruct((B, n_out), F32),
        grid=(n_out // tn,),
        in_specs=[
            pl.BlockSpec((B, D), lambda j: (0, 0)),
            pl.BlockSpec((D, tn), lambda j: (0, j)),
            pl.BlockSpec((1, tn), lambda j: (0, j)),
        ],
        out_specs=pl.BlockSpec((B, tn), lambda j: (0, j)),
        compiler_params=pltpu.CompilerParams(dimension_semantics=("arbitrary",)),
        name="adaln_modulation",
    )(c, w_ada, b_ada[None, :])


def kernel(x, c, positions, w_ada, b_ada, g_pre, g_post, w_in, sinks, ln_v_g, ln_v_b, w_s, b_s,
           w_proj_a, w_proj_b, w_out):
    depth = w_in.shape[0]
    for l in range(depth):
        ada = _ada(c, w_ada[l], b_ada[l])
        x = _layer(x, ada, positions, g_pre[l], g_post[l], w_in[l], sinks[l], ln_v_g[l],
                   ln_v_b[l], w_s[l], b_s[l], w_proj_a[l], w_proj_b[l], w_out[l])
    return x
```

```python
import functools
import math

import jax
import jax.numpy as jnp
from jax import lax
from jax.experimental import pallas as pl
from jax.experimental.pallas import tpu as pltpu

HEAD_DIM = 64
N_Q_HEADS = 8
N_KV_HEADS = 2
GQ = N_Q_HEADS // N_KV_HEADS
WINDOW = 128
BLOCK = 128
ROPE_THETA = 10000.0
CHUNK = 128
N_GROUPS = 4
EPS = 1e-6
NEG = -1e30
LOG2E = 1.4426950408889634

LANES = 128
SEQ_TILE = 1024
SUB_TILE = 256
VMEM_LIMIT_BYTES = 44 * 1024 * 1024

BF16 = jnp.bfloat16
F32 = jnp.float32


def _ada_kernel(c_ref, w_ref, b_ref, o_ref):
    c = c_ref[...]
    c_act = c * jax.nn.sigmoid(c)
    o_ref[...] = jnp.dot(c_act, w_ref[...], preferred_element_type=F32) + b_ref[...]


def _two_sigmoid_of_half(h):
    return jnp.tanh(h) + 1.0


def _silu_of_half(h):
    return h * _two_sigmoid_of_half(h)


def _gelu_times_sqrt2(t):
    return t * (1.0 + lax.erf(t))


def _rope(t, cos, sin_signed, first_half):
    outs = []
    for c in range(t.shape[1] // LANES):
        tc = t[:, c * LANES:(c + 1) * LANES]
        rot = jnp.where(first_half,
                        pltpu.roll(tc, LANES - HEAD_DIM // 2, axis=1),
                        pltpu.roll(tc, HEAD_DIM // 2, axis=1))
        outs.append(tc * cos + rot * sin_signed)
    return outs


def _block_kernel(sinks_ref, x_ref, pos_ref, ada_ref, gpre_ref, gpost_ref, invf_ref,
                  win_ref, lng_ref, lnb_ref, ws_ref, bs_ref, wpa_ref, wpb_ref, wout_ref,
                  o_ref, k_scr, vt_scr, *, d_a, d_kv, d_b, d_model):
    T = x_ref.shape[1]
    R = SUB_TILE
    n_sub = T // R
    blk_per_sub = R // BLOCK
    s_idx = pl.program_id(1)

    c_q, c_k, c_v = 0, d_a, d_a + d_kv
    c_za = c_v + d_kv
    c_u = c_za + d_a
    c_vb = c_u + d_b
    c_zb = c_vb + d_b
    c_g = c_zb + d_b

    @pl.when(s_idx == 0)
    def _():
        k_scr[:, 0:BLOCK, :] = jnp.zeros((N_KV_HEADS, BLOCK, HEAD_DIM), BF16)
        vt_scr[:, :, 0:BLOCK] = jnp.zeros((N_KV_HEADS, HEAD_DIM, BLOCK), BF16)

    b_idx = pl.program_id(0)
    ada = ada_ref[pl.ds(b_idx, 1), :]
    shift = ada[:, 0:d_model]
    scale = ada[:, d_model:2 * d_model]
    gate = ada[:, 2 * d_model:3 * d_model]
    pre_gain = gpre_ref[...] * (1.0 + scale)
    post_gain = gate * gpost_ref[...]
    invf = invf_ref[...]
    lane = lax.broadcasted_iota(jnp.int32, (1, LANES), 1)
    first_half = (lane % HEAD_DIM) < (HEAD_DIM // 2)

    kj = lax.broadcasted_iota(jnp.int32, (2 * BLOCK, BLOCK), 0)
    qi = lax.broadcasted_iota(jnp.int32, (2 * BLOCK, BLOCK), 1)
    rel = qi + BLOCK - kj
    in_win = (rel >= 0) & (rel < WINDOW)
    first_lo = jnp.where(s_idx == 0, BLOCK, 0)
    ti = lax.broadcasted_iota(jnp.int32, (CHUNK, CHUNK), 0)
    si = lax.broadcasted_iota(jnp.int32, (CHUNK, CHUNK), 1)
    causal = si <= ti
    group_w = d_b // N_GROUPS
    assert group_w == CHUNK
    gate_bias = [jnp.broadcast_to(bs_ref[g:g + 1, :], (CHUNK, CHUNK)).T for g in range(N_GROUPS)]
    q_scale = LOG2E / math.sqrt(HEAD_DIM)

    st = [dict() for _ in range(n_sub)]

    def proj(j, lo, hi):
        return jnp.dot(st[j]["hb"], win_ref[:, lo:hi], preferred_element_type=F32)

    def head(j):
        xc = x_ref[0, j * R:(j + 1) * R, :]
        ms = jnp.mean(xc * xc, axis=-1, keepdims=True)
        st[j]["hb"] = ((xc * lax.rsqrt(ms + EPS)) * pre_gain + shift).astype(BF16)
        st[j]["qkv"] = proj(j, c_q, c_za)

    def gmlp_in(j):
        v = _gelu_times_sqrt2(proj(j, c_vb, c_zb))
        mu = jnp.mean(v, axis=-1, keepdims=True)
        vc = v - mu
        var = jnp.mean(vc * vc, axis=-1, keepdims=True)
        st[j]["vn"] = (vc * lax.rsqrt(var + 2.0 * EPS) * lng_ref[...] + lnb_ref[...]).astype(BF16)
        st[j]["u"] = _gelu_times_sqrt2(proj(j, c_u, c_vb))

    def rope_and_scores(j):
        pos = pos_ref[pl.ds(b_idx, 1), j * R:(j + 1) * R].astype(F32)
        cos_rows, sin_rows = [], []
        for b in range(blk_per_sub):
            ang = invf * pos[:, b * BLOCK:(b + 1) * BLOCK]
            cs = jnp.cos(ang)
            sn = jnp.sin(ang)
            cos_rows.append(jnp.concatenate([cs, cs, cs, cs], axis=0).T)
            sin_rows.append(jnp.concatenate([-sn, sn, -sn, sn], axis=0).T)
        cos = jnp.concatenate(cos_rows, axis=0)
        sin_signed = jnp.concatenate(sin_rows, axis=0)
        qkv = st[j]["qkv"]
        q_tiles = _rope(qkv[:, c_q:c_k], cos, sin_signed, first_half)
        k_r = _rope(qkv[:, c_k:c_v], cos, sin_signed, first_half)[0]
        vt_new = qkv[:, c_v:c_za].T
        lo = BLOCK + j * R
        for g in range(N_KV_HEADS):
            k_scr[g, lo:lo + R, :] = k_r[:, g * HEAD_DIM:(g + 1) * HEAD_DIM].astype(BF16)
            vt_scr[g, :, lo:lo + R] = vt_new[g * HEAD_DIM:(g + 1) * HEAD_DIM, :].astype(BF16)
        q_heads = []
        for c in range(len(q_tiles)):
            qs = (q_tiles[c] * q_scale).astype(BF16)
            q_heads.append(qs[:, 0:HEAD_DIM])
            q_heads.append(qs[:, HEAD_DIM:2 * HEAD_DIM])
        ss = []
        for b in range(blk_per_sub):
            n = j * blk_per_sub + b
            band = slice(n * BLOCK, n * BLOCK + 2 * BLOCK)
            for g in range(N_KV_HEADS):
                kb = k_scr[g, band, :]
                q_stack = jnp.concatenate(
                    [q_heads[g * GQ + h][b * BLOCK:(b + 1) * BLOCK] for h in range(GQ)], axis=0)
                ss.append(lax.dot_general(kb, q_stack, (((1,), (1,)), ((), ())),
                                          preferred_element_type=F32))
        st[j]["scores"] = ss

    def pv_unit(j, idx):
        b, g = divmod(idx, N_KV_HEADS)
        n = j * blk_per_sub + b
        band = slice(n * BLOCK, n * BLOCK + 2 * BLOCK)
        s = st[j]["scores"][idx]
        valid = in_win & (kj >= first_lo) if n == 0 else in_win
        s = jnp.where(jnp.concatenate([valid] * GQ, axis=1), s, NEG)
        sink = jnp.concatenate(
            [jnp.full((1, BLOCK), sinks_ref[g * GQ + h] * LOG2E, F32) for h in range(GQ)], axis=1)
        m = jnp.maximum(jnp.max(s, axis=0, keepdims=True), sink)
        p = jnp.exp2(s - m)
        denom = jnp.sum(p, axis=0, keepdims=True) + jnp.exp2(sink - m)
        ot = jnp.dot(vt_scr[g, :, band], p.astype(BF16),
                     preferred_element_type=F32) * (1.0 / denom)
        for pair in range(GQ // 2):
            two = jnp.concatenate(
                [ot[:, (2 * pair) * BLOCK:(2 * pair + 1) * BLOCK],
                 ot[:, (2 * pair + 1) * BLOCK:(2 * pair + 2) * BLOCK]], axis=0)
            st[j]["attn"][b][g * (GQ // 2) + pair] = two.T

    def gating(j):
        rows_out = []
        for n in range(R // CHUNK):
            rows = slice(n * CHUNK, (n + 1) * CHUNK)
            cols_out = []
            for g in range(N_GROUPS):
                w_g = jnp.where(causal, ws_ref[g], 0.0).astype(BF16)
                bias = gate_bias[g]
                cols = slice(g * group_w, (g + 1) * group_w)
                sv = jnp.dot(w_g, st[j]["vn"][rows, cols], preferred_element_type=F32) + bias
                cols_out.append(st[j]["u"][rows, cols] * sv)
            rows_out.append(jnp.concatenate(cols_out, axis=1))
        st[j]["yb"] = jnp.concatenate(rows_out, axis=0)

    def branch_proj(j):
        attn = jnp.concatenate([jnp.concatenate(row, axis=1) for row in st[j]["attn"]], axis=0)
        y_a = attn.astype(BF16) * _silu_of_half(st[j]["za"]).astype(BF16)
        st[j]["pa"] = jnp.dot(y_a, wpa_ref[...], preferred_element_type=F32)
        y_b = st[j]["yb"].astype(BF16) * _silu_of_half(st[j]["zb"]).astype(BF16)
        st[j]["pb"] = jnp.dot(y_b, wpb_ref[...], preferred_element_type=F32)

    def out_proj(j):
        merged2 = (_two_sigmoid_of_half(st[j]["ga"]).astype(BF16) * st[j]["pa"].astype(BF16)
                   + _two_sigmoid_of_half(st[j]["gb"]).astype(BF16) * st[j]["pb"].astype(BF16))
        y = jnp.dot(merged2, wout_ref[...], preferred_element_type=F32)
        ms_y = jnp.mean(y * y, axis=-1, keepdims=True)
        rows = slice(j * R, (j + 1) * R)
        o_ref[0, rows, :] = x_ref[0, rows, :] + (y * lax.rsqrt(ms_y + EPS)) * post_gain
        st[j].clear()

    n_units = blk_per_sub * N_KV_HEADS
    for j in range(n_sub + 1):
        if j < n_sub:
            head(j)
            gmlp_in(j)
        if j >= 1:
            out_proj(j - 1)
        if j < n_sub:
            rope_and_scores(j)
            st[j]["attn"] = [[None] * (N_Q_HEADS // 2) for _ in range(blk_per_sub)]
            st[j]["za"] = proj(j, c_za, c_u)
            st[j]["zb"] = proj(j, c_zb, c_g)
            st[j]["ga"] = proj(j, c_g, c_g + d_model)
            for idx in range(n_units):
                pv_unit(j, idx)
            gating(j)
            st[j]["gb"] = proj(j, c_g + d_model, c_g + 2 * d_model)
            branch_proj(j)

    k_scr[:, 0:BLOCK, :] = k_scr[:, T:T + BLOCK, :]
    vt_scr[:, :, 0:BLOCK] = vt_scr[:, :, T:T + BLOCK]


def _const_spec(shape):
    return pl.BlockSpec(shape, lambda b, s: (0,) * len(shape), pipeline_mode=pl.Buffered(1))


def _layer(x, ada, positions, g_pre, g_post, w_in, sinks, ln_v_g, ln_v_b, w_s, b_s,
           w_proj_a, w_proj_b, w_out):
    B, S, D = x.shape
    T = SEQ_TILE
    assert S % T == 0 and T % SUB_TILE == 0 and SUB_TILE % BLOCK == 0 and SUB_TILE % CHUNK == 0
    d_a = N_Q_HEADS * HEAD_DIM
    d_kv = N_KV_HEADS * HEAD_DIM
    d_b = w_proj_b.shape[0]
    d_in = w_in.shape[1]
    assert d_in == 2 * d_a + 2 * d_kv + 3 * d_b + 2 * D

    half = HEAD_DIM // 2
    inv_freq = ROPE_THETA ** (-jnp.arange(half, dtype=F32) / half)
    invf = jnp.broadcast_to(inv_freq[:, None], (half, LANES))

    c_za, c_u = d_a + 2 * d_kv, 2 * d_a + 2 * d_kv
    c_zb = c_u + 2 * d_b
    col = jnp.arange(d_in)
    halved = ((col >= c_za) & (col < c_u)) | (col >= c_zb)
    inv_sqrt2 = 1.0 / math.sqrt(2.0)
    col_scale = jnp.where(halved, 0.5, jnp.where((col >= c_u) & (col < c_zb), inv_sqrt2, 1.0))
    w_in_b = (w_in * col_scale.astype(F32)[None, :]).astype(BF16)
    w_out_b = (w_out * 0.5).astype(BF16)

    kern = functools.partial(_block_kernel, d_a=d_a, d_kv=d_kv, d_b=d_b, d_model=D)
    return pl.pallas_call(
        kern,
        out_shape=jax.ShapeDtypeStruct((B, S, D), x.dtype),
        grid=(B, S // T),
        in_specs=[
            pl.BlockSpec(memory_space=pltpu.SMEM),
            pl.BlockSpec((1, T, D), lambda b, s: (b, s, 0)),
            pl.BlockSpec((B, T), lambda b, s: (0, s)),
            _const_spec((B, 3 * D)),
            _const_spec((1, D)),
            _const_spec((1, D)),
            _const_spec((half, LANES)),
            _const_spec((D, d_in)),
            _const_spec((1, d_b)),
            _const_spec((1, d_b)),
            _const_spec((N_GROUPS, CHUNK, CHUNK)),
            _const_spec((N_GROUPS, CHUNK)),
            _const_spec((d_a, D)),
            _const_spec((d_b, D)),
            _const_spec((D, D)),
        ],
        out_specs=pl.BlockSpec((1, T, D), lambda b, s: (b, s, 0)),
        scratch_shapes=[
            pltpu.VMEM((N_KV_HEADS, BLOCK + T, HEAD_DIM), BF16),
            pltpu.VMEM((N_KV_HEADS, HEAD_DIM, BLOCK + T), BF16),
        ],
        compiler_params=pltpu.CompilerParams(
            dimension_semantics=("arbitrary", "arbitrary"),
            allow_input_fusion=[i in (12, 13, 14) for i in range(15)],
            vmem_limit_bytes=VMEM_LIMIT_BYTES),
        name="hybrid_block",
    )(sinks, x, positions, ada,
      g_pre[None, :], g_post[None, :], invf, w_in_b,
      ln_v_g[None, :], ln_v_b[None, :], w_s, b_s,
      w_proj_a.astype(BF16), (w_proj_b * inv_sqrt2).astype(BF16), w_out_b)


def _ada(c, w_ada, b_ada):
    B, D = c.shape
    n_out = w_ada.shape[1]
    tn = n_out // 2
    return pl.pallas_call(
        _ada_kernel,
        out_shape=jax.ShapeDtypeStruct((B, n_out), F32),
        grid=(n_out // tn,),
        in_specs=[
            pl.BlockSpec((B, D), lambda j: (0, 0)),
            pl.BlockSpec((D, tn), lambda j: (0, j)),
            pl.BlockSpec((1, tn), lambda j: (0, j)),
        ],
        out_specs=pl.BlockSpec((B, tn), lambda j: (0, j)),
        compiler_params=pltpu.CompilerParams(dimension_semantics=("arbitrary",)),
        name="adaln_modulation",
    )(c, w_ada, b_ada[None, :])


def kernel(x, c, positions, w_ada, b_ada, g_pre, g_post, w_in, sinks, ln_v_g, ln_v_b, w_s, b_s,
           w_proj_a, w_proj_b, w_out):
    depth = w_in.shape[0]
    for l in range(depth):
        ada = _ada(c, w_ada[l], b_ada[l])
        x = _layer(x, ada, positions, g_pre[l], g_post[l], w_in[l], sinks[l], ln_v_g[l],
                   ln_v_b[l], w_s[l], b_s[l], w_proj_a[l], w_proj_b[l], w_out[l])
    return x
```
